```python
import math
import jax, jax.numpy as jnp
from jax import lax
import numpy as np

D_MODEL = 1024
BATCH = 8
SEQ = 4096
DEPTH = 4

CTX_LEN = 256
GRID_W = 64
N_MIXERS = 4
Q_BLOCK = 128
ROPE_THETA = 10000.0
NORM_EPS = 1e-6
N_MOD = 9
D_FF = 2816
NEG_INF = -1e30

DA_QK_DIM = 64
DA_HEADS = D_MODEL // (2 * DA_QK_DIM)
DA_V_DIM = 2 * DA_QK_DIM
DA_QKV = 2 * DA_HEADS * 2 * DA_QK_DIM + DA_HEADS * DA_V_DIM
GA_HEAD_DIM = 128
GA_HEADS = D_MODEL // GA_HEAD_DIM
GA_KV_HEADS = 2
MLA_HEADS = 16
MLA_Q_RANK = 256
MLA_KV_RANK = 128
MLA_NOPE_DIM = 64
MLA_ROPE_DIM = 32
MLA_V_DIM = 64
SWA_HEAD_DIM = 64
SWA_HEADS = D_MODEL // SWA_HEAD_DIM
SWA_KV_HEADS = 2
WINDOW = 128
BAND = Q_BLOCK + 2 * WINDOW

kernel_name = "hybrid_interleaved_diffusion_trunk"


def rmsnorm(x, g):
    xf = x.astype(jnp.float32)
    y = xf * lax.rsqrt(jnp.mean(xf * xf, axis=-1, keepdims=True) + NORM_EPS)
    return (y * g.astype(jnp.float32)).astype(x.dtype)


def axial_rope_tables(seq_len, rot_dim):
    rows = seq_len // GRID_W
    row = jnp.repeat(jnp.arange(rows, dtype=jnp.int32), GRID_W).astype(jnp.float32)
    col = jnp.tile(jnp.arange(GRID_W, dtype=jnp.int32), rows).astype(jnp.float32)
    n_axis = rot_dim // 4
    freqs = ROPE_THETA ** (-jnp.arange(n_axis, dtype=jnp.float32) / n_axis)
    ang = jnp.concatenate([row[:, None] * freqs, col[:, None] * freqs], axis=-1)
    return jnp.cos(ang), jnp.sin(ang)


def apply_rope(x, cos, sin):
    seq, half = cos.shape
    bshape = (seq,) + (1,) * (x.ndim - 3) + (half,)
    cs = cos.reshape(bshape).astype(x.dtype)
    sn = sin.reshape(bshape).astype(x.dtype)
    xp = x.reshape(x.shape[:-1] + (half, 2))
    x0, x1 = xp[..., 0], xp[..., 1]
    return jnp.stack([x0 * cs - x1 * sn, x0 * sn + x1 * cs], axis=-1).reshape(x.shape)


def sweep_query_blocks(fn, *qs):
    b, s = qs[0].shape[:2]
    nblk = s // Q_BLOCK
    blocks = tuple(jnp.moveaxis(q.reshape((b, nblk, Q_BLOCK) + q.shape[2:]), 1, 0) for q in qs)
    out = lax.map(lambda args: fn(*args), (jnp.arange(nblk),) + blocks)
    return jnp.moveaxis(out, 0, 1).reshape((b, s) + out.shape[3:])


def grouped_attend(q, k, v, scale):
    s = jnp.einsum('blhgd,bkhd->bhglk', q, k, preferred_element_type=jnp.float32) * scale
    p = jax.nn.softmax(s, axis=-1)
    o = jnp.einsum('bhglk,bkhd->blhgd', p.astype(v.dtype), v)
    return o.reshape(o.shape[:2] + (-1,))


def swiglu(h, w_in, w_out):
    gate, up = jnp.split(h @ w_in, 2, axis=-1)
    return (jax.nn.silu(gate) * up) @ w_out


def adaln(cond, w, b):
    m = jax.nn.silu(cond) @ w + b
    return m.reshape(m.shape[:-1] + (N_MOD, D_MODEL))


def modulated_norm(h, g, m, k):
    return rmsnorm(h, g) * (1.0 + m[..., k + 1, :]) + m[..., k, :]


def gated_residual(h, y, g, m, k, weight):
    return h + weight * m[..., k + 2, :] * rmsnorm(y, g)


def half_ffn(h, m, k, g_pre, g_post, w_in, w_out):
    y = swiglu(modulated_norm(h, g_pre, m, k), w_in, w_out)
    return gated_residual(h, y, g_post, m, k, 0.5)


def diff_attention_mixer(h, hc, w_in, lam, subln, w_out, lambda_init, ctx_out):
    b, s, _ = h.shape
    cos, sin = axial_rope_tables(s, DA_QK_DIM)

    def project(t):
        n = t.shape[1]
        q, k, v = jnp.split(t @ w_in, [DA_HEADS * 2 * DA_QK_DIM, 2 * DA_HEADS * 2 * DA_QK_DIM], axis=-1)
        return (q.reshape(b, n, DA_HEADS, 2, DA_QK_DIM), k.reshape(b, n, DA_HEADS, 2, DA_QK_DIM),
                v.reshape(b, n, DA_HEADS, DA_V_DIM))

    q, k, v = project(h)
    qc, kc, vc = project(hc)
    q, k = apply_rope(q, cos, sin), apply_rope(k, cos, sin)
    lf = lam.astype(jnp.float32)
    lam_full = jnp.exp(jnp.sum(lf[0] * lf[1])) - jnp.exp(jnp.sum(lf[2] * lf[3])) + lambda_init
    scale = DA_QK_DIM ** -0.5

    def attend(qb, kb, vb):
        sc = jnp.einsum('blhid,bkhid->bihlk', qb, kb, preferred_element_type=jnp.float32) * scale
        p = jax.nn.softmax(sc, axis=-1)
        p = p[:, 0] - lam_full * p[:, 1]
        o = jnp.einsum('bhlk,bkhd->blhd', p.astype(vb.dtype), vb)
        o = rmsnorm(o, subln) * (1.0 - lambda_init)
        return o.reshape(o.shape[0], o.shape[1], DA_HEADS * DA_V_DIM)

    k_all = jnp.concatenate([kc, k], axis=1)
    v_all = jnp.concatenate([vc, v], axis=1)
    y = sweep_query_blocks(lambda i, qb: attend(qb, k_all, v_all), q) @ w_out
    yc = attend(qc, kc, vc) @ w_out if ctx_out else None
    return y, yc


def gqa_axial_mixer(h, hc, w_in, q_norm, k_norm, w_out, ctx_out):
    b, s, _ = h.shape
    cos, sin = axial_rope_tables(s, GA_HEAD_DIM)
    grp = GA_HEADS // GA_KV_HEADS

    def project(t):
        n = t.shape[1]
        q, k, v = jnp.split(t @ w_in, [GA_HEADS * GA_HEAD_DIM, (GA_HEADS + GA_KV_HEADS) * GA_HEAD_DIM], axis=-1)
        q = rmsnorm(q.reshape(b, n, GA_KV_HEADS, grp, GA_HEAD_DIM), q_norm)
        k = rmsnorm(k.reshape(b, n, GA_KV_HEADS, GA_HEAD_DIM), k_norm)
        return q, k, v.reshape(b, n, GA_KV_HEADS, GA_HEAD_DIM)

    q, k, v = project(h)
    qc, kc, vc = project(hc)
    q, k = apply_rope(q, cos, sin), apply_rope(k, cos, sin)
    scale = GA_HEAD_DIM ** -0.5
    k_all = jnp.concatenate([kc, k], axis=1)
    v_all = jnp.concatenate([vc, v], axis=1)
    y = sweep_query_blocks(lambda i, qb: grouped_attend(qb, k_all, v_all, scale), q) @ w_out
    yc = grouped_attend(qc, kc, vc, scale) @ w_out if ctx_out else None
    return y, yc


def mla_mixer(h, hc, w_in, q_norm, kv_norm, w_uq, w_ukv, w_out, ctx_out):
    b, s, _ = h.shape
    cos, sin = axial_rope_tables(s, MLA_ROPE_DIM)

    def project(t):
        n = t.shape[1]
        cq, ckv, kr = jnp.split(t @ w_in, [MLA_Q_RANK, MLA_Q_RANK + MLA_KV_RANK], axis=-1)
        q = (rmsnorm(cq, q_norm) @ w_uq).reshape(b, n, MLA_HEADS, MLA_NOPE_DIM + MLA_ROPE_DIM)
        kv = (rmsnorm(ckv, kv_norm) @ w_ukv).reshape(b, n, MLA_HEADS, MLA_NOPE_DIM + MLA_V_DIM)
        qn, qr = jnp.split(q, [MLA_NOPE_DIM], axis=-1)
        kn, v = jnp.split(kv, [MLA_NOPE_DIM], axis=-1)
        return qn, qr, kn, kr, v

    qn, qr, kn, kr, v = project(h)
    qnc, qrc, knc, krc, vc = project(hc)
    qr, kr = apply_rope(qr, cos, sin), apply_rope(kr, cos, sin)
    scale = (MLA_NOPE_DIM + MLA_ROPE_DIM) ** -0.5

    def attend(qnb, qrb, knb, krb, vb):
        sc = (jnp.einsum('blhd,bkhd->bhlk', qnb, knb, preferred_element_type=jnp.float32)
              + jnp.einsum('blhd,bkd->bhlk', qrb, krb, preferred_element_type=jnp.float32)) * scale
        p = jax.nn.softmax(sc, axis=-1)
        o = jnp.einsum('bhlk,bkhd->blhd', p.astype(vb.dtype), vb)
        return o.reshape(o.shape[0], o.shape[1], MLA_HEADS * MLA_V_DIM)

    kn_all = jnp.concatenate([knc, kn], axis=1)
    kr_all = jnp.concatenate([krc, kr], axis=1)
    v_all = jnp.concatenate([vc, v], axis=1)
    y = sweep_query_blocks(lambda i, a, r: attend(a, r, kn_all, kr_all, v_all), qn, qr) @ w_out
    yc = attend(qnc, qrc, knc, krc, vc) @ w_out if ctx_out else None
    return y, yc


def window_sink_mixer(h, hc, w_in, sink, w_out, ctx_out):
    b, s, _ = h.shape
    cos, sin = axial_rope_tables(s, SWA_HEAD_DIM)
    grp = SWA_HEADS // SWA_KV_HEADS

    def project(t):
        n = t.shape[1]
        q, k, v = jnp.split(t @ w_in, [SWA_HEADS * SWA_HEAD_DIM, (SWA_HEADS + SWA_KV_HEADS) * SWA_HEAD_DIM], axis=-1)
        return (q.reshape(b, n, SWA_KV_HEADS, grp, SWA_HEAD_DIM), k.reshape(b, n, SWA_KV_HEADS, SWA_HEAD_DIM),
                v.reshape(b, n, SWA_KV_HEADS, SWA_HEAD_DIM))

    q, k, v = project(h)
    qc, kc, vc = project(hc)
    q, k = apply_rope(q, cos, sin), apply_rope(k, cos, sin)
    scale = SWA_HEAD_DIM ** -0.5
    sink_l = sink.astype(jnp.float32).reshape(SWA_KV_HEADS, grp)[None, :, :, None, None]

    def sink_softmax_mix(sc, vals):
        sc = jnp.concatenate([sc, jnp.broadcast_to(sink_l, sc.shape[:-1] + (1,))], axis=-1)
        p = jax.nn.softmax(sc, axis=-1)[..., :-1]
        o = jnp.einsum('bhglk,bkhd->blhgd', p.astype(vals.dtype), vals)
        return o.reshape(o.shape[:2] + (-1,))

    pad = ((0, 0), (WINDOW, WINDOW), (0, 0), (0, 0))
    k_pad, v_pad = jnp.pad(k, pad), jnp.pad(v, pad)

    def latent_block(i, qb):
        start = i * Q_BLOCK
        kb = lax.dynamic_slice_in_dim(k_pad, start, BAND, axis=1)
        vb = lax.dynamic_slice_in_dim(v_pad, start, BAND, axis=1)
        qpos = start + jnp.arange(Q_BLOCK)
        kpos = start - WINDOW + jnp.arange(BAND)
        allowed = (kpos[None, :] >= 0) & (kpos[None, :] < s) & (jnp.abs(qpos[:, None] - kpos[None, :]) <= WINDOW)
        s_band = jnp.einsum('blhgd,bkhd->bhglk', qb, kb, preferred_element_type=jnp.float32) * scale
        s_band = jnp.where(allowed, s_band, NEG_INF)
        s_ctx = jnp.einsum('blhgd,bkhd->bhglk', qb, kc, preferred_element_type=jnp.float32) * scale
        return sink_softmax_mix(jnp.concatenate([s_ctx, s_band], axis=-1), jnp.concatenate([vc, vb], axis=1))

    y = sweep_query_blocks(latent_block, q) @ w_out
    yc = None
    if ctx_out:
        s_cc = jnp.einsum('blhgd,bkhd->bhglk', qc, kc, preferred_element_type=jnp.float32) * scale
        yc = sink_softmax_mix(s_cc, vc) @ w_out
    return y, yc


def setup_inputs(seed: int = 0) -> dict:
    key = jax.random.key(seed)
    ks = iter(jax.random.split(key, 32))
    f32 = jnp.float32

    def nrm(shape, scale=1.0):
        return jax.random.normal(next(ks), shape, f32) * scale

    def w(shape, fan_in):
        return nrm(shape, fan_in ** -0.5)

    def gain(shape):
        return 1.0 + nrm(shape, 0.02)

    n0, n1, n2, n3 = [len(range(k, DEPTH, N_MIXERS)) for k in range(N_MIXERS)]
    return {
        "x": nrm((BATCH, SEQ, D_MODEL)),
        "c": nrm((BATCH, D_MODEL)),
        "ctx": nrm((BATCH, CTX_LEN, D_MODEL)),
        "c_ctx": nrm((D_MODEL,)),
        "ada_w": w((DEPTH, D_MODEL, N_MOD * D_MODEL), D_MODEL),
        "ada_b": nrm((DEPTH, N_MOD * D_MODEL), 0.02),
        "norm_g": gain((DEPTH, 6, D_MODEL)),
        "ffn_w_in": w((DEPTH, 2, D_MODEL, 2 * D_FF), D_MODEL),
        "ffn_w_out": w((DEPTH, 2, D_FF, D_MODEL), D_FF),
        "da_w_in": w((n0, D_MODEL, DA_QKV), D_MODEL),
        "da_lambda": nrm((n0, 4, DA_QK_DIM), 0.1),
        "da_subln": gain((n0, DA_V_DIM)),
        "da_w_out": w((n0, DA_HEADS * DA_V_DIM, D_MODEL), DA_HEADS * DA_V_DIM),
        "ga_w_in": w((n1, D_MODEL, (GA_HEADS + 2 * GA_KV_HEADS) * GA_HEAD_DIM), D_MODEL),
        "ga_q_norm": gain((n1, GA_HEAD_DIM)),
        "ga_k_norm": gain((n1, GA_HEAD_DIM)),
        "ga_w_out": w((n1, GA_HEADS * GA_HEAD_DIM, D_MODEL), GA_HEADS * GA_HEAD_DIM),
        "mla_w_in": w((n2, D_MODEL, MLA_Q_RANK + MLA_KV_RANK + MLA_ROPE_DIM), D_MODEL),
        "mla_q_norm": gain((n2, MLA_Q_RANK)),
        "mla_kv_norm": gain((n2, MLA_KV_RANK)),
        "mla_w_uq": w((n2, MLA_Q_RANK, MLA_HEADS * (MLA_NOPE_DIM + MLA_ROPE_DIM)), MLA_Q_RANK),
        "mla_w_ukv": w((n2, MLA_KV_RANK, MLA_HEADS * (MLA_NOPE_DIM + MLA_V_DIM)), MLA_KV_RANK),
        "mla_w_out": w((n2, MLA_HEADS * MLA_V_DIM, D_MODEL), MLA_HEADS * MLA_V_DIM),
        "swa_w_in": w((n3, D_MODEL, (SWA_HEADS + 2 * SWA_KV_HEADS) * SWA_HEAD_DIM), D_MODEL),
        "swa_sink": nrm((n3, SWA_HEADS), 0.5),
        "swa_w_out": w((n3, SWA_HEADS * SWA_HEAD_DIM, D_MODEL), SWA_HEADS * SWA_HEAD_DIM),
    }


def reference(x, c, ctx, c_ctx, ada_w, ada_b, norm_g, ffn_w_in, ffn_w_out,
              da_w_in, da_lambda, da_subln, da_w_out,
              ga_w_in, ga_q_norm, ga_k_norm, ga_w_out,
              mla_w_in, mla_q_norm, mla_kv_norm, mla_w_uq, mla_w_ukv, mla_w_out,
              swa_w_in, swa_sink, swa_w_out):
    h, hc = x, ctx
    for i in range(DEPTH):
        kind, occ = i % N_MIXERS, i // N_MIXERS
        ctx_out = i < DEPTH - 1
        m = adaln(c, ada_w[i], ada_b[i])[:, None]
        mc = adaln(c_ctx, ada_w[i], ada_b[i])[None]
        g = norm_g[i]
        h = half_ffn(h, m, 0, g[0], g[1], ffn_w_in[i, 0], ffn_w_out[i, 0])
        hc = half_ffn(hc, mc, 0, g[0], g[1], ffn_w_in[i, 0], ffn_w_out[i, 0])
        u = modulated_norm(h, g[2], m, 3)
        uc = modulated_norm(hc, g[2], mc, 3)
        if kind == 0:
            lambda_init = 0.8 - 0.6 * math.exp(-0.3 * i)
            y, yc = diff_attention_mixer(u, uc, da_w_in[occ], da_lambda[occ], da_subln[occ], da_w_out[occ],
                                         lambda_init, ctx_out)
        elif kind == 1:
            y, yc = gqa_axial_mixer(u, uc, ga_w_in[occ], ga_q_norm[occ], ga_k_norm[occ], ga_w_out[occ], ctx_out)
        elif kind == 2:
            y, yc = mla_mixer(u, uc, mla_w_in[occ], mla_q_norm[occ], mla_kv_norm[occ], mla_w_uq[occ],
                              mla_w_ukv[occ], mla_w_out[occ], ctx_out)
        else:
            y, yc = window_sink_mixer(u, uc, swa_w_in[occ], swa_sink[occ], swa_w_out[occ], ctx_out)
        h = gated_residual(h, y, g[3], m, 3, 1.0)
        h = half_ffn(h, m, 6, g[4], g[5], ffn_w_in[i, 1], ffn_w_out[i, 1])
        if ctx_out:
            hc = gated_residual(hc, yc, g[3], mc, 3, 1.0)
            hc = half_ffn(hc, mc, 6, g[4], g[5], ffn_w_in[i, 1], ffn_w_out[i, 1])
    return h
```

```python
import functools
import math

import numpy as np
import jax
import jax.numpy as jnp
from jax import lax
from jax.experimental import pallas as pl
from jax.experimental.pallas import tpu as pltpu

F32 = jnp.float32
BF16 = jnp.bfloat16

D_MODEL = 1024
D_FF = 2816
N_MOD = 9
NORM_EPS = 1e-6
GRID_W = 64
ROPE_THETA = 10000.0
NEG_INF = -1e30
WINDOW = 128

LANES = 128
HALF = LANES // 2
ROW_TILE = 256
FF_CHUNK = 256
MOD_ROWS = 16
VMEM_LIMIT = 52 * 1024 * 1024

DA_HEADS, DA_QK = 8, 64
GA_HEADS, GA_KV, GA_DIM = 8, 2, 128
MLA_HEADS, MLA_Q_RANK, MLA_KV_RANK, MLA_NOPE, MLA_ROPE, MLA_V = 16, 256, 128, 64, 32, 64
SWA_HEADS, SWA_KV, SWA_DIM = 16, 2, 64


def _cparams(n_axes):
    return pltpu.CompilerParams(dimension_semantics=("arbitrary",) * n_axes,
                                vmem_limit_bytes=VMEM_LIMIT)


def _rms(x, g):
    return x * lax.rsqrt(jnp.mean(x * x, axis=-1, keepdims=True) + NORM_EPS) * g


def _mod(mod_ref, k):
    return mod_ref[:, k * D_MODEL:(k + 1) * D_MODEL]


def _modnorm(x, g, mod_ref, k):
    return _rms(x, g) * (1.0 + _mod(mod_ref, k + 1)) + _mod(mod_ref, k)


def _rope(x, c, s):
    return x * c + pltpu.roll(x, HALF, 1) * s


def _ada_kernel(c_ref, w_ref, b_ref, o_ref):
    c = c_ref[...]
    sc = c / (1.0 + jnp.exp(-c))
    o_ref[...] = jnp.dot(sc, w_ref[...], precision=lax.Precision.HIGHEST,
                         preferred_element_type=F32) + b_ref[...]


def _ada_call(cond, ada_w, ada_b):
    depth, d, n = ada_w.shape
    cw = 1024
    return pl.pallas_call(
        _ada_kernel,
        grid=(depth, n // cw),
        in_specs=[pl.BlockSpec((MOD_ROWS, d), lambda l, j: (0, 0)),
                  pl.BlockSpec((None, d, cw), lambda l, j: (l, 0, j)),
                  pl.BlockSpec((None, 1, cw), lambda l, j: (l, 0, j))],
        out_specs=pl.BlockSpec((None, MOD_ROWS, cw), lambda l, j: (l, 0, j)),
        out_shape=jax.ShapeDtypeStruct((depth, MOD_ROWS, n), F32),
        compiler_params=_cparams(2),
        name="adaln",
    )(cond, ada_w, ada_b.reshape(depth, 1, n))


class _Geom:
    def __init__(self, batch, seq, ctx):
        self.batch, self.seq, self.ctx = batch, seq, ctx
        self.n_lat_rows = batch * seq
        self.n_rows = batch * (seq + ctx)
        self.n_lat = self.n_lat_rows // ROW_TILE
        self.n_tiles = self.n_rows // ROW_TILE
        self.tiles_per_batch = seq // ROW_TILE

    def mod_row(self, t):
        return jnp.where(t < self.n_lat, t // self.tiles_per_batch, self.batch)

    def rope_blk(self, t):
        return jnp.where(t < self.n_lat, t % self.tiles_per_batch, self.tiles_per_batch)


def _const_spec(shape):
    nd = len(shape)
    return pl.BlockSpec(shape, lambda *_: (0,) * nd, pipeline_mode=pl.Buffered(1))


def _ffn_kernel(*refs, k, gi, n_lat, has_pre, has_ctx):
    it = iter(refs)
    x_ref, mod_ref, g_ref, win_ref, wout_ref = (next(it) for _ in range(5))
    x = x_ref[...]
    if has_pre:
        ol_ref = next(it)
        oc_ref = next(it) if has_ctx else None
        wmo_ref = next(it)
        nblk = ol_ref.shape[0]
        o = jnp.concatenate([ol_ref[i] for i in range(nblk)], axis=1)
        if has_ctx:
            oc = jnp.concatenate([oc_ref[i] for i in range(nblk)], axis=1)
            o = jnp.where(pl.program_id(0) < n_lat, o, oc)
        y = jnp.dot(o, wmo_ref[...], preferred_element_type=F32)
        x = x + _mod(mod_ref, 5) * _rms(y, g_ref[3:4, :])
    out_ref = next(it)

    u = _modnorm(x, g_ref[gi:gi + 1, :], mod_ref, k).astype(BF16)
    acc = None
    for c in range(D_FF // FF_CHUNK):
        lo = c * FF_CHUNK
        a = jnp.dot(u, win_ref[:, lo:lo + FF_CHUNK], preferred_element_type=F32)
        b = jnp.dot(u, win_ref[:, D_FF + lo:D_FF + lo + FF_CHUNK], preferred_element_type=F32)
        act = (a / (1.0 + jnp.exp(-a)) * b).astype(BF16)
        part = jnp.dot(act, wout_ref[lo:lo + FF_CHUNK, :], preferred_element_type=F32)
        acc = part if acc is None else acc + part
    out_ref[...] = x + 0.5 * _mod(mod_ref, k + 2) * _rms(acc, g_ref[gi + 1:gi + 2, :])


def _ffn_call(geom, h, mod, g8, w_in, w_out, *, k, gi, pre=None, n_tiles=None, name):
    n_tiles = geom.n_tiles if n_tiles is None else n_tiles
    d = D_MODEL
    in_specs = [pl.BlockSpec((ROW_TILE, d), lambda t: (t, 0)),
                pl.BlockSpec((None, 1, N_MOD * d), lambda t: (geom.mod_row(t), 0, 0)),
                _const_spec(g8.shape), _const_spec(w_in.shape), _const_spec(w_out.shape)]
    args = [h, mod, g8, w_in, w_out]
    has_pre = pre is not None
    has_ctx = False
    if has_pre:
        o_lat, o_ctx, w_mo = pre
        nblk = o_lat.shape[0]
        in_specs.append(pl.BlockSpec((nblk, ROW_TILE, LANES),
                                     lambda t: (0, jnp.minimum(t, geom.n_lat - 1), 0)))
        args.append(o_lat)
        has_ctx = o_ctx is not None
        if has_ctx:
            in_specs.append(pl.BlockSpec((nblk, ROW_TILE, LANES),
                                         lambda t: (0, jnp.maximum(t - geom.n_lat, 0), 0)))
            args.append(o_ctx)
        in_specs.append(_const_spec(w_mo.shape))
        args.append(w_mo)
    kern = functools.partial(_ffn_kernel, k=k, gi=gi, n_lat=geom.n_lat, has_pre=has_pre, has_ctx=has_ctx)
    return pl.pallas_call(
        kern,
        grid=(n_tiles,),
        in_specs=in_specs,
        out_specs=pl.BlockSpec((ROW_TILE, d), lambda t: (t, 0)),
        out_shape=jax.ShapeDtypeStruct((n_tiles * ROW_TILE, d), F32),
        compiler_params=_cparams(1),
        name=name,
    )(*args)


def _store_blocks(dst_ref, x, n, fn=None):
    for i in range(n):
        blk = x[:, i * LANES:(i + 1) * LANES]
        if fn is not None:
            blk = fn(blk)
        dst_ref[i] = blk.astype(BF16)


def _proj0_kernel(x_ref, mod_ref, g_ref, w_ref, cq_ref, sq_ref, ck_ref, sk_ref, q_ref, k_ref, v_ref, *, nq, nk, nv):
    u = _modnorm(x_ref[...], g_ref[2:3, :], mod_ref, 3).astype(BF16)
    cq, sq, ck, sk = cq_ref[...], sq_ref[...], ck_ref[...], sk_ref[...]
    t = jnp.dot(u, w_ref[...], preferred_element_type=F32)
    _store_blocks(q_ref, t, nq, lambda b: _rope(b, cq, sq))
    _store_blocks(k_ref, t[:, nq * LANES:], nk, lambda b: _rope(b, ck, sk))
    _store_blocks(v_ref, t[:, (nq + nk) * LANES:], nv)


def _proj1_kernel(x_ref, mod_ref, g_ref, w_ref, qn_ref, kn_ref, cq_ref, sq_ref, ck_ref, sk_ref,
                  q_ref, k_ref, v_ref, *, nq, nk, nv):
    u = _modnorm(x_ref[...], g_ref[2:3, :], mod_ref, 3).astype(BF16)
    cq, sq, ck, sk = cq_ref[...], sq_ref[...], ck_ref[...], sk_ref[...]
    qn, kn = qn_ref[...], kn_ref[...]
    t = jnp.dot(u, w_ref[...], preferred_element_type=F32)
    _store_blocks(q_ref, t, nq, lambda b: _rope(_rms(b, qn), cq, sq))
    _store_blocks(k_ref, t[:, nq * LANES:], nk, lambda b: _rope(_rms(b, kn), ck, sk))
    _store_blocks(v_ref, t[:, (nq + nk) * LANES:], nv)


def _proj2_kernel(x_ref, mod_ref, g_ref, w1_ref, qn_ref, kvn_ref, wuq_ref, wuk_ref, wuv_ref,
                  cq_ref, sq_ref, ck_ref, sk_ref, q_ref, k_ref, v_ref):
    u = _modnorm(x_ref[...], g_ref[2:3, :], mod_ref, 3).astype(BF16)
    cq, sq, ck, sk = cq_ref[...], sq_ref[...], ck_ref[...], sk_ref[...]
    t = jnp.dot(u, w1_ref[...], preferred_element_type=F32)
    cqv = _rms(t[:, :MLA_Q_RANK], qn_ref[...]).astype(BF16)
    ckv = _rms(t[:, MLA_Q_RANK:MLA_Q_RANK + MLA_KV_RANK], kvn_ref[...]).astype(BF16)
    kr = _rope(t[:, MLA_Q_RANK + MLA_KV_RANK:], ck, sk)
    qa = jnp.dot(cqv, wuq_ref[...], preferred_element_type=F32)
    _store_blocks(q_ref, qa, MLA_HEADS, lambda b: _rope(b, cq, sq))
    ka = jnp.dot(ckv, wuk_ref[...], preferred_element_type=F32)
    _store_blocks(k_ref, ka, MLA_HEADS, lambda b: b + kr)
    va = jnp.dot(ckv, wuv_ref[...], preferred_element_type=F32)
    _store_blocks(v_ref, va, MLA_HEADS // 2)


def _proj_call(geom, kern, h, mod, g8, consts, tables, nq, nk, nv, name):
    d = D_MODEL
    tile = lambda t: (t, 0)
    rope_spec = pl.BlockSpec((ROW_TILE, LANES), lambda t: (geom.rope_blk(t), 0))
    in_specs = ([pl.BlockSpec((ROW_TILE, d), tile),
                 pl.BlockSpec((None, 1, N_MOD * d), lambda t: (geom.mod_row(t), 0, 0)),
                 _const_spec(g8.shape)]
                + [_const_spec(c.shape) for c in consts]
                + [rope_spec] * 4)
    blk = lambda n: pl.BlockSpec((n, ROW_TILE, LANES), lambda t: (0, t, 0))
    shp = lambda n: jax.ShapeDtypeStruct((n, geom.n_rows, LANES), BF16)
    return pl.pallas_call(
        kern,
        grid=(geom.n_tiles,),
        in_specs=in_specs,
        out_specs=[blk(nq), blk(nk), blk(nv)],
        out_shape=[shp(nq), shp(nk), shp(nv)],
        compiler_params=_cparams(1),
        name=name,
    )(h, mod, g8, *consts, *tables)


def _flash(qs, chunks, init=None, mask_fn=None):
    if init is None:
        m = l = acc = None
    else:
        m, l, acc = init
    for idx, (k_ref, hk, v_ref, hv, start, size) in enumerate(chunks):
        k = k_ref[hk, pl.ds(start, size), :]
        s = lax.dot_general(qs, k, (((1,), (1,)), ((), ())), preferred_element_type=F32)
        if mask_fn is not None:
            s = mask_fn(idx, s)
        mc = jnp.max(s, axis=1, keepdims=True)
        m_new = mc if m is None else jnp.maximum(m, mc)
        p = jnp.exp(s - m_new)
        ps = jnp.sum(p, axis=1, keepdims=True)
        pv = jnp.dot(p.astype(BF16), v_ref[hv, pl.ds(start, size), :], preferred_element_type=F32)
        if m is None:
            l, acc = ps, pv
        else:
            alpha = jnp.exp(m - m_new)
            l = alpha * l + ps
            acc = alpha * acc + pv
        m = m_new
    return acc, l


def _kv_chunks(kc_ref, vc_ref, kl_ref, vl_ref, hk, hv, n_ctx_keys, n_lat_keys, tk):
    chunks = [(kc_ref, hk, vc_ref, hv, 0, n_ctx_keys)]
    if kl_ref is not None:
        chunks += [(kl_ref, hk, vl_ref, hv, j * tk, tk) for j in range(n_lat_keys // tk)]
    return chunks


def _split_refs(refs, n_lead, has_lat):
    lead = refs[:n_lead]
    rest = refs[n_lead:]
    q_ref, kc_ref, vc_ref = rest[:3]
    if has_lat:
        kl_ref, vl_ref, o_ref = rest[3:6]
    else:
        kl_ref = vl_ref = None
        o_ref = rest[3]
    return lead, q_ref, kc_ref, vc_ref, kl_ref, vl_ref, o_ref


def _lane_lo(rows):
    lane = lax.broadcasted_iota(jnp.int32, (rows, LANES), 1)
    return (lane % HALF) < (HALF // 2)


def _attn_diff_kernel(*refs, has_lat, n_heads, n_ctx_keys, n_lat_keys, tk, lam_init):
    (lam_ref, subln_ref), q_ref, kc_ref, vc_ref, kl_ref, vl_ref, o_ref = _split_refs(refs, 2, has_lat)
    tq = q_ref.shape[1]
    lo = _lane_lo(tq)
    lf = lam_ref[...]
    lam = (jnp.exp(jnp.sum(lf[0:1] * lf[1:2], keepdims=True))
           - jnp.exp(jnp.sum(lf[2:3] * lf[3:4], keepdims=True)) + lam_init)
    subln = subln_ref[...]

    def head(h, carry):
        q = q_ref[h]
        zero = jnp.zeros_like(q)
        qs = jnp.concatenate([jnp.where(lo, q, zero), jnp.where(lo, zero, q)], axis=0)
        acc, l = _flash(qs, _kv_chunks(kc_ref, vc_ref, kl_ref, vl_ref, h, h, n_ctx_keys, n_lat_keys, tk))
        o = acc / l
        o = o[:tq] - lam * o[tq:]
        o_ref[h] = (_rms(o, subln) * (1.0 - lam_init)).astype(BF16)
        return carry

    lax.fori_loop(0, n_heads, head, 0)


def _attn_gqa_kernel(*refs, has_lat, grp, n_ctx_keys, n_lat_keys, tk):
    _, q_ref, kc_ref, vc_ref, kl_ref, vl_ref, o_ref = _split_refs(refs, 0, has_lat)
    tq = q_ref.shape[1]
    qs = jnp.concatenate([q_ref[j] for j in range(grp)], axis=0)
    acc, l = _flash(qs, _kv_chunks(kc_ref, vc_ref, kl_ref, vl_ref, 0, 0, n_ctx_keys, n_lat_keys, tk))
    o = acc / l
    for j in range(grp):
        o_ref[j] = o[j * tq:(j + 1) * tq].astype(BF16)


def _attn_mla_kernel(*refs, has_lat, n_pairs, n_ctx_keys, n_lat_keys, tk):
    _, q_ref, kc_ref, vc_ref, kl_ref, vl_ref, o_ref = _split_refs(refs, 0, has_lat)
    tq = q_ref.shape[1]
    lane = lax.broadcasted_iota(jnp.int32, (tq, LANES), 1)

    def pair(p, carry):
        outs = []
        for e in range(2):
            h = 2 * p + e
            acc, l = _flash(q_ref[h], _kv_chunks(kc_ref, vc_ref, kl_ref, vl_ref, h, p, n_ctx_keys, n_lat_keys, tk))
            outs.append(acc / l)
        o_ref[p] = jnp.where(lane < HALF, outs[0], outs[1]).astype(BF16)
        return carry

    lax.fori_loop(0, n_pairs, pair, 0)


def _attn_swa_kernel(sink_ref, q_ref, kc_ref, vc_ref, kl_ref, vl_ref, o_ref, *, n_pairs, n_ctx_keys, seq, win):
    tq = q_ref.shape[1]
    i = pl.program_id(2)
    start = jnp.clip(i * tq - WINDOW, 0, seq - win)
    start = pl.multiple_of(start, WINDOW)
    lo = _lane_lo(tq)
    lane = lax.broadcasted_iota(jnp.int32, (tq, LANES), 1)
    row = lax.broadcasted_iota(jnp.int32, (2 * tq, win), 0)
    col = lax.broadcasted_iota(jnp.int32, (2 * tq, win), 1)
    qpos = i * tq + jnp.where(row >= tq, row - tq, row)
    allowed = jnp.abs(qpos - (start + col)) <= WINDOW

    def mask_fn(idx, s):
        return s if idx == 0 else jnp.where(allowed, s, NEG_INF)

    chunks = [(kc_ref, 0, vc_ref, 0, 0, n_ctx_keys), (kl_ref, 0, vl_ref, 0, start, win)]
    for j in range(n_pairs):
        q = q_ref[j]
        zero = jnp.zeros_like(q)
        qs = jnp.concatenate([jnp.where(lo, q, zero), jnp.where(lo, zero, q)], axis=0)
        m0 = jnp.concatenate([jnp.broadcast_to(sink_ref[2 * j:2 * j + 1, 0:1], (tq, 1)),
                              jnp.broadcast_to(sink_ref[2 * j + 1:2 * j + 2, 0:1], (tq, 1))], axis=0)
        init = (m0, jnp.ones((2 * tq, 1), F32), jnp.zeros((2 * tq, LANES), F32))
        acc, l = _flash(qs, chunks, init=init, mask_fn=mask_fn)
        o = acc / l
        o_ref[j] = jnp.where(lane < HALF, o[:tq], o[tq:]).astype(BF16)


def _attn_call(geom, kern, lead, lead_specs, q, k, v, *, has_lat, tq, q_per, k_per, v_per, o_per, n_groups, name):
    b, s, c = geom.batch, geom.seq, geom.ctx
    ctx_blk0 = geom.n_lat_rows // c
    if has_lat:
        n_q = s // tq
        q_map = lambda bi, g, i: (g, bi * n_q + i, 0)
        o_rows = geom.n_lat_rows
    else:
        assert tq == c
        n_q = 1
        q_map = lambda bi, g, i: (g, ctx_blk0 + bi, 0)
        o_rows = b * c
    o_map = lambda bi, g, i: (g, bi * n_q + i, 0)
    ctx_map = lambda bi, g, i: (g, ctx_blk0 + bi, 0)
    lat_map = lambda bi, g, i: (g, bi, 0)
    in_specs = list(lead_specs) + [pl.BlockSpec((q_per, tq, LANES), q_map),
                                   pl.BlockSpec((k_per, c, LANES), ctx_map),
                                   pl.BlockSpec((v_per, c, LANES), ctx_map)]
    args = list(lead) + [q, k, v]
    if has_lat:
        in_specs += [pl.BlockSpec((k_per, s, LANES), lat_map), pl.BlockSpec((v_per, s, LANES), lat_map)]
        args += [k, v]
    return pl.pallas_call(
        kern,
        grid=(b, n_groups, n_q),
        in_specs=in_specs,
        out_specs=pl.BlockSpec((o_per, tq, LANES), o_map),
        out_shape=jax.ShapeDtypeStruct((o_per * n_groups, o_rows, LANES), BF16),
        compiler_params=_cparams(3),
        name=name,
    )(*args)


def _rope_tables(seq, rot_dim, aidx, active, scale):
    pos = np.arange(seq)
    row = jnp.asarray(pos // GRID_W, F32)
    col = jnp.asarray(pos % GRID_W, F32)
    n_axis = rot_dim // 4
    freqs = ROPE_THETA ** (-jnp.arange(n_axis, dtype=F32) / n_axis)
    ang = jnp.concatenate([row[:, None] * freqs, col[:, None] * freqs], axis=-1)
    a = ang[:, np.asarray(aidx)]
    act = jnp.asarray(active)[None, :]
    sign = jnp.asarray(np.where(np.arange(LANES) < HALF, -1.0, 1.0), F32)[None, :]
    c = jnp.where(act, jnp.cos(a), 1.0)
    sn = jnp.where(act, jnp.sin(a) * sign, 0.0)
    c = jnp.concatenate([c, jnp.ones((ROW_TILE, LANES), F32)], axis=0)
    sn = jnp.concatenate([sn, jnp.zeros((ROW_TILE, LANES), F32)], axis=0)
    return (c * scale, sn * scale), (c, sn)


def _take_cols(w, idx):
    idx = np.asarray(idx)
    out = jnp.take(w, jnp.asarray(np.maximum(idx, 0)), axis=1)
    if (idx < 0).any():
        out = jnp.where(jnp.asarray(idx >= 0)[None, :], out, 0.0)
    return out


def _eo(d):
    return np.concatenate([np.arange(0, d, 2), np.arange(1, d, 2)])


def _pair_layout():
    ev, od = np.arange(0, 64, 2), np.arange(1, 64, 2)
    which = np.concatenate([np.zeros(32), np.ones(32), np.zeros(32), np.ones(32)]).astype(np.int64)
    dim = np.concatenate([ev, ev, od, od])
    return which, dim


def kernel(x, c, ctx, c_ctx, ada_w, ada_b, norm_g, ffn_w_in, ffn_w_out, da_w_in, da_lambda, da_subln, da_w_out,
           ga_w_in, ga_q_norm, ga_k_norm, ga_w_out, mla_w_in, mla_q_norm, mla_kv_norm, mla_w_uq, mla_w_ukv,
           mla_w_out, swa_w_in, swa_sink, swa_w_out):
    batch, seq, d = x.shape
    n_ctx = ctx.shape[1]
    depth = ada_w.shape[0]
    assert d == D_MODEL and n_ctx == ROW_TILE and seq % 512 == 0 and batch < MOD_ROWS and depth == 4
    geom = _Geom(batch, seq, n_ctx)

    cond = jnp.zeros((MOD_ROWS, d), F32).at[:batch].set(c).at[batch].set(c_ctx)
    mods = _ada_call(cond, ada_w, ada_b).reshape(depth, MOD_ROWS, 1, N_MOD * d)

    h = jnp.concatenate([x.reshape(batch * seq, d), ctx.reshape(batch * n_ctx, d)], axis=0)
    g8 = jnp.concatenate([norm_g, jnp.ones((depth, 2, d), F32)], axis=1)
    w_in_b = ffn_w_in.astype(BF16)
    w_out_b = ffn_w_out.astype(BF16)

    which, pdim = _pair_layout()
    tk = 512

    for i in range(depth):
        mod = mods[i]
        last = i == depth - 1
        h = _ffn_call(geom, h, mod, g8[i], w_in_b[i, 0], w_out_b[i, 0], k=0, gi=0, name=f"ffn_a{i}")

        if i == 0:
            w = da_w_in[0]
            hq = DA_HEADS * 2 * DA_QK
            qcols = np.concatenate([hd * 128 + which * 64 + pdim for hd in range(DA_HEADS)])
            wcat = jnp.concatenate([_take_cols(w, qcols), _take_cols(w, hq + qcols), w[:, 2 * hq:]], axis=1)
            tq_tab, tk_tab = _rope_tables(seq, DA_QK, np.arange(LANES) % 32, np.ones(LANES, bool), DA_QK ** -0.5)
            q, k, v = _proj_call(geom, functools.partial(_proj0_kernel, nq=8, nk=8, nv=8), h, mod, g8[i],
                                 [wcat.astype(BF16)], [*tq_tab, *tk_tab], 8, 8, 8, name="proj_diff")
            lam_init = 0.8 - 0.6 * math.exp(-0.3 * i)
            lead = [da_lambda[0], da_subln[0].reshape(1, LANES)]
            lead_specs = [pl.BlockSpec((4, DA_QK), lambda bi, g, qi: (0, 0)),
                          pl.BlockSpec((1, LANES), lambda bi, g, qi: (0, 0))]
            hs = 4
            mk = lambda has_lat: functools.partial(_attn_diff_kernel, has_lat=has_lat, n_heads=hs, n_ctx_keys=n_ctx,
                                                   n_lat_keys=seq, tk=tk, lam_init=lam_init)
            o_lat = _attn_call(geom, mk(True), lead, lead_specs, q, k, v, has_lat=True, tq=256, q_per=hs, k_per=hs,
                               v_per=hs, o_per=hs, n_groups=DA_HEADS // hs, name="attn_diff")
            o_ctx = _attn_call(geom, mk(False), lead, lead_specs, q, k, v, has_lat=False, tq=n_ctx, q_per=hs,
                               k_per=hs, v_per=hs, o_per=hs, n_groups=DA_HEADS // hs, name="attn_diff_ctx")
            w_mo = da_w_out[0]
        elif i == 1:
            w = ga_w_in[0]
            eo = _eo(GA_DIM)
            qcols = np.concatenate([hd * GA_DIM + eo for hd in range(GA_HEADS)])
            kcols = GA_HEADS * GA_DIM + np.concatenate([hd * GA_DIM + eo for hd in range(GA_KV)])
            wcat = jnp.concatenate([_take_cols(w, qcols), _take_cols(w, kcols), w[:, (GA_HEADS + GA_KV) * GA_DIM:]],
                                   axis=1)
            tq_tab, tk_tab = _rope_tables(seq, GA_DIM, np.arange(LANES) % 64, np.ones(LANES, bool), GA_DIM ** -0.5)
            consts = [wcat.astype(BF16), ga_q_norm[0][eo].reshape(1, LANES), ga_k_norm[0][eo].reshape(1, LANES)]
            q, k, v = _proj_call(geom, functools.partial(_proj1_kernel, nq=8, nk=2, nv=2), h, mod, g8[i],
                                 consts, [*tq_tab, *tk_tab], 8, 2, 2, name="proj_gqa")
            grp = GA_HEADS // GA_KV
            mk = lambda has_lat: functools.partial(_attn_gqa_kernel, has_lat=has_lat, grp=grp, n_ctx_keys=n_ctx,
                                                   n_lat_keys=seq, tk=tk)
            o_lat = _attn_call(geom, mk(True), [], [], q, k, v, has_lat=True, tq=256, q_per=grp, k_per=1, v_per=1,
                               o_per=grp, n_groups=GA_KV, name="attn_gqa")
            o_ctx = _attn_call(geom, mk(False), [], [], q, k, v, has_lat=False, tq=n_ctx, q_per=grp, k_per=1,
                               v_per=1, o_per=grp, n_groups=GA_KV, name="attn_gqa_ctx")
            w_mo = ga_w_out[0]
        elif i == 2:
            lanes = np.arange(LANES)
            is_rope = ((lanes % HALF) >= 32) & ((lanes % HALF) < 48)
            is_nope = (lanes % HALF) < 32
            rope_dim = 2 * ((lanes % HALF) - 32) + (lanes >= HALF)
            nope_dim = (lanes % HALF) + 32 * (lanes >= HALF)
            w = mla_w_in[0]
            kr_cols = np.where(is_rope, MLA_Q_RANK + MLA_KV_RANK + rope_dim, -1)
            w1 = jnp.concatenate([w[:, :MLA_Q_RANK + MLA_KV_RANK], _take_cols(w, kr_cols)], axis=1)
            qd = MLA_NOPE + MLA_ROPE
            q_blk = np.where(is_nope, nope_dim, np.where(is_rope, MLA_NOPE + rope_dim, -1))
            q_cols = np.concatenate([np.where(q_blk >= 0, hd * qd + q_blk, -1) for hd in range(MLA_HEADS)])
            kvd = MLA_NOPE + MLA_V
            k_blk = np.where(is_nope, nope_dim, -1)
            k_cols = np.concatenate([np.where(k_blk >= 0, hd * kvd + k_blk, -1) for hd in range(MLA_HEADS)])
            v_cols = np.concatenate([hd * kvd + MLA_NOPE + np.arange(MLA_V) for hd in range(MLA_HEADS)])
            consts = [w1.astype(BF16), mla_q_norm[0].reshape(1, -1), mla_kv_norm[0].reshape(1, -1),
                      _take_cols(mla_w_uq[0], q_cols).astype(BF16), _take_cols(mla_w_ukv[0], k_cols).astype(BF16),
                      _take_cols(mla_w_ukv[0], v_cols).astype(BF16)]
            aidx = np.where(is_rope, (lanes % HALF) - 32, 0)
            tq_tab, tk_tab = _rope_tables(seq, MLA_ROPE, aidx, is_rope, (MLA_NOPE + MLA_ROPE) ** -0.5)
            q, k, v = _proj_call(geom, _proj2_kernel, h, mod, g8[i], consts, [*tq_tab, *tk_tab],
                                 MLA_HEADS, MLA_HEADS, MLA_HEADS // 2, name="proj_mla")
            n_pairs = 4
            mk = lambda has_lat: functools.partial(_attn_mla_kernel, has_lat=has_lat, n_pairs=n_pairs,
                                                   n_ctx_keys=n_ctx, n_lat_keys=seq, tk=tk)
            o_lat = _attn_call(geom, mk(True), [], [], q, k, v, has_lat=True, tq=512, q_per=2 * n_pairs,
                               k_per=2 * n_pairs, v_per=n_pairs, o_per=n_pairs,
                               n_groups=MLA_HEADS // (2 * n_pairs), name="attn_mla")
            o_ctx = _attn_call(geom, mk(False), [], [], q, k, v, has_lat=False, tq=n_ctx, q_per=2 * n_pairs,
                               k_per=2 * n_pairs, v_per=n_pairs, o_per=n_pairs,
                               n_groups=MLA_HEADS // (2 * n_pairs), name="attn_mla_ctx")
            w_mo = mla_w_out[0]
        else:
            w = swa_w_in[0]
            hq = SWA_HEADS * SWA_DIM
            qcols = np.concatenate([(2 * p + which) * SWA_DIM + pdim for p in range(SWA_HEADS // 2)])
            kcols = np.concatenate([hq + g * SWA_DIM + pdim for g in range(SWA_KV)])
            vcols = np.concatenate([hq + SWA_KV * SWA_DIM + g * SWA_DIM + (np.arange(LANES) % SWA_DIM)
                                    for g in range(SWA_KV)])
            wcat = jnp.concatenate([_take_cols(w, qcols), _take_cols(w, kcols), _take_cols(w, vcols)], axis=1)
            tq_tab, tk_tab = _rope_tables(seq, SWA_DIM, np.arange(LANES) % 32, np.ones(LANES, bool), SWA_DIM ** -0.5)
            q, k, v = _proj_call(geom, functools.partial(_proj0_kernel, nq=8, nk=2, nv=2), h, mod, g8[i],
                                 [wcat.astype(BF16)], [*tq_tab, *tk_tab], 8, 2, 2, name="proj_swa")
            hpg = SWA_HEADS // SWA_KV
            sink = jnp.broadcast_to(swa_sink[0].astype(F32).reshape(SWA_KV, hpg, 1), (SWA_KV, hpg, LANES))
            tq = 256
            kern = functools.partial(_attn_swa_kernel, n_pairs=hpg // 2, n_ctx_keys=n_ctx, seq=seq,
                                     win=tq + 2 * WINDOW)
            o_lat = _attn_call(geom, kern, [sink], [pl.BlockSpec((None, hpg, LANES), lambda bi, g, qi: (g, 0, 0))],
                               q, k, v, has_lat=True, tq=tq, q_per=hpg // 2, k_per=1, v_per=1, o_per=hpg // 2,
                               n_groups=SWA_KV, name="attn_swa")
            o_ctx = None
            w_mo = swa_w_out[0]

        h = _ffn_call(geom, h, mod, g8[i], w_in_b[i, 1], w_out_b[i, 1], k=6, gi=4,
                      pre=(o_lat, o_ctx, w_mo.astype(BF16)), n_tiles=geom.n_lat if last else None,
                      name=f"ffn_b{i}")
    return h.reshape(batch, seq, d)
```

```python
import functools
import math

import numpy as np
import jax
import jax.numpy as jnp
from jax import lax
from jax.experimental import pallas as pl
from jax.experimental.pallas import tpu as pltpu

F32 = jnp.float32
BF16 = jnp.bfloat16

D_MODEL = 1024
D_FF = 2816
N_MOD = 9
NORM_EPS = 1e-6
GRID_W = 64
ROPE_THETA = 10000.0
NEG_INF = -1e30
WINDOW = 128
LOG2E = math.log2(math.e)

LANES = 128
HALF = LANES // 2
ROW_TILE = 512
FF_CHUNK = 256
MOD_ROWS = 16
VMEM_LIMIT = 52 * 1024 * 1024

DA_HEADS, DA_QK = 8, 64
GA_HEADS, GA_KV, GA_DIM = 8, 2, 128
MLA_HEADS, MLA_Q_RANK, MLA_KV_RANK, MLA_NOPE, MLA_ROPE, MLA_V = 16, 256, 128, 64, 32, 64
SWA_HEADS, SWA_KV, SWA_DIM = 16, 2, 64


def _cparams(n_axes):
    return pltpu.CompilerParams(dimension_semantics=("arbitrary",) * n_axes,
                                vmem_limit_bytes=VMEM_LIMIT)


def _rms(x, g):
    return x * lax.rsqrt(jnp.mean(x * x, axis=-1, keepdims=True) + NORM_EPS) * g


def _mod(mod_ref, k):
    return mod_ref[:, k * D_MODEL:(k + 1) * D_MODEL]


def _modnorm(x, g, mod_ref, k):
    return _rms(x, g) * (1.0 + _mod(mod_ref, k + 1)) + _mod(mod_ref, k)


def _rope(x, c, s):
    return x * c + pltpu.roll(x, HALF, 1) * s


def _ada_kernel(c_ref, w_ref, b_ref, o_ref):
    c = c_ref[...]
    sc = c / (1.0 + jnp.exp(-c))
    o_ref[...] = jnp.dot(sc, w_ref[...], precision=lax.Precision.HIGHEST,
                         preferred_element_type=F32) + b_ref[...]


def _ada_call(cond, ada_w, ada_b):
    depth, d, n = ada_w.shape
    cw = 1024
    return pl.pallas_call(
        _ada_kernel,
        grid=(depth, n // cw),
        in_specs=[pl.BlockSpec((MOD_ROWS, d), lambda l, j: (0, 0)),
                  pl.BlockSpec((None, d, cw), lambda l, j: (l, 0, j)),
                  pl.BlockSpec((None, 1, cw), lambda l, j: (l, 0, j))],
        out_specs=pl.BlockSpec((None, MOD_ROWS, cw), lambda l, j: (l, 0, j)),
        out_shape=jax.ShapeDtypeStruct((depth, MOD_ROWS, n), F32),
        compiler_params=_cparams(2),
        name="adaln",
    )(cond, ada_w, ada_b.reshape(depth, 1, n))


class _Geom:
    def __init__(self, batch, seq, ctx):
        self.batch, self.seq, self.ctx = batch, seq, ctx
        self.n_lat_rows = batch * seq
        self.n_rows = batch * (seq + ctx)
        self.n_lat = self.n_lat_rows // ROW_TILE
        self.n_tiles = self.n_rows // ROW_TILE
        self.tiles_per_batch = seq // ROW_TILE

    def mod_row(self, t):
        return jnp.where(t < self.n_lat, t // self.tiles_per_batch, self.batch)

    def rope_blk(self, t):
        return jnp.where(t < self.n_lat, t % self.tiles_per_batch, self.tiles_per_batch)


def _const_spec(shape):
    nd = len(shape)
    return pl.BlockSpec(shape, lambda *_: (0,) * nd, pipeline_mode=pl.Buffered(1))


def _ffn_kernel(*refs, k, gi, n_lat, has_pre, has_ctx):
    it = iter(refs)
    x_ref, mod_ref, g_ref, win_ref, wout_ref = (next(it) for _ in range(5))
    x = x_ref[...]
    if has_pre:
        ol_ref = next(it)
        oc_ref = next(it) if has_ctx else None
        wmo_ref = next(it)
        nblk = ol_ref.shape[0]
        o = jnp.concatenate([ol_ref[i] for i in range(nblk)], axis=1)
        if has_ctx:
            oc = jnp.concatenate([oc_ref[i] for i in range(nblk)], axis=1)
            o = jnp.where(pl.program_id(0) < n_lat, o, oc)
        y = jnp.dot(o, wmo_ref[...], preferred_element_type=F32)
        x = x + _mod(mod_ref, 5) * _rms(y, g_ref[3:4, :])
    out_ref = next(it)

    u = _modnorm(x, g_ref[gi:gi + 1, :], mod_ref, k).astype(BF16)

    def gate_up(c):
        lo = c * FF_CHUNK
        return (jnp.dot(u, win_ref[:, lo:lo + FF_CHUNK], preferred_element_type=F32),
                jnp.dot(u, win_ref[:, D_FF + lo:D_FF + lo + FF_CHUNK], preferred_element_type=F32))

    n_chunks = D_FF // FF_CHUNK
    acc = None
    nxt = gate_up(0)
    for c in range(n_chunks):
        a, b = nxt
        if c + 1 < n_chunks:
            nxt = gate_up(c + 1)
        act = (a / (1.0 + jnp.exp(-a)) * b).astype(BF16)
        part = jnp.dot(act, wout_ref[c * FF_CHUNK:(c + 1) * FF_CHUNK, :], preferred_element_type=F32)
        acc = part if acc is None else acc + part
    out_ref[...] = x + 0.5 * _mod(mod_ref, k + 2) * _rms(acc, g_ref[gi + 1:gi + 2, :])


def _ffn_call(geom, h, mod, g8, w_in, w_out, *, k, gi, pre=None, n_tiles=None, name):
    n_tiles = geom.n_tiles if n_tiles is None else n_tiles
    d = D_MODEL
    in_specs = [pl.BlockSpec((ROW_TILE, d), lambda t: (t, 0)),
                pl.BlockSpec((None, 1, N_MOD * d), lambda t: (geom.mod_row(t), 0, 0)),
                _const_spec(g8.shape), _const_spec(w_in.shape), _const_spec(w_out.shape)]
    args = [h, mod, g8, w_in, w_out]
    has_pre = pre is not None
    has_ctx = False
    if has_pre:
        o_lat, o_ctx, w_mo = pre
        nblk = o_lat.shape[0]
        in_specs.append(pl.BlockSpec((nblk, ROW_TILE, LANES),
                                     lambda t: (0, jnp.minimum(t, geom.n_lat - 1), 0)))
        args.append(o_lat)
        has_ctx = o_ctx is not None
        if has_ctx:
            in_specs.append(pl.BlockSpec((nblk, ROW_TILE, LANES),
                                         lambda t: (0, jnp.maximum(t - geom.n_lat, 0), 0)))
            args.append(o_ctx)
        in_specs.append(_const_spec(w_mo.shape))
        args.append(w_mo)
    kern = functools.partial(_ffn_kernel, k=k, gi=gi, n_lat=geom.n_lat, has_pre=has_pre, has_ctx=has_ctx)
    return pl.pallas_call(
        kern,
        grid=(n_tiles,),
        in_specs=in_specs,
        out_specs=pl.BlockSpec((ROW_TILE, d), lambda t: (t, 0)),
        out_shape=jax.ShapeDtypeStruct((n_tiles * ROW_TILE, d), F32),
        compiler_params=_cparams(1),
        name=name,
    )(*args)


def _store_blocks(dst_ref, x, n, fn=None):
    for i in range(n):
        blk = x[:, i * LANES:(i + 1) * LANES]
        if fn is not None:
            blk = fn(blk)
        dst_ref[i] = blk.astype(BF16)


def _proj0_kernel(x_ref, mod_ref, g_ref, w_ref, cq_ref, sq_ref, ck_ref, sk_ref, q_ref, k_ref, v_ref, *, nq, nk, nv):
    u = _modnorm(x_ref[...], g_ref[2:3, :], mod_ref, 3).astype(BF16)
    cq, sq, ck, sk = cq_ref[...], sq_ref[...], ck_ref[...], sk_ref[...]
    t = jnp.dot(u, w_ref[...], preferred_element_type=F32)
    _store_blocks(q_ref, t, nq, lambda b: _rope(b, cq, sq))
    _store_blocks(k_ref, t[:, nq * LANES:], nk, lambda b: _rope(b, ck, sk))
    _store_blocks(v_ref, t[:, (nq + nk) * LANES:], nv)


def _proj1_kernel(x_ref, mod_ref, g_ref, w_ref, qn_ref, kn_ref, cq_ref, sq_ref, ck_ref, sk_ref,
                  q_ref, k_ref, v_ref, *, nq, nk, nv):
    u = _modnorm(x_ref[...], g_ref[2:3, :], mod_ref, 3).astype(BF16)
    cq, sq, ck, sk = cq_ref[...], sq_ref[...], ck_ref[...], sk_ref[...]
    qn, kn = qn_ref[...], kn_ref[...]
    t = jnp.dot(u, w_ref[...], preferred_element_type=F32)
    _store_blocks(q_ref, t, nq, lambda b: _rope(_rms(b, qn), cq, sq))
    _store_blocks(k_ref, t[:, nq * LANES:], nk, lambda b: _rope(_rms(b, kn), ck, sk))
    _store_blocks(v_ref, t[:, (nq + nk) * LANES:], nv)


def _proj2_kernel(x_ref, mod_ref, g_ref, w1_ref, qn_ref, kvn_ref, wuq_ref, wuk_ref, wuv_ref,
                  cq_ref, sq_ref, ck_ref, sk_ref, q_ref, k_ref, v_ref):
    u = _modnorm(x_ref[...], g_ref[2:3, :], mod_ref, 3).astype(BF16)
    cq, sq, ck, sk = cq_ref[...], sq_ref[...], ck_ref[...], sk_ref[...]
    t = jnp.dot(u, w1_ref[...], preferred_element_type=F32)
    cqv = _rms(t[:, :MLA_Q_RANK], qn_ref[...]).astype(BF16)
    ckv = _rms(t[:, MLA_Q_RANK:MLA_Q_RANK + MLA_KV_RANK], kvn_ref[...]).astype(BF16)
    kr = _rope(t[:, MLA_Q_RANK + MLA_KV_RANK:], ck, sk)
    qa = jnp.dot(cqv, wuq_ref[...], preferred_element_type=F32)
    _store_blocks(q_ref, qa, MLA_HEADS, lambda b: _rope(b, cq, sq))
    ka = jnp.dot(ckv, wuk_ref[...], preferred_element_type=F32)
    _store_blocks(k_ref, ka, MLA_HEADS, lambda b: b + kr)
    va = jnp.dot(ckv, wuv_ref[...], preferred_element_type=F32)
    _store_blocks(v_ref, va, MLA_HEADS // 2)


def _proj_call(geom, kern, h, mod, g8, consts, tables, nq, nk, nv, name):
    d = D_MODEL
    tile = lambda t: (t, 0)
    rope_spec = pl.BlockSpec((ROW_TILE, LANES), lambda t: (geom.rope_blk(t), 0))
    in_specs = ([pl.BlockSpec((ROW_TILE, d), tile),
                 pl.BlockSpec((None, 1, N_MOD * d), lambda t: (geom.mod_row(t), 0, 0)),
                 _const_spec(g8.shape)]
                + [_const_spec(c.shape) for c in consts]
                + [rope_spec] * 4)
    blk = lambda n: pl.BlockSpec((n, ROW_TILE, LANES), lambda t: (0, t, 0))
    shp = lambda n: jax.ShapeDtypeStruct((n, geom.n_rows, LANES), BF16)
    return pl.pallas_call(
        kern,
        grid=(geom.n_tiles,),
        in_specs=in_specs,
        out_specs=[blk(nq), blk(nk), blk(nv)],
        out_shape=[shp(nq), shp(nk), shp(nv)],
        compiler_params=_cparams(1),
        name=name,
    )(h, mod, g8, *consts, *tables)


def _qk(qs, k):
    return lax.dot_general(qs, k, (((1,), (1,)), ((), ())), preferred_element_type=F32)


def _flash(qs, chunks):
    def scores(chunk):
        k_ref, hk, _, _, start, size = chunk
        return _qk(qs, k_ref[hk, pl.ds(start, size), :])

    m = l = acc = None
    s_next = scores(chunks[0])
    for idx, (_, _, v_ref, hv, start, size) in enumerate(chunks):
        s = s_next
        if idx + 1 < len(chunks):
            s_next = scores(chunks[idx + 1])
        mc = jnp.max(s, axis=1, keepdims=True)
        m_new = mc if m is None else jnp.maximum(m, mc)
        p = jnp.exp2(s - m_new)
        ps = jnp.sum(p, axis=1, keepdims=True)
        pv = jnp.dot(p.astype(BF16), v_ref[hv, pl.ds(start, size), :], preferred_element_type=F32)
        if m is None:
            l, acc = ps, pv
        else:
            alpha = jnp.exp2(m - m_new)
            l = alpha * l + ps
            acc = alpha * acc + pv
        m = m_new
    return acc, l


def _kv_chunks(kc_ref, vc_ref, kl_ref, vl_ref, hk, hv, n_ctx_keys, n_lat_keys, tk):
    chunks = [(kc_ref, hk, vc_ref, hv, 0, n_ctx_keys)]
    if kl_ref is not None:
        chunks += [(kl_ref, hk, vl_ref, hv, j * tk, tk) for j in range(n_lat_keys // tk)]
    return chunks


def _split_refs(refs, n_lead, has_lat):
    lead = refs[:n_lead]
    rest = refs[n_lead:]
    q_ref, kc_ref, vc_ref = rest[:3]
    if has_lat:
        kl_ref, vl_ref, o_ref = rest[3:6]
    else:
        kl_ref = vl_ref = None
        o_ref = rest[3]
    return lead, q_ref, kc_ref, vc_ref, kl_ref, vl_ref, o_ref


def _lane_lo(rows):
    lane = lax.broadcasted_iota(jnp.int32, (rows, LANES), 1)
    return (lane % HALF) < (HALF // 2)


def _attn_diff_kernel(*refs, has_lat, n_heads, n_ctx_keys, n_lat_keys, tk, lam_init):
    (lam_ref, subln_ref), q_ref, kc_ref, vc_ref, kl_ref, vl_ref, o_ref = _split_refs(refs, 2, has_lat)
    tq = q_ref.shape[1]
    lo = _lane_lo(tq)
    lf = lam_ref[...]
    lam = (jnp.exp(jnp.sum(lf[0:1] * lf[1:2], keepdims=True))
           - jnp.exp(jnp.sum(lf[2:3] * lf[3:4], keepdims=True)) + lam_init)
    subln = subln_ref[...]

    def head(h, carry):
        q = q_ref[h]
        zero = jnp.zeros_like(q)
        qs = jnp.concatenate([jnp.where(lo, q, zero), jnp.where(lo, zero, q)], axis=0)
        acc, l = _flash(qs, _kv_chunks(kc_ref, vc_ref, kl_ref, vl_ref, h, h, n_ctx_keys, n_lat_keys, tk))
        o = acc / l
        o = o[:tq] - lam * o[tq:]
        o_ref[h] = (_rms(o, subln) * (1.0 - lam_init)).astype(BF16)
        return carry

    lax.fori_loop(0, n_heads, head, 0)


def _attn_gqa_kernel(*refs, has_lat, grp, n_ctx_keys, n_lat_keys, tk):
    _, q_ref, kc_ref, vc_ref, kl_ref, vl_ref, o_ref = _split_refs(refs, 0, has_lat)
    tq = q_ref.shape[1]
    qs = jnp.concatenate([q_ref[j] for j in range(grp)], axis=0)
    acc, l = _flash(qs, _kv_chunks(kc_ref, vc_ref, kl_ref, vl_ref, 0, 0, n_ctx_keys, n_lat_keys, tk))
    o = acc / l
    for j in range(grp):
        o_ref[j] = o[j * tq:(j + 1) * tq].astype(BF16)


def _attn_mla_kernel(*refs, has_lat, n_pairs, n_ctx_keys, n_lat_keys, tk):
    _, q_ref, kc_ref, vc_ref, kl_ref, vl_ref, o_ref = _split_refs(refs, 0, has_lat)
    tq = q_ref.shape[1]
    lane = lax.broadcasted_iota(jnp.int32, (tq, LANES), 1)

    def pair(p, carry):
        outs = []
        for e in range(2):
            h = 2 * p + e
            acc, l = _flash(q_ref[h], _kv_chunks(kc_ref, vc_ref, kl_ref, vl_ref, h, p, n_ctx_keys, n_lat_keys, tk))
            outs.append(acc / l)
        o_ref[p] = jnp.where(lane < HALF, outs[0], outs[1]).astype(BF16)
        return carry

    lax.fori_loop(0, n_pairs, pair, 0)


def _attn_swa_kernel(sink_ref, q_ref, kc_ref, vc_ref, kl_ref, vl_ref, o_ref, *, n_pairs, seq, win):
    tq = q_ref.shape[1]
    i = pl.program_id(2)
    start = pl.multiple_of(jnp.clip(i * tq - WINDOW, 0, seq - win), WINDOW)
    lo = _lane_lo(tq)
    lane = lax.broadcasted_iota(jnp.int32, (tq, LANES), 1)
    row = lax.broadcasted_iota(jnp.int32, (2 * tq, win), 0)
    col = lax.broadcasted_iota(jnp.int32, (2 * tq, win), 1)
    qpos = i * tq + jnp.where(row >= tq, row - tq, row)
    allowed = jnp.abs(qpos - (start + col)) <= WINDOW

    def scores(j):
        q = q_ref[j]
        zero = jnp.zeros_like(q)
        qs = jnp.concatenate([jnp.where(lo, q, zero), jnp.where(lo, zero, q)], axis=0)
        s_band = _qk(qs, kl_ref[0, pl.ds(start, win), :])
        return _qk(qs, kc_ref[0]), jnp.where(allowed, s_band, NEG_INF)

    nxt = scores(0)
    for j in range(n_pairs):
        s_ctx, s_band = nxt
        if j + 1 < n_pairs:
            nxt = scores(j + 1)
        sink = jnp.concatenate([jnp.broadcast_to(sink_ref[2 * j:2 * j + 1, 0:1], (tq, 1)),
                                jnp.broadcast_to(sink_ref[2 * j + 1:2 * j + 2, 0:1], (tq, 1))], axis=0)
        m = jnp.maximum(jnp.maximum(jnp.max(s_ctx, axis=1, keepdims=True),
                                    jnp.max(s_band, axis=1, keepdims=True)), sink)
        p_ctx = jnp.exp2(s_ctx - m)
        p_band = jnp.exp2(s_band - m)
        l = (jnp.sum(p_ctx, axis=1, keepdims=True) + jnp.sum(p_band, axis=1, keepdims=True)
             + jnp.exp2(sink - m))
        acc = (jnp.dot(p_ctx.astype(BF16), vc_ref[0], preferred_element_type=F32)
               + jnp.dot(p_band.astype(BF16), vl_ref[0, pl.ds(start, win), :], preferred_element_type=F32))
        o = acc / l
        o_ref[j] = jnp.where(lane < HALF, o[:tq], o[tq:]).astype(BF16)


def _attn_call(geom, kern, lead, lead_specs, q, k, v, *, has_lat, tq, q_per, k_per, v_per, o_per, n_groups, name):
    b, s, c = geom.batch, geom.seq, geom.ctx
    ctx_blk0 = geom.n_lat_rows // c
    if has_lat:
        n_q = s // tq
        q_map = lambda bi, g, i: (g, bi * n_q + i, 0)
        o_rows = geom.n_lat_rows
    else:
        assert tq == c
        n_q = 1
        q_map = lambda bi, g, i: (g, ctx_blk0 + bi, 0)
        o_rows = b * c
    o_map = lambda bi, g, i: (g, bi * n_q + i, 0)
    ctx_map = lambda bi, g, i: (g, ctx_blk0 + bi, 0)
    lat_map = lambda bi, g, i: (g, bi, 0)
    in_specs = list(lead_specs) + [pl.BlockSpec((q_per, tq, LANES), q_map),
                                   pl.BlockSpec((k_per, c, LANES), ctx_map),
                                   pl.BlockSpec((v_per, c, LANES), ctx_map)]
    args = list(lead) + [q, k, v]
    if has_lat:
        in_specs += [pl.BlockSpec((k_per, s, LANES), lat_map), pl.BlockSpec((v_per, s, LANES), lat_map)]
        args += [k, v]
    return pl.pallas_call(
        kern,
        grid=(b, n_groups, n_q),
        in_specs=in_specs,
        out_specs=pl.BlockSpec((o_per, tq, LANES), o_map),
        out_shape=jax.ShapeDtypeStruct((o_per * n_groups, o_rows, LANES), BF16),
        compiler_params=_cparams(3),
        name=name,
    )(*args)


def _rope_tables(seq, rot_dim, aidx, active, scale):
    scale = scale * LOG2E
    pos = np.arange(seq)
    row = jnp.asarray(pos // GRID_W, F32)
    col = jnp.asarray(pos % GRID_W, F32)
    n_axis = rot_dim // 4
    freqs = ROPE_THETA ** (-jnp.arange(n_axis, dtype=F32) / n_axis)
    ang = jnp.concatenate([row[:, None] * freqs, col[:, None] * freqs], axis=-1)
    a = ang[:, np.asarray(aidx)]
    act = jnp.asarray(active)[None, :]
    sign = jnp.asarray(np.where(np.arange(LANES) < HALF, -1.0, 1.0), F32)[None, :]
    c = jnp.where(act, jnp.cos(a), 1.0)
    sn = jnp.where(act, jnp.sin(a) * sign, 0.0)
    c = jnp.concatenate([c, jnp.ones((ROW_TILE, LANES), F32)], axis=0)
    sn = jnp.concatenate([sn, jnp.zeros((ROW_TILE, LANES), F32)], axis=0)
    return (c * scale, sn * scale), (c, sn)


def _take_cols(w, idx):
    idx = np.asarray(idx)
    out = jnp.take(w, jnp.asarray(np.maximum(idx, 0)), axis=1)
    if (idx < 0).any():
        out = jnp.where(jnp.asarray(idx >= 0)[None, :], out, 0.0)
    return out


def _eo(d):
    return np.concatenate([np.arange(0, d, 2), np.arange(1, d, 2)])


def _pair_layout():
    ev, od = np.arange(0, 64, 2), np.arange(1, 64, 2)
    which = np.concatenate([np.zeros(32), np.ones(32), np.zeros(32), np.ones(32)]).astype(np.int64)
    dim = np.concatenate([ev, ev, od, od])
    return which, dim


def kernel(x, c, ctx, c_ctx, ada_w, ada_b, norm_g, ffn_w_in, ffn_w_out, da_w_in, da_lambda, da_subln, da_w_out,
           ga_w_in, ga_q_norm, ga_k_norm, ga_w_out, mla_w_in, mla_q_norm, mla_kv_norm, mla_w_uq, mla_w_ukv,
           mla_w_out, swa_w_in, swa_sink, swa_w_out):
    batch, seq, d = x.shape
    n_ctx = ctx.shape[1]
    depth = ada_w.shape[0]
    assert d == D_MODEL and seq % ROW_TILE == 0 and (batch * n_ctx) % ROW_TILE == 0
    assert n_ctx % 256 == 0 and seq % 512 == 0 and batch < MOD_ROWS and depth == 4
    geom = _Geom(batch, seq, n_ctx)

    cond = jnp.zeros((MOD_ROWS, d), F32).at[:batch].set(c).at[batch].set(c_ctx)
    mods = _ada_call(cond, ada_w, ada_b).reshape(depth, MOD_ROWS, 1, N_MOD * d)

    h = jnp.concatenate([x.reshape(batch * seq, d), ctx.reshape(batch * n_ctx, d)], axis=0)
    g8 = jnp.concatenate([norm_g, jnp.ones((depth, 2, d), F32)], axis=1)
    w_in_b = ffn_w_in.astype(BF16)
    w_out_b = ffn_w_out.astype(BF16)

    which, pdim = _pair_layout()
    tk = 512

    for i in range(depth):
        mod = mods[i]
        last = i == depth - 1
        h = _ffn_call(geom, h, mod, g8[i], w_in_b[i, 0], w_out_b[i, 0], k=0, gi=0, name=f"ffn_a{i}")

        if i == 0:
            w = da_w_in[0]
            hq = DA_HEADS * 2 * DA_QK
            qcols = np.concatenate([hd * 128 + which * 64 + pdim for hd in range(DA_HEADS)])
            wcat = jnp.concatenate([_take_cols(w, qcols), _take_cols(w, hq + qcols), w[:, 2 * hq:]], axis=1)
            tq_tab, tk_tab = _rope_tables(seq, DA_QK, np.arange(LANES) % 32, np.ones(LANES, bool), DA_QK ** -0.5)
            q, k, v = _proj_call(geom, functools.partial(_proj0_kernel, nq=8, nk=8, nv=8), h, mod, g8[i],
                                 [wcat.astype(BF16)], [*tq_tab, *tk_tab], 8, 8, 8, name="proj_diff")
            lam_init = 0.8 - 0.6 * math.exp(-0.3 * i)
            lead = [da_lambda[0], da_subln[0].reshape(1, LANES)]
            lead_specs = [pl.BlockSpec((4, DA_QK), lambda bi, g, qi: (0, 0)),
                          pl.BlockSpec((1, LANES), lambda bi, g, qi: (0, 0))]
            hs = 4
            mk = lambda has_lat: functools.partial(_attn_diff_kernel, has_lat=has_lat, n_heads=hs, n_ctx_keys=n_ctx,
                                                   n_lat_keys=seq, tk=tk, lam_init=lam_init)
            o_lat = _attn_call(geom, mk(True), lead, lead_specs, q, k, v, has_lat=True, tq=256, q_per=hs, k_per=hs,
                               v_per=hs, o_per=hs, n_groups=DA_HEADS // hs, name="attn_diff")
            o_ctx = _attn_call(geom, mk(False), lead, lead_specs, q, k, v, has_lat=False, tq=n_ctx, q_per=hs,
                               k_per=hs, v_per=hs, o_per=hs, n_groups=DA_HEADS // hs, name="attn_diff_ctx")
            w_mo = da_w_out[0]
        elif i == 1:
            w = ga_w_in[0]
            eo = _eo(GA_DIM)
            qcols = np.concatenate([hd * GA_DIM + eo for hd in range(GA_HEADS)])
            kcols = GA_HEADS * GA_DIM + np.concatenate([hd * GA_DIM + eo for hd in range(GA_KV)])
            wcat = jnp.concatenate([_take_cols(w, qcols), _take_cols(w, kcols), w[:, (GA_HEADS + GA_KV) * GA_DIM:]],
                                   axis=1)
            tq_tab, tk_tab = _rope_tables(seq, GA_DIM, np.arange(LANES) % 64, np.ones(LANES, bool), GA_DIM ** -0.5)
            consts = [wcat.astype(BF16), ga_q_norm[0][eo].reshape(1, LANES), ga_k_norm[0][eo].reshape(1, LANES)]
            q, k, v = _proj_call(geom, functools.partial(_proj1_kernel, nq=8, nk=2, nv=2), h, mod, g8[i],
                                 consts, [*tq_tab, *tk_tab], 8, 2, 2, name="proj_gqa")
            grp = GA_HEADS // GA_KV
            mk = lambda has_lat: functools.partial(_attn_gqa_kernel, has_lat=has_lat, grp=grp, n_ctx_keys=n_ctx,
                                                   n_lat_keys=seq, tk=tk)
            o_lat = _attn_call(geom, mk(True), [], [], q, k, v, has_lat=True, tq=256, q_per=grp, k_per=1, v_per=1,
                               o_per=grp, n_groups=GA_KV, name="attn_gqa")
            o_ctx = _attn_call(geom, mk(False), [], [], q, k, v, has_lat=False, tq=n_ctx, q_per=grp, k_per=1,
                               v_per=1, o_per=grp, n_groups=GA_KV, name="attn_gqa_ctx")
            w_mo = ga_w_out[0]
        elif i == 2:
            lanes = np.arange(LANES)
            is_rope = ((lanes % HALF) >= 32) & ((lanes % HALF) < 48)
            is_nope = (lanes % HALF) < 32
            rope_dim = 2 * ((lanes % HALF) - 32) + (lanes >= HALF)
            nope_dim = (lanes % HALF) + 32 * (lanes >= HALF)
            w = mla_w_in[0]
            kr_cols = np.where(is_rope, MLA_Q_RANK + MLA_KV_RANK + rope_dim, -1)
            w1 = jnp.concatenate([w[:, :MLA_Q_RANK + MLA_KV_RANK], _take_cols(w, kr_cols)], axis=1)
            qd = MLA_NOPE + MLA_ROPE
            q_blk = np.where(is_nope, nope_dim, np.where(is_rope, MLA_NOPE + rope_dim, -1))
            q_cols = np.concatenate([np.where(q_blk >= 0, hd * qd + q_blk, -1) for hd in range(MLA_HEADS)])
            kvd = MLA_NOPE + MLA_V
            k_blk = np.where(is_nope, nope_dim, -1)
            k_cols = np.concatenate([np.where(k_blk >= 0, hd * kvd + k_blk, -1) for hd in range(MLA_HEADS)])
            v_cols = np.concatenate([hd * kvd + MLA_NOPE + np.arange(MLA_V) for hd in range(MLA_HEADS)])
            consts = [w1.astype(BF16), mla_q_norm[0].reshape(1, -1), mla_kv_norm[0].reshape(1, -1),
                      _take_cols(mla_w_uq[0], q_cols).astype(BF16), _take_cols(mla_w_ukv[0], k_cols).astype(BF16),
                      _take_cols(mla_w_ukv[0], v_cols).astype(BF16)]
            aidx = np.where(is_rope, (lanes % HALF) - 32, 0)
            tq_tab, tk_tab = _rope_tables(seq, MLA_ROPE, aidx, is_rope, (MLA_NOPE + MLA_ROPE) ** -0.5)
            q, k, v = _proj_call(geom, _proj2_kernel, h, mod, g8[i], consts, [*tq_tab, *tk_tab],
                                 MLA_HEADS, MLA_HEADS, MLA_HEADS // 2, name="proj_mla")
            n_pairs = 4
            mk = lambda has_lat: functools.partial(_attn_mla_kernel, has_lat=has_lat, n_pairs=n_pairs,
                                                   n_ctx_keys=n_ctx, n_lat_keys=seq, tk=tk)
            o_lat = _attn_call(geom, mk(True), [], [], q, k, v, has_lat=True, tq=512, q_per=2 * n_pairs,
                               k_per=2 * n_pairs, v_per=n_pairs, o_per=n_pairs,
                               n_groups=MLA_HEADS // (2 * n_pairs), name="attn_mla")
            o_ctx = _attn_call(geom, mk(False), [], [], q, k, v, has_lat=False, tq=n_ctx, q_per=2 * n_pairs,
                               k_per=2 * n_pairs, v_per=n_pairs, o_per=n_pairs,
                               n_groups=MLA_HEADS // (2 * n_pairs), name="attn_mla_ctx")
            w_mo = mla_w_out[0]
        else:
            w = swa_w_in[0]
            hq = SWA_HEADS * SWA_DIM
            qcols = np.concatenate([(2 * p + which) * SWA_DIM + pdim for p in range(SWA_HEADS // 2)])
            kcols = np.concatenate([hq + g * SWA_DIM + pdim for g in range(SWA_KV)])
            vcols = np.concatenate([hq + SWA_KV * SWA_DIM + g * SWA_DIM + (np.arange(LANES) % SWA_DIM)
                                    for g in range(SWA_KV)])
            wcat = jnp.concatenate([_take_cols(w, qcols), _take_cols(w, kcols), _take_cols(w, vcols)], axis=1)
            tq_tab, tk_tab = _rope_tables(seq, SWA_DIM, np.arange(LANES) % 32, np.ones(LANES, bool), SWA_DIM ** -0.5)
            q, k, v = _proj_call(geom, functools.partial(_proj0_kernel, nq=8, nk=2, nv=2), h, mod, g8[i],
                                 [wcat.astype(BF16)], [*tq_tab, *tk_tab], 8, 2, 2, name="proj_swa")
            hpg = SWA_HEADS // SWA_KV
            sink = jnp.broadcast_to((swa_sink[0].astype(F32) * LOG2E).reshape(SWA_KV, hpg, 1), (SWA_KV, hpg, LANES))
            tq = 256
            kern = functools.partial(_attn_swa_kernel, n_pairs=hpg // 2, seq=seq,
                                     win=tq + 2 * WINDOW)
            o_lat = _attn_call(geom, kern, [sink], [pl.BlockSpec((None, hpg, LANES), lambda bi, g, qi: (g, 0, 0))],
                               q, k, v, has_lat=True, tq=tq, q_per=hpg // 2, k_per=1, v_per=1, o_per=hpg // 2,
                               n_groups=SWA_KV, name="attn_swa")
            o_ctx = None
            w_mo = swa_w_out[0]

        h = _ffn_call(geom, h, mod, g8[i], w_in_b[i, 1], w_out_b[i, 1], k=6, gi=4,
                      pre=(o_lat, o_ctx, w_mo.astype(BF16)), n_tiles=geom.n_lat if last else None,
                      name=f"ffn_b{i}")
    return h.reshape(batch, seq, d)
```

```python
import functools
import math

import numpy as np
import jax
import jax.numpy as jnp
from jax import lax
from jax.experimental import pallas as pl
from jax.experimental.pallas import tpu as pltpu

F32 = jnp.float32
BF16 = jnp.bfloat16

D_MODEL = 1024
D_FF = 2816
N_MOD = 9
NORM_EPS = 1e-6
GRID_W = 64
ROPE_THETA = 10000.0
NEG_INF = -1e30
WINDOW = 128
LOG2E = math.log2(math.e)

LANES = 128
HALF = LANES // 2
ROW_TILE = 512
FF_CHUNK = 256
MOD_ROWS = 16
VMEM_LIMIT = 52 * 1024 * 1024

DA_HEADS, DA_QK = 8, 64
GA_HEADS, GA_KV, GA_DIM = 8, 2, 128
MLA_HEADS, MLA_Q_RANK, MLA_KV_RANK, MLA_NOPE, MLA_ROPE, MLA_V = 16, 256, 128, 64, 32, 64
SWA_HEADS, SWA_KV, SWA_DIM = 16, 2, 64


def _cparams(n_axes):
    return pltpu.CompilerParams(dimension_semantics=("arbitrary",) * n_axes,
                                vmem_limit_bytes=VMEM_LIMIT)


def _rms(x, g):
    return x * lax.rsqrt(jnp.mean(x * x, axis=-1, keepdims=True) + NORM_EPS) * g


def _mod(mod_ref, k):
    return mod_ref[:, k * D_MODEL:(k + 1) * D_MODEL]


def _modnorm(x, g, mod_ref, k):
    return _rms(x, g) * (1.0 + _mod(mod_ref, k + 1)) + _mod(mod_ref, k)


def _rope(x, c, s):
    return x * c + pltpu.roll(x, HALF, 1) * s


def _ada_kernel(c_ref, w_ref, b_ref, o_ref):
    c = c_ref[...]
    sc = c / (1.0 + jnp.exp(-c))
    o_ref[...] = jnp.dot(sc, w_ref[...], precision=lax.Precision.HIGHEST,
                         preferred_element_type=F32) + b_ref[...]


def _ada_call(cond, ada_w, ada_b):
    depth, d, n = ada_w.shape
    cw = 1024
    return pl.pallas_call(
        _ada_kernel,
        grid=(depth, n // cw),
        in_specs=[pl.BlockSpec((MOD_ROWS, d), lambda l, j: (0, 0)),
                  pl.BlockSpec((None, d, cw), lambda l, j: (l, 0, j)),
                  pl.BlockSpec((None, 1, cw), lambda l, j: (l, 0, j))],
        out_specs=pl.BlockSpec((None, MOD_ROWS, cw), lambda l, j: (l, 0, j)),
        out_shape=jax.ShapeDtypeStruct((depth, MOD_ROWS, n), F32),
        compiler_params=_cparams(2),
        name="adaln",
    )(cond, ada_w, ada_b.reshape(depth, 1, n))


class _Geom:
    def __init__(self, batch, seq, ctx):
        self.batch, self.seq, self.ctx = batch, seq, ctx
        self.n_lat_rows = batch * seq
        self.n_rows = batch * (seq + ctx)
        self.n_lat = self.n_lat_rows // ROW_TILE
        self.n_tiles = self.n_rows // ROW_TILE
        self.tiles_per_batch = seq // ROW_TILE

    def mod_row(self, t):
        return jnp.where(t < self.n_lat, t // self.tiles_per_batch, self.batch)

    def rope_blk(self, t):
        return jnp.where(t < self.n_lat, t % self.tiles_per_batch, self.tiles_per_batch)


def _const_spec(shape):
    nd = len(shape)
    return pl.BlockSpec(shape, lambda *_: (0,) * nd, pipeline_mode=pl.Buffered(1))


def _ffn_kernel(*refs, k, gi, n_lat, has_pre, has_ctx):
    it = iter(refs)
    x_ref, mod_ref, g_ref, win_ref, wout_ref = (next(it) for _ in range(5))
    x = x_ref[...]
    if has_pre:
        ol_ref = next(it)
        oc_ref = next(it) if has_ctx else None
        wmo_ref = next(it)
        nblk = ol_ref.shape[0]
        o = jnp.concatenate([ol_ref[i] for i in range(nblk)], axis=1)
        if has_ctx:
            oc = jnp.concatenate([oc_ref[i] for i in range(nblk)], axis=1)
            o = jnp.where(pl.program_id(0) < n_lat, o, oc)
        y = jnp.dot(o, wmo_ref[...], preferred_element_type=F32)
        x = x + _mod(mod_ref, 5) * _rms(y, g_ref[3:4, :])
    out_ref = next(it)

    u = _modnorm(x, g_ref[gi:gi + 1, :], mod_ref, k).astype(BF16)

    def gate_up(c):
        lo = c * FF_CHUNK
        return (jnp.dot(u, win_ref[:, lo:lo + FF_CHUNK], preferred_element_type=F32),
                jnp.dot(u, win_ref[:, D_FF + lo:D_FF + lo + FF_CHUNK], preferred_element_type=F32))

    n_chunks = D_FF // FF_CHUNK
    acc = None
    nxt = gate_up(0)
    for c in range(n_chunks):
        a, b = nxt
        if c + 1 < n_chunks:
            nxt = gate_up(c + 1)
        act = (a / (1.0 + jnp.exp(-a)) * b).astype(BF16)
        part = jnp.dot(act, wout_ref[c * FF_CHUNK:(c + 1) * FF_CHUNK, :], preferred_element_type=F32)
        acc = part if acc is None else acc + part
    out_ref[...] = x + 0.5 * _mod(mod_ref, k + 2) * _rms(acc, g_ref[gi + 1:gi + 2, :])


def _ffn_call(geom, h, mod, g8, w_in, w_out, *, k, gi, pre=None, n_tiles=None, name):
    n_tiles = geom.n_tiles if n_tiles is None else n_tiles
    d = D_MODEL
    in_specs = [pl.BlockSpec((ROW_TILE, d), lambda t: (t, 0)),
                pl.BlockSpec((None, 1, N_MOD * d), lambda t: (geom.mod_row(t), 0, 0)),
                _const_spec(g8.shape), _const_spec(w_in.shape), _const_spec(w_out.shape)]
    args = [h, mod, g8, w_in, w_out]
    has_pre = pre is not None
    has_ctx = False
    if has_pre:
        o_lat, o_ctx, w_mo = pre
        nblk = o_lat.shape[0]
        in_specs.append(pl.BlockSpec((nblk, ROW_TILE, LANES),
                                     lambda t: (0, jnp.minimum(t, geom.n_lat - 1), 0)))
        args.append(o_lat)
        has_ctx = o_ctx is not None
        if has_ctx:
            in_specs.append(pl.BlockSpec((nblk, ROW_TILE, LANES),
                                         lambda t: (0, jnp.maximum(t - geom.n_lat, 0), 0)))
            args.append(o_ctx)
        in_specs.append(_const_spec(w_mo.shape))
        args.append(w_mo)
    kern = functools.partial(_ffn_kernel, k=k, gi=gi, n_lat=geom.n_lat, has_pre=has_pre, has_ctx=has_ctx)
    return pl.pallas_call(
        kern,
        grid=(n_tiles,),
        in_specs=in_specs,
        out_specs=pl.BlockSpec((ROW_TILE, d), lambda t: (t, 0)),
        out_shape=jax.ShapeDtypeStruct((n_tiles * ROW_TILE, d), F32),
        compiler_params=_cparams(1),
        name=name,
    )(*args)


def _store_blocks(dst_ref, x, n, fn=None):
    for i in range(n):
        blk = x[:, i * LANES:(i + 1) * LANES]
        if fn is not None:
            blk = fn(blk)
        dst_ref[i] = blk.astype(BF16)


def _proj0_kernel(x_ref, mod_ref, g_ref, w_ref, cq_ref, sq_ref, ck_ref, sk_ref, q_ref, k_ref, v_ref, *, nq, nk, nv):
    u = _modnorm(x_ref[...], g_ref[2:3, :], mod_ref, 3).astype(BF16)
    cq, sq, ck, sk = cq_ref[...], sq_ref[...], ck_ref[...], sk_ref[...]
    t = jnp.dot(u, w_ref[...], preferred_element_type=F32)
    _store_blocks(q_ref, t, nq, lambda b: _rope(b, cq, sq))
    _store_blocks(k_ref, t[:, nq * LANES:], nk, lambda b: _rope(b, ck, sk))
    _store_blocks(v_ref, t[:, (nq + nk) * LANES:], nv)


def _proj1_kernel(x_ref, mod_ref, g_ref, w_ref, qn_ref, kn_ref, cq_ref, sq_ref, ck_ref, sk_ref,
                  q_ref, k_ref, v_ref, *, nq, nk, nv):
    u = _modnorm(x_ref[...], g_ref[2:3, :], mod_ref, 3).astype(BF16)
    cq, sq, ck, sk = cq_ref[...], sq_ref[...], ck_ref[...], sk_ref[...]
    qn, kn = qn_ref[...], kn_ref[...]
    t = jnp.dot(u, w_ref[...], preferred_element_type=F32)
    _store_blocks(q_ref, t, nq, lambda b: _rope(_rms(b, qn), cq, sq))
    _store_blocks(k_ref, t[:, nq * LANES:], nk, lambda b: _rope(_rms(b, kn), ck, sk))
    _store_blocks(v_ref, t[:, (nq + nk) * LANES:], nv)


def _proj2_kernel(x_ref, mod_ref, g_ref, w1_ref, qn_ref, kvn_ref, wuq_ref, wuk_ref, wuv_ref,
                  cq_ref, sq_ref, ck_ref, sk_ref, q_ref, k_ref, v_ref):
    u = _modnorm(x_ref[...], g_ref[2:3, :], mod_ref, 3).astype(BF16)
    cq, sq, ck, sk = cq_ref[...], sq_ref[...], ck_ref[...], sk_ref[...]
    t = jnp.dot(u, w1_ref[...], preferred_element_type=F32)
    cqv = _rms(t[:, :MLA_Q_RANK], qn_ref[...]).astype(BF16)
    ckv = _rms(t[:, MLA_Q_RANK:MLA_Q_RANK + MLA_KV_RANK], kvn_ref[...]).astype(BF16)
    kr = _rope(t[:, MLA_Q_RANK + MLA_KV_RANK:], ck, sk)
    qa = jnp.dot(cqv, wuq_ref[...], preferred_element_type=F32)
    _store_blocks(q_ref, qa, MLA_HEADS, lambda b: _rope(b, cq, sq))
    ka = jnp.dot(ckv, wuk_ref[...], preferred_element_type=F32)
    _store_blocks(k_ref, ka, MLA_HEADS, lambda b: b + kr)
    va = jnp.dot(ckv, wuv_ref[...], preferred_element_type=F32)
    _store_blocks(v_ref, va, MLA_HEADS // 2)


def _proj_call(geom, kern, h, mod, g8, consts, tables, nq, nk, nv, name):
    d = D_MODEL
    tile = lambda t: (t, 0)
    rope_spec = pl.BlockSpec((ROW_TILE, LANES), lambda t: (geom.rope_blk(t), 0))
    in_specs = ([pl.BlockSpec((ROW_TILE, d), tile),
                 pl.BlockSpec((None, 1, N_MOD * d), lambda t: (geom.mod_row(t), 0, 0)),
                 _const_spec(g8.shape)]
                + [_const_spec(c.shape) for c in consts]
                + [rope_spec] * 4)
    blk = lambda n: pl.BlockSpec((n, ROW_TILE, LANES), lambda t: (0, t, 0))
    shp = lambda n: jax.ShapeDtypeStruct((n, geom.n_rows, LANES), BF16)
    return pl.pallas_call(
        kern,
        grid=(geom.n_tiles,),
        in_specs=in_specs,
        out_specs=[blk(nq), blk(nk), blk(nv)],
        out_shape=[shp(nq), shp(nk), shp(nv)],
        compiler_params=_cparams(1),
        name=name,
    )(h, mod, g8, *consts, *tables)


def _qk(qs, k):
    return lax.dot_general(qs, k, (((1,), (1,)), ((), ())), preferred_element_type=F32)


def _pv_with_sums(p, v):
    v_ext = jnp.concatenate([v, jnp.ones_like(v)], axis=1)
    return jnp.dot(p, v_ext, preferred_element_type=F32)


def _flash(qs, chunks):
    def scores(chunk):
        k_ref, hk, _, _, start, size = chunk
        return _qk(qs, k_ref[hk, pl.ds(start, size), :])

    m = acc = None
    s_next = scores(chunks[0])
    for idx, (_, _, v_ref, hv, start, size) in enumerate(chunks):
        s = s_next
        if idx + 1 < len(chunks):
            s_next = scores(chunks[idx + 1])
        mc = jnp.max(s, axis=1, keepdims=True)
        m_new = mc if m is None else jnp.maximum(m, mc)
        p = jnp.exp2(s - m_new).astype(BF16)
        pv = _pv_with_sums(p, v_ref[hv, pl.ds(start, size), :])
        acc = pv if m is None else jnp.exp2(m - m_new) * acc + pv
        m = m_new
    return acc[:, :LANES], acc[:, LANES:]


def _kv_chunks(kc_ref, vc_ref, kl_ref, vl_ref, hk, hv, n_ctx_keys, n_lat_keys, tk):
    chunks = [(kc_ref, hk, vc_ref, hv, 0, n_ctx_keys)]
    if kl_ref is not None:
        chunks += [(kl_ref, hk, vl_ref, hv, j * tk, tk) for j in range(n_lat_keys // tk)]
    return chunks


def _split_refs(refs, n_lead, has_lat):
    lead = refs[:n_lead]
    rest = refs[n_lead:]
    q_ref, kc_ref, vc_ref = rest[:3]
    if has_lat:
        kl_ref, vl_ref, o_ref = rest[3:6]
    else:
        kl_ref = vl_ref = None
        o_ref = rest[3]
    return lead, q_ref, kc_ref, vc_ref, kl_ref, vl_ref, o_ref


def _lane_lo(rows):
    lane = lax.broadcasted_iota(jnp.int32, (rows, LANES), 1)
    return (lane % HALF) < (HALF // 2)


def _attn_diff_kernel(*refs, has_lat, n_heads, n_ctx_keys, n_lat_keys, tk, lam_init):
    (lam_ref, subln_ref), q_ref, kc_ref, vc_ref, kl_ref, vl_ref, o_ref = _split_refs(refs, 2, has_lat)
    tq = q_ref.shape[1]
    lo = _lane_lo(tq)
    lf = lam_ref[...]
    lam = (jnp.exp(jnp.sum(lf[0:1] * lf[1:2], keepdims=True))
           - jnp.exp(jnp.sum(lf[2:3] * lf[3:4], keepdims=True)) + lam_init)
    subln = subln_ref[...]

    def head(h, carry):
        q = q_ref[h]
        zero = jnp.zeros_like(q)
        qs = jnp.concatenate([jnp.where(lo, q, zero), jnp.where(lo, zero, q)], axis=0)
        acc, l = _flash(qs, _kv_chunks(kc_ref, vc_ref, kl_ref, vl_ref, h, h, n_ctx_keys, n_lat_keys, tk))
        o = acc / l
        o = o[:tq] - lam * o[tq:]
        o_ref[h] = (_rms(o, subln) * (1.0 - lam_init)).astype(BF16)
        return carry

    lax.fori_loop(0, n_heads, head, 0)


def _attn_gqa_kernel(*refs, has_lat, grp, n_ctx_keys, n_lat_keys, tk):
    _, q_ref, kc_ref, vc_ref, kl_ref, vl_ref, o_ref = _split_refs(refs, 0, has_lat)
    tq = q_ref.shape[1]
    qs = jnp.concatenate([q_ref[j] for j in range(grp)], axis=0)
    acc, l = _flash(qs, _kv_chunks(kc_ref, vc_ref, kl_ref, vl_ref, 0, 0, n_ctx_keys, n_lat_keys, tk))
    o = acc / l
    for j in range(grp):
        o_ref[j] = o[j * tq:(j + 1) * tq].astype(BF16)


def _attn_mla_kernel(*refs, has_lat, n_pairs, n_ctx_keys, n_lat_keys, tk):
    _, q_ref, kc_ref, vc_ref, kl_ref, vl_ref, o_ref = _split_refs(refs, 0, has_lat)
    tq = q_ref.shape[1]
    lane = lax.broadcasted_iota(jnp.int32, (tq, LANES), 1)

    def pair(p, carry):
        outs = []
        for e in range(2):
            h = 2 * p + e
            acc, l = _flash(q_ref[h], _kv_chunks(kc_ref, vc_ref, kl_ref, vl_ref, h, p, n_ctx_keys, n_lat_keys, tk))
            outs.append(acc / l)
        o_ref[p] = jnp.where(lane < HALF, outs[0], outs[1]).astype(BF16)
        return carry

    lax.fori_loop(0, n_pairs, pair, 0)


def _attn_swa_kernel(sink_ref, q_ref, kc_ref, vc_ref, kl_ref, vl_ref, o_ref, *, n_pairs, seq, win):
    tq = q_ref.shape[1]
    i = pl.program_id(2)
    start = pl.multiple_of(jnp.clip(i * tq - WINDOW, 0, seq - win), WINDOW)
    lo = _lane_lo(tq)
    lane = lax.broadcasted_iota(jnp.int32, (tq, LANES), 1)
    qpos = i * tq + lax.broadcasted_iota(jnp.int32, (tq, win), 0)
    kpos = start + lax.broadcasted_iota(jnp.int32, (tq, win), 1)
    bias = jnp.where(jnp.abs(qpos - kpos) <= WINDOW, 0.0, NEG_INF)

    n_heads = 2 * n_pairs
    zero = jnp.zeros((tq, LANES), BF16)
    q_rows = []
    for j in range(n_pairs):
        q = q_ref[j]
        q_rows += [jnp.where(lo, q, zero), jnp.where(lo, zero, q)]
    qs = jnp.concatenate(q_rows, axis=0)
    sink = jnp.concatenate([jnp.broadcast_to(sink_ref[h:h + 1, 0:1], (tq, 1)) for h in range(n_heads)], axis=0)
    s_ctx = _qk(qs, kc_ref[0])
    s_band = _qk(qs, kl_ref[0, pl.ds(start, win), :]) + jnp.concatenate([bias] * n_heads, axis=0)
    m = jnp.maximum(jnp.maximum(jnp.max(s_ctx, axis=1, keepdims=True),
                                jnp.max(s_band, axis=1, keepdims=True)), sink)
    acc = (_pv_with_sums(jnp.exp2(s_ctx - m).astype(BF16), vc_ref[0])
           + _pv_with_sums(jnp.exp2(s_band - m).astype(BF16), vl_ref[0, pl.ds(start, win), :]))
    o = acc[:, :LANES] / (acc[:, LANES:] + jnp.exp2(sink - m))
    for j in range(n_pairs):
        o_ref[j] = jnp.where(lane < HALF, o[2 * j * tq:(2 * j + 1) * tq],
                             o[(2 * j + 1) * tq:(2 * j + 2) * tq]).astype(BF16)


def _attn_call(geom, kern, lead, lead_specs, q, k, v, *, has_lat, tq, q_per, k_per, v_per, o_per, n_groups, name):
    b, s, c = geom.batch, geom.seq, geom.ctx
    ctx_blk0 = geom.n_lat_rows // c
    if has_lat:
        n_q = s // tq
        q_map = lambda bi, g, i: (g, bi * n_q + i, 0)
        o_rows = geom.n_lat_rows
    else:
        assert tq == c
        n_q = 1
        q_map = lambda bi, g, i: (g, ctx_blk0 + bi, 0)
        o_rows = b * c
    o_map = lambda bi, g, i: (g, bi * n_q + i, 0)
    ctx_map = lambda bi, g, i: (g, ctx_blk0 + bi, 0)
    lat_map = lambda bi, g, i: (g, bi, 0)
    in_specs = list(lead_specs) + [pl.BlockSpec((q_per, tq, LANES), q_map),
                                   pl.BlockSpec((k_per, c, LANES), ctx_map),
                                   pl.BlockSpec((v_per, c, LANES), ctx_map)]
    args = list(lead) + [q, k, v]
    if has_lat:
        in_specs += [pl.BlockSpec((k_per, s, LANES), lat_map), pl.BlockSpec((v_per, s, LANES), lat_map)]
        args += [k, v]
    return pl.pallas_call(
        kern,
        grid=(b, n_groups, n_q),
        in_specs=in_specs,
        out_specs=pl.BlockSpec((o_per, tq, LANES), o_map),
        out_shape=jax.ShapeDtypeStruct((o_per * n_groups, o_rows, LANES), BF16),
        compiler_params=_cparams(3),
        name=name,
    )(*args)


def _rope_tables(seq, rot_dim, aidx, active, scale):
    scale = scale * LOG2E
    pos = np.arange(seq)
    row = jnp.asarray(pos // GRID_W, F32)
    col = jnp.asarray(pos % GRID_W, F32)
    n_axis = rot_dim // 4
    freqs = ROPE_THETA ** (-jnp.arange(n_axis, dtype=F32) / n_axis)
    ang = jnp.concatenate([row[:, None] * freqs, col[:, None] * freqs], axis=-1)
    a = ang[:, np.asarray(aidx)]
    act = jnp.asarray(active)[None, :]
    sign = jnp.asarray(np.where(np.arange(LANES) < HALF, -1.0, 1.0), F32)[None, :]
    c = jnp.where(act, jnp.cos(a), 1.0)
    sn = jnp.where(act, jnp.sin(a) * sign, 0.0)
    c = jnp.concatenate([c, jnp.ones((ROW_TILE, LANES), F32)], axis=0)
    sn = jnp.concatenate([sn, jnp.zeros((ROW_TILE, LANES), F32)], axis=0)
    return (c * scale, sn * scale), (c, sn)


def _take_cols(w, idx):
    idx = np.asarray(idx)
    out = jnp.take(w, jnp.asarray(np.maximum(idx, 0)), axis=1)
    if (idx < 0).any():
        out = jnp.where(jnp.asarray(idx >= 0)[None, :], out, 0.0)
    return out


def _eo(d):
    return np.concatenate([np.arange(0, d, 2), np.arange(1, d, 2)])


def _pair_layout():
    ev, od = np.arange(0, 64, 2), np.arange(1, 64, 2)
    which = np.concatenate([np.zeros(32), np.ones(32), np.zeros(32), np.ones(32)]).astype(np.int64)
    dim = np.concatenate([ev, ev, od, od])
    return which, dim


def kernel(x, c, ctx, c_ctx, ada_w, ada_b, norm_g, ffn_w_in, ffn_w_out, da_w_in, da_lambda, da_subln, da_w_out,
           ga_w_in, ga_q_norm, ga_k_norm, ga_w_out, mla_w_in, mla_q_norm, mla_kv_norm, mla_w_uq, mla_w_ukv,
           mla_w_out, swa_w_in, swa_sink, swa_w_out):
    batch, seq, d = x.shape
    n_ctx = ctx.shape[1]
    depth = ada_w.shape[0]
    assert d == D_MODEL and seq % ROW_TILE == 0 and (batch * n_ctx) % ROW_TILE == 0
    assert n_ctx % 256 == 0 and seq % 512 == 0 and batch < MOD_ROWS and depth == 4
    geom = _Geom(batch, seq, n_ctx)

    cond = jnp.zeros((MOD_ROWS, d), F32).at[:batch].set(c).at[batch].set(c_ctx)
    mods = _ada_call(cond, ada_w, ada_b).reshape(depth, MOD_ROWS, 1, N_MOD * d)

    h = jnp.concatenate([x.reshape(batch * seq, d), ctx.reshape(batch * n_ctx, d)], axis=0)
    g8 = jnp.concatenate([norm_g, jnp.ones((depth, 2, d), F32)], axis=1)
    w_in_b = ffn_w_in.astype(BF16)
    w_out_b = ffn_w_out.astype(BF16)

    which, pdim = _pair_layout()
    tk = 512

    for i in range(depth):
        mod = mods[i]
        last = i == depth - 1
        h = _ffn_call(geom, h, mod, g8[i], w_in_b[i, 0], w_out_b[i, 0], k=0, gi=0, name=f"ffn_a{i}")

        if i == 0:
            w = da_w_in[0]
            hq = DA_HEADS * 2 * DA_QK
            qcols = np.concatenate([hd * 128 + which * 64 + pdim for hd in range(DA_HEADS)])
            wcat = jnp.concatenate([_take_cols(w, qcols), _take_cols(w, hq + qcols), w[:, 2 * hq:]], axis=1)
            tq_tab, tk_tab = _rope_tables(seq, DA_QK, np.arange(LANES) % 32, np.ones(LANES, bool), DA_QK ** -0.5)
            q, k, v = _proj_call(geom, functools.partial(_proj0_kernel, nq=8, nk=8, nv=8), h, mod, g8[i],
                                 [wcat.astype(BF16)], [*tq_tab, *tk_tab], 8, 8, 8, name="proj_diff")
            lam_init = 0.8 - 0.6 * math.exp(-0.3 * i)
            lead = [da_lambda[0], da_subln[0].reshape(1, LANES)]
            lead_specs = [pl.BlockSpec((4, DA_QK), lambda bi, g, qi: (0, 0)),
                          pl.BlockSpec((1, LANES), lambda bi, g, qi: (0, 0))]
            hs = 4
            mk = lambda has_lat: functools.partial(_attn_diff_kernel, has_lat=has_lat, n_heads=hs, n_ctx_keys=n_ctx,
                                                   n_lat_keys=seq, tk=tk, lam_init=lam_init)
            o_lat = _attn_call(geom, mk(True), lead, lead_specs, q, k, v, has_lat=True, tq=512, q_per=hs, k_per=hs,
                               v_per=hs, o_per=hs, n_groups=DA_HEADS // hs, name="attn_diff")
            o_ctx = _attn_call(geom, mk(False), lead, lead_specs, q, k, v, has_lat=False, tq=n_ctx, q_per=hs,
                               k_per=hs, v_per=hs, o_per=hs, n_groups=DA_HEADS // hs, name="attn_diff_ctx")
            w_mo = da_w_out[0]
        elif i == 1:
            w = ga_w_in[0]
            eo = _eo(GA_DIM)
            qcols = np.concatenate([hd * GA_DIM + eo for hd in range(GA_HEADS)])
            kcols = GA_HEADS * GA_DIM + np.concatenate([hd * GA_DIM + eo for hd in range(GA_KV)])
            wcat = jnp.concatenate([_take_cols(w, qcols), _take_cols(w, kcols), w[:, (GA_HEADS + GA_KV) * GA_DIM:]],
                                   axis=1)
            tq_tab, tk_tab = _rope_tables(seq, GA_DIM, np.arange(LANES) % 64, np.ones(LANES, bool), GA_DIM ** -0.5)
            consts = [wcat.astype(BF16), ga_q_norm[0][eo].reshape(1, LANES), ga_k_norm[0][eo].reshape(1, LANES)]
            q, k, v = _proj_call(geom, functools.partial(_proj1_kernel, nq=8, nk=2, nv=2), h, mod, g8[i],
                                 consts, [*tq_tab, *tk_tab], 8, 2, 2, name="proj_gqa")
            grp = GA_HEADS // GA_KV
            mk = lambda has_lat: functools.partial(_attn_gqa_kernel, has_lat=has_lat, grp=grp, n_ctx_keys=n_ctx,
                                                   n_lat_keys=seq, tk=tk)
            o_lat = _attn_call(geom, mk(True), [], [], q, k, v, has_lat=True, tq=256, q_per=grp, k_per=1, v_per=1,
                               o_per=grp, n_groups=GA_KV, name="attn_gqa")
            o_ctx = _attn_call(geom, mk(False), [], [], q, k, v, has_lat=False, tq=n_ctx, q_per=grp, k_per=1,
                               v_per=1, o_per=grp, n_groups=GA_KV, name="attn_gqa_ctx")
            w_mo = ga_w_out[0]
        elif i == 2:
            lanes = np.arange(LANES)
            is_rope = ((lanes % HALF) >= 32) & ((lanes % HALF) < 48)
            is_nope = (lanes % HALF) < 32
            rope_dim = 2 * ((lanes % HALF) - 32) + (lanes >= HALF)
            nope_dim = (lanes % HALF) + 32 * (lanes >= HALF)
            w = mla_w_in[0]
            kr_cols = np.where(is_rope, MLA_Q_RANK + MLA_KV_RANK + rope_dim, -1)
            w1 = jnp.concatenate([w[:, :MLA_Q_RANK + MLA_KV_RANK], _take_cols(w, kr_cols)], axis=1)
            qd = MLA_NOPE + MLA_ROPE
            q_blk = np.where(is_nope, nope_dim, np.where(is_rope, MLA_NOPE + rope_dim, -1))
            q_cols = np.concatenate([np.where(q_blk >= 0, hd * qd + q_blk, -1) for hd in range(MLA_HEADS)])
            kvd = MLA_NOPE + MLA_V
            k_blk = np.where(is_nope, nope_dim, -1)
            k_cols = np.concatenate([np.where(k_blk >= 0, hd * kvd + k_blk, -1) for hd in range(MLA_HEADS)])
            v_cols = np.concatenate([hd * kvd + MLA_NOPE + np.arange(MLA_V) for hd in range(MLA_HEADS)])
            consts = [w1.astype(BF16), mla_q_norm[0].reshape(1, -1), mla_kv_norm[0].reshape(1, -1),
                      _take_cols(mla_w_uq[0], q_cols).astype(BF16), _take_cols(mla_w_ukv[0], k_cols).astype(BF16),
                      _take_cols(mla_w_ukv[0], v_cols).astype(BF16)]
            aidx = np.where(is_rope, (lanes % HALF) - 32, 0)
            tq_tab, tk_tab = _rope_tables(seq, MLA_ROPE, aidx, is_rope, (MLA_NOPE + MLA_ROPE) ** -0.5)
            q, k, v = _proj_call(geom, _proj2_kernel, h, mod, g8[i], consts, [*tq_tab, *tk_tab],
                                 MLA_HEADS, MLA_HEADS, MLA_HEADS // 2, name="proj_mla")
            n_pairs = 4
            mk = lambda has_lat: functools.partial(_attn_mla_kernel, has_lat=has_lat, n_pairs=n_pairs,
                                                   n_ctx_keys=n_ctx, n_lat_keys=seq, tk=tk)
            o_lat = _attn_call(geom, mk(True), [], [], q, k, v, has_lat=True, tq=1024, q_per=2 * n_pairs,
                               k_per=2 * n_pairs, v_per=n_pairs, o_per=n_pairs,
                               n_groups=MLA_HEADS // (2 * n_pairs), name="attn_mla")
            o_ctx = _attn_call(geom, mk(False), [], [], q, k, v, has_lat=False, tq=n_ctx, q_per=2 * n_pairs,
                               k_per=2 * n_pairs, v_per=n_pairs, o_per=n_pairs,
                               n_groups=MLA_HEADS // (2 * n_pairs), name="attn_mla_ctx")
            w_mo = mla_w_out[0]
        else:
            w = swa_w_in[0]
            hq = SWA_HEADS * SWA_DIM
            qcols = np.concatenate([(2 * p + which) * SWA_DIM + pdim for p in range(SWA_HEADS // 2)])
            kcols = np.concatenate([hq + g * SWA_DIM + pdim for g in range(SWA_KV)])
            vcols = np.concatenate([hq + SWA_KV * SWA_DIM + g * SWA_DIM + (np.arange(LANES) % SWA_DIM)
                                    for g in range(SWA_KV)])
            wcat = jnp.concatenate([_take_cols(w, qcols), _take_cols(w, kcols), _take_cols(w, vcols)], axis=1)
            tq_tab, tk_tab = _rope_tables(seq, SWA_DIM, np.arange(LANES) % 32, np.ones(LANES, bool), SWA_DIM ** -0.5)
            q, k, v = _proj_call(geom, functools.partial(_proj0_kernel, nq=8, nk=2, nv=2), h, mod, g8[i],
                                 [wcat.astype(BF16)], [*tq_tab, *tk_tab], 8, 2, 2, name="proj_swa")
            hpg = SWA_HEADS // SWA_KV
            sink = jnp.broadcast_to((swa_sink[0].astype(F32) * LOG2E).reshape(SWA_KV, hpg, 1), (SWA_KV, hpg, LANES))
            tq = 256
            kern = functools.partial(_attn_swa_kernel, n_pairs=hpg // 2, seq=seq,
                                     win=tq + 2 * WINDOW)
            o_lat = _attn_call(geom, kern, [sink], [pl.BlockSpec((None, hpg, LANES), lambda bi, g, qi: (g, 0, 0))],
                               q, k, v, has_lat=True, tq=tq, q_per=hpg // 2, k_per=1, v_per=1, o_per=hpg // 2,
                               n_groups=SWA_KV, name="attn_swa")
            o_ctx = None
            w_mo = swa_w_out[0]

        h = _ffn_call(geom, h, mod, g8[i], w_in_b[i, 1], w_out_b[i, 1], k=6, gi=4,
                      pre=(o_lat, o_ctx, w_mo.astype(BF16)), n_tiles=geom.n_lat if last else None,
                      name=f"ffn_b{i}")
    return h.reshape(batch, seq, d)
```

```python
import functools
import math

import numpy as np
import jax
import jax.numpy as jnp
from jax import lax
from jax.experimental import pallas as pl
from jax.experimental.pallas import tpu as pltpu

F32 = jnp.float32
BF16 = jnp.bfloat16

D_MODEL = 1024
D_FF = 2816
N_MOD = 9
NORM_EPS = 1e-6
GRID_W = 64
ROPE_THETA = 10000.0
NEG_INF = -1e30
WINDOW = 128
LOG2E = math.log2(math.e)

LANES = 128
HALF = LANES // 2
ROW_TILE = 512
FF_CHUNK = 256
MOD_ROWS = 16
VMEM_LIMIT = 52 * 1024 * 1024

DA_HEADS, DA_QK = 8, 64
GA_HEADS, GA_KV, GA_DIM = 8, 2, 128
MLA_HEADS, MLA_Q_RANK, MLA_KV_RANK, MLA_NOPE, MLA_ROPE, MLA_V = 16, 256, 128, 64, 32, 64
SWA_HEADS, SWA_KV, SWA_DIM = 16, 2, 64


def _cparams(n_axes):
    return pltpu.CompilerParams(dimension_semantics=("arbitrary",) * n_axes,
                                vmem_limit_bytes=VMEM_LIMIT)


def _rms(x, g):
    return x * lax.rsqrt(jnp.mean(x * x, axis=-1, keepdims=True) + NORM_EPS) * g


def _mod(mod_ref, k):
    return mod_ref[:, k * D_MODEL:(k + 1) * D_MODEL]


def _modnorm(x, g, mod_ref, k):
    return _rms(x, g) * (1.0 + _mod(mod_ref, k + 1)) + _mod(mod_ref, k)


def _rope(x, c, s):
    return x * c + pltpu.roll(x, HALF, 1) * s


def _ada_kernel(c_ref, w_ref, b_ref, o_ref):
    c = c_ref[...]
    sc = c / (1.0 + jnp.exp(-c))
    o_ref[...] = jnp.dot(sc, w_ref[...], precision=lax.Precision.HIGHEST,
                         preferred_element_type=F32) + b_ref[...]


def _ada_call(cond, ada_w, ada_b):
    depth, d, n = ada_w.shape
    cw = 1024
    return pl.pallas_call(
        _ada_kernel,
        grid=(depth, n // cw),
        in_specs=[pl.BlockSpec((MOD_ROWS, d), lambda l, j: (0, 0)),
                  pl.BlockSpec((None, d, cw), lambda l, j: (l, 0, j)),
                  pl.BlockSpec((None, 1, cw), lambda l, j: (l, 0, j))],
        out_specs=pl.BlockSpec((None, MOD_ROWS, cw), lambda l, j: (l, 0, j)),
        out_shape=jax.ShapeDtypeStruct((depth, MOD_ROWS, n), F32),
        compiler_params=_cparams(2),
        name="adaln",
    )(cond, ada_w, ada_b.reshape(depth, 1, n))


class _Geom:
    def __init__(self, batch, seq, ctx):
        self.batch, self.seq, self.ctx = batch, seq, ctx
        self.n_lat_rows = batch * seq
        self.n_rows = batch * (seq + ctx)
        self.n_lat = self.n_lat_rows // ROW_TILE
        self.n_tiles = self.n_rows // ROW_TILE
        self.tiles_per_batch = seq // ROW_TILE

    def mod_row(self, t):
        return jnp.where(t < self.n_lat, t // self.tiles_per_batch, self.batch)

    def rope_blk(self, t):
        return jnp.where(t < self.n_lat, t % self.tiles_per_batch, self.tiles_per_batch)


def _const_spec(shape):
    nd = len(shape)
    return pl.BlockSpec(shape, lambda *_: (0,) * nd, pipeline_mode=pl.Buffered(1))


def _ffn_kernel(*refs, k, gi, n_lat, split_in, has_pre, has_ctx):
    it = iter(refs)
    x_ref = next(it)
    xc_ref = next(it) if split_in else None
    mod_ref, g_ref, win_ref, wout_ref = (next(it) for _ in range(4))
    x = x_ref[...]
    if split_in:
        x = jnp.where(pl.program_id(0) < n_lat, x, xc_ref[...])
    if has_pre:
        ol_ref = next(it)
        oc_ref = next(it) if has_ctx else None
        wmo_ref = next(it)
        nblk = ol_ref.shape[0]
        o = jnp.concatenate([ol_ref[i] for i in range(nblk)], axis=1)
        if has_ctx:
            oc = jnp.concatenate([oc_ref[i] for i in range(nblk)], axis=1)
            o = jnp.where(pl.program_id(0) < n_lat, o, oc)
        y = jnp.dot(o, wmo_ref[...], preferred_element_type=F32)
        x = x + _mod(mod_ref, 5) * _rms(y, g_ref[3:4, :])
    out_ref = next(it)

    u = _modnorm(x, g_ref[gi:gi + 1, :], mod_ref, k).astype(BF16)

    def gate_up(c):
        lo = c * FF_CHUNK
        return (jnp.dot(u, win_ref[:, lo:lo + FF_CHUNK], preferred_element_type=F32),
                jnp.dot(u, win_ref[:, D_FF + lo:D_FF + lo + FF_CHUNK], preferred_element_type=F32))

    n_chunks = D_FF // FF_CHUNK
    acc = None
    nxt = gate_up(0)
    for c in range(n_chunks):
        a, b = nxt
        if c + 1 < n_chunks:
            nxt = gate_up(c + 1)
        act = (a / (1.0 + jnp.exp(-a)) * b).astype(BF16)
        part = jnp.dot(act, wout_ref[c * FF_CHUNK:(c + 1) * FF_CHUNK, :], preferred_element_type=F32)
        acc = part if acc is None else acc + part
    out_ref[...] = x + 0.5 * _mod(mod_ref, k + 2) * _rms(acc, g_ref[gi + 1:gi + 2, :])


def _ffn_call(geom, h, mod, g8, w_in, w_out, *, k, gi, pre=None, n_tiles=None, name):
    n_tiles = geom.n_tiles if n_tiles is None else n_tiles
    d = D_MODEL
    split_in = isinstance(h, tuple)
    if split_in:
        in_specs = [pl.BlockSpec((ROW_TILE, d), lambda t: (jnp.minimum(t, geom.n_lat - 1), 0)),
                    pl.BlockSpec((ROW_TILE, d), lambda t: (jnp.maximum(t - geom.n_lat, 0), 0))]
        args = list(h)
    else:
        in_specs = [pl.BlockSpec((ROW_TILE, d), lambda t: (t, 0))]
        args = [h]
    in_specs += [pl.BlockSpec((None, 1, N_MOD * d), lambda t: (geom.mod_row(t), 0, 0)),
                 _const_spec(g8.shape), _const_spec(w_in.shape), _const_spec(w_out.shape)]
    args += [mod, g8, w_in, w_out]
    has_pre = pre is not None
    has_ctx = False
    if has_pre:
        o_lat, o_ctx, w_mo = pre
        nblk = o_lat.shape[0]
        in_specs.append(pl.BlockSpec((nblk, ROW_TILE, LANES),
                                     lambda t: (0, jnp.minimum(t, geom.n_lat - 1), 0)))
        args.append(o_lat)
        has_ctx = o_ctx is not None
        if has_ctx:
            in_specs.append(pl.BlockSpec((nblk, ROW_TILE, LANES),
                                         lambda t: (0, jnp.maximum(t - geom.n_lat, 0), 0)))
            args.append(o_ctx)
        in_specs.append(_const_spec(w_mo.shape))
        args.append(w_mo)
    kern = functools.partial(_ffn_kernel, k=k, gi=gi, n_lat=geom.n_lat, split_in=split_in, has_pre=has_pre,
                             has_ctx=has_ctx)
    return pl.pallas_call(
        kern,
        grid=(n_tiles,),
        in_specs=in_specs,
        out_specs=pl.BlockSpec((ROW_TILE, d), lambda t: (t, 0)),
        out_shape=jax.ShapeDtypeStruct((n_tiles * ROW_TILE, d), F32),
        compiler_params=_cparams(1),
        name=name,
    )(*args)


def _store_blocks(dst_ref, x, n, fn=None):
    for i in range(n):
        blk = x[:, i * LANES:(i + 1) * LANES]
        if fn is not None:
            blk = fn(blk)
        dst_ref[i] = blk.astype(BF16)


def _proj0_kernel(x_ref, mod_ref, g_ref, w_ref, cq_ref, sq_ref, ck_ref, sk_ref, q_ref, k_ref, v_ref, *, nq, nk, nv):
    u = _modnorm(x_ref[...], g_ref[2:3, :], mod_ref, 3).astype(BF16)
    cq, sq, ck, sk = cq_ref[...], sq_ref[...], ck_ref[...], sk_ref[...]
    t = jnp.dot(u, w_ref[...], preferred_element_type=F32)
    _store_blocks(q_ref, t, nq, lambda b: _rope(b, cq, sq))
    _store_blocks(k_ref, t[:, nq * LANES:], nk, lambda b: _rope(b, ck, sk))
    _store_blocks(v_ref, t[:, (nq + nk) * LANES:], nv)


def _proj1_kernel(x_ref, mod_ref, g_ref, w_ref, qn_ref, kn_ref, cq_ref, sq_ref, ck_ref, sk_ref,
                  q_ref, k_ref, v_ref, *, nq, nk, nv):
    u = _modnorm(x_ref[...], g_ref[2:3, :], mod_ref, 3).astype(BF16)
    cq, sq, ck, sk = cq_ref[...], sq_ref[...], ck_ref[...], sk_ref[...]
    qn, kn = qn_ref[...], kn_ref[...]
    t = jnp.dot(u, w_ref[...], preferred_element_type=F32)
    _store_blocks(q_ref, t, nq, lambda b: _rope(_rms(b, qn), cq, sq))
    _store_blocks(k_ref, t[:, nq * LANES:], nk, lambda b: _rope(_rms(b, kn), ck, sk))
    _store_blocks(v_ref, t[:, (nq + nk) * LANES:], nv)


def _proj2_kernel(x_ref, mod_ref, g_ref, w1_ref, qn_ref, kvn_ref, wuq_ref, wuk_ref, wuv_ref,
                  cq_ref, sq_ref, ck_ref, sk_ref, q_ref, k_ref, v_ref):
    u = _modnorm(x_ref[...], g_ref[2:3, :], mod_ref, 3).astype(BF16)
    cq, sq, ck, sk = cq_ref[...], sq_ref[...], ck_ref[...], sk_ref[...]
    t = jnp.dot(u, w1_ref[...], preferred_element_type=F32)
    cqv = _rms(t[:, :MLA_Q_RANK], qn_ref[...]).astype(BF16)
    ckv = _rms(t[:, MLA_Q_RANK:MLA_Q_RANK + MLA_KV_RANK], kvn_ref[...]).astype(BF16)
    kr = _rope(t[:, MLA_Q_RANK + MLA_KV_RANK:], ck, sk)
    qa = jnp.dot(cqv, wuq_ref[...], preferred_element_type=F32)
    _store_blocks(q_ref, qa, MLA_HEADS, lambda b: _rope(b, cq, sq))
    ka = jnp.dot(ckv, wuk_ref[...], preferred_element_type=F32)
    _store_blocks(k_ref, ka, MLA_HEADS, lambda b: b + kr)
    va = jnp.dot(ckv, wuv_ref[...], preferred_element_type=F32)
    _store_blocks(v_ref, va, MLA_HEADS // 2)


def _proj_call(geom, kern, h, mod, g8, consts, tables, nq, nk, nv, name):
    d = D_MODEL
    tile = lambda t: (t, 0)
    rope_spec = pl.BlockSpec((ROW_TILE, LANES), lambda t: (geom.rope_blk(t), 0))
    in_specs = ([pl.BlockSpec((ROW_TILE, d), tile),
                 pl.BlockSpec((None, 1, N_MOD * d), lambda t: (geom.mod_row(t), 0, 0)),
                 _const_spec(g8.shape)]
                + [_const_spec(c.shape) for c in consts]
                + [rope_spec] * 4)
    blk = lambda n: pl.BlockSpec((n, ROW_TILE, LANES), lambda t: (0, t, 0))
    shp = lambda n: jax.ShapeDtypeStruct((n, geom.n_rows, LANES), BF16)
    return pl.pallas_call(
        kern,
        grid=(geom.n_tiles,),
        in_specs=in_specs,
        out_specs=[blk(nq), blk(nk), blk(nv)],
        out_shape=[shp(nq), shp(nk), shp(nv)],
        compiler_params=_cparams(1),
        name=name,
    )(h, mod, g8, *consts, *tables)


def _qk(qs, k):
    return lax.dot_general(qs, k, (((1,), (1,)), ((), ())), preferred_element_type=F32)


def _pv_with_sums(p, v):
    v_ext = jnp.concatenate([v, jnp.ones_like(v)], axis=1)
    return jnp.dot(p, v_ext, preferred_element_type=F32)


def _flash_stream(jobs):
    steps = [(j, c) for j, (_, chunks, _) in enumerate(jobs) for c in range(len(chunks))]
    qs_of = {}

    def scores(j, c):
        if j not in qs_of:
            qs_of[j] = jobs[j][0]()
        k_ref, hk, _, _, start, size = jobs[j][1][c]
        return _qk(qs_of[j], k_ref[hk, pl.ds(start, size), :])

    m = acc = None
    s_next = scores(*steps[0])
    for idx, (j, c) in enumerate(steps):
        s = s_next
        if idx + 1 < len(steps):
            s_next = scores(*steps[idx + 1])
        _, _, v_ref, hv, start, size = jobs[j][1][c]
        mc = jnp.max(s, axis=1, keepdims=True)
        m_new = mc if c == 0 else jnp.maximum(m, mc)
        p = jnp.exp2(s - m_new).astype(BF16)
        pv = _pv_with_sums(p, v_ref[hv, pl.ds(start, size), :])
        acc = pv if c == 0 else jnp.exp2(m - m_new) * acc + pv
        m = m_new
        if c == len(jobs[j][1]) - 1:
            qs_of.pop(j)
            jobs[j][2](acc[:, :LANES], acc[:, LANES:])


def _kv_chunks(kc_ref, vc_ref, kl_ref, vl_ref, hk, hv, n_ctx_keys, n_lat_keys, tk):
    chunks = [(kc_ref, hk, vc_ref, hv, 0, n_ctx_keys)]
    if kl_ref is not None:
        chunks += [(kl_ref, hk, vl_ref, hv, j * tk, tk) for j in range(n_lat_keys // tk)]
    return chunks


def _split_refs(refs, n_lead, has_lat):
    lead = refs[:n_lead]
    rest = refs[n_lead:]
    q_ref, kc_ref, vc_ref = rest[:3]
    if has_lat:
        kl_ref, vl_ref, o_ref = rest[3:6]
    else:
        kl_ref = vl_ref = None
        o_ref = rest[3]
    return lead, q_ref, kc_ref, vc_ref, kl_ref, vl_ref, o_ref


def _lane_lo(rows):
    lane = lax.broadcasted_iota(jnp.int32, (rows, LANES), 1)
    return (lane % HALF) < (HALF // 2)


def _attn_diff_kernel(*refs, has_lat, n_heads, n_ctx_keys, n_lat_keys, tk, lam_init):
    (lam_ref, subln_ref), q_ref, kc_ref, vc_ref, kl_ref, vl_ref, o_ref = _split_refs(refs, 2, has_lat)
    tq = q_ref.shape[1]
    lo = _lane_lo(tq)
    lf = lam_ref[...]
    lam = (jnp.exp(jnp.sum(lf[0:1] * lf[1:2], keepdims=True))
           - jnp.exp(jnp.sum(lf[2:3] * lf[3:4], keepdims=True)) + lam_init)
    subln = subln_ref[...]

    def job(h):
        def make_qs():
            q = q_ref[h]
            zero = jnp.zeros_like(q)
            return jnp.concatenate([jnp.where(lo, q, zero), jnp.where(lo, zero, q)], axis=0)

        def done(acc, l):
            o = acc / l
            o = o[:tq] - lam * o[tq:]
            o_ref[h] = (_rms(o, subln) * (1.0 - lam_init)).astype(BF16)

        return make_qs, _kv_chunks(kc_ref, vc_ref, kl_ref, vl_ref, h, h, n_ctx_keys, n_lat_keys, tk), done

    def head_pair(i, carry):
        _flash_stream([job(2 * i), job(2 * i + 1)])
        return carry

    lax.fori_loop(0, n_heads // 2, head_pair, 0)


def _attn_gqa_kernel(*refs, has_lat, grp, n_ctx_keys, n_lat_keys, tk):
    _, q_ref, kc_ref, vc_ref, kl_ref, vl_ref, o_ref = _split_refs(refs, 0, has_lat)
    tq = q_ref.shape[1]

    def done(acc, l):
        o = acc / l
        for j in range(grp):
            o_ref[j] = o[j * tq:(j + 1) * tq].astype(BF16)

    _flash_stream([(lambda: jnp.concatenate([q_ref[j] for j in range(grp)], axis=0),
                    _kv_chunks(kc_ref, vc_ref, kl_ref, vl_ref, 0, 0, n_ctx_keys, n_lat_keys, tk), done)])


def _attn_mla_kernel(*refs, has_lat, n_pairs, n_ctx_keys, n_lat_keys, tk):
    _, q_ref, kc_ref, vc_ref, kl_ref, vl_ref, o_ref = _split_refs(refs, 0, has_lat)
    tq = q_ref.shape[1]
    lane = lax.broadcasted_iota(jnp.int32, (tq, LANES), 1)

    def pair(p, carry):
        even_out = []

        def job(e):
            h = 2 * p + e

            def done(acc, l):
                if e == 0:
                    even_out.append(acc / l)
                else:
                    o_ref[p] = jnp.where(lane < HALF, even_out[0], acc / l).astype(BF16)

            return ((lambda: q_ref[h]),
                    _kv_chunks(kc_ref, vc_ref, kl_ref, vl_ref, h, p, n_ctx_keys, n_lat_keys, tk), done)

        _flash_stream([job(0), job(1)])
        return carry

    lax.fori_loop(0, n_pairs, pair, 0)


def _attn_swa_kernel(sink_ref, q_ref, kc_ref, vc_ref, kl_ref, vl_ref, o_ref, *, n_pairs, pairs_per_stage, seq, win):
    tq = q_ref.shape[1]
    i = pl.program_id(2)
    start = pl.multiple_of(jnp.clip(i * tq - WINDOW, 0, seq - win), WINDOW)
    lo = _lane_lo(tq)
    lane = lax.broadcasted_iota(jnp.int32, (tq, LANES), 1)
    qpos = i * tq + lax.broadcasted_iota(jnp.int32, (tq, win), 0)
    kpos = start + lax.broadcasted_iota(jnp.int32, (tq, win), 1)
    bias = jnp.where(jnp.abs(qpos - kpos) <= WINDOW, 0.0, NEG_INF)

    zero = jnp.zeros((tq, LANES), BF16)
    n_stage = n_pairs // pairs_per_stage
    heads = 2 * pairs_per_stage
    bias_rows = jnp.concatenate([bias] * heads, axis=0)

    def scores(t):
        q_rows = []
        for j in range(t * pairs_per_stage, (t + 1) * pairs_per_stage):
            q = q_ref[j]
            q_rows += [jnp.where(lo, q, zero), jnp.where(lo, zero, q)]
        qs = jnp.concatenate(q_rows, axis=0)
        return _qk(qs, kc_ref[0]), _qk(qs, kl_ref[0, pl.ds(start, win), :]) + bias_rows

    def softmax(t, s_ctx, s_band):
        sink = jnp.concatenate([jnp.broadcast_to(sink_ref[h:h + 1, 0:1], (tq, 1))
                                for h in range(t * heads, (t + 1) * heads)], axis=0)
        m = jnp.maximum(jnp.maximum(jnp.max(s_ctx, axis=1, keepdims=True),
                                    jnp.max(s_band, axis=1, keepdims=True)), sink)
        return jnp.exp2(s_ctx - m).astype(BF16), jnp.exp2(s_band - m).astype(BF16), jnp.exp2(sink - m)

    def finish(t, p_ctx, p_band, p_sink):
        acc = (_pv_with_sums(p_ctx, vc_ref[0]) + _pv_with_sums(p_band, vl_ref[0, pl.ds(start, win), :]))
        o = acc[:, :LANES] / (acc[:, LANES:] + p_sink)
        for j in range(pairs_per_stage):
            o_ref[t * pairs_per_stage + j] = jnp.where(
                lane < HALF, o[2 * j * tq:(2 * j + 1) * tq], o[(2 * j + 1) * tq:(2 * j + 2) * tq]).astype(BF16)

    s_vals, p_vals = {}, {}
    for t in range(n_stage + 2):
        if t < n_stage:
            s_vals[t] = scores(t)
        if 0 <= t - 1 < n_stage:
            p_vals[t - 1] = softmax(t - 1, *s_vals.pop(t - 1))
        if 0 <= t - 2 < n_stage:
            finish(t - 2, *p_vals.pop(t - 2))


def _attn_call(geom, kern, lead, lead_specs, q, k, v, *, has_lat, tq, q_per, k_per, v_per, o_per, n_groups, name):
    b, s, c = geom.batch, geom.seq, geom.ctx
    ctx_blk0 = geom.n_lat_rows // c
    if has_lat:
        n_q = s // tq
        q_map = lambda bi, g, i: (g, bi * n_q + i, 0)
        o_rows = geom.n_lat_rows
    else:
        assert tq == c
        n_q = 1
        q_map = lambda bi, g, i: (g, ctx_blk0 + bi, 0)
        o_rows = b * c
    o_map = lambda bi, g, i: (g, bi * n_q + i, 0)
    ctx_map = lambda bi, g, i: (g, ctx_blk0 + bi, 0)
    lat_map = lambda bi, g, i: (g, bi, 0)
    in_specs = list(lead_specs) + [pl.BlockSpec((q_per, tq, LANES), q_map),
                                   pl.BlockSpec((k_per, c, LANES), ctx_map),
                                   pl.BlockSpec((v_per, c, LANES), ctx_map)]
    args = list(lead) + [q, k, v]
    if has_lat:
        in_specs += [pl.BlockSpec((k_per, s, LANES), lat_map), pl.BlockSpec((v_per, s, LANES), lat_map)]
        args += [k, v]
    return pl.pallas_call(
        kern,
        grid=(b, n_groups, n_q),
        in_specs=in_specs,
        out_specs=pl.BlockSpec((o_per, tq, LANES), o_map),
        out_shape=jax.ShapeDtypeStruct((o_per * n_groups, o_rows, LANES), BF16),
        compiler_params=_cparams(3),
        name=name,
    )(*args)


def _rope_tables(seq, rot_dim, aidx, active, scale):
    scale = scale * LOG2E
    pos = np.arange(seq)
    row = jnp.asarray(pos // GRID_W, F32)
    col = jnp.asarray(pos % GRID_W, F32)
    n_axis = rot_dim // 4
    freqs = ROPE_THETA ** (-jnp.arange(n_axis, dtype=F32) / n_axis)
    ang = jnp.concatenate([row[:, None] * freqs, col[:, None] * freqs], axis=-1)
    a = ang[:, np.asarray(aidx)]
    act = jnp.asarray(active)[None, :]
    sign = jnp.asarray(np.where(np.arange(LANES) < HALF, -1.0, 1.0), F32)[None, :]
    c = jnp.where(act, jnp.cos(a), 1.0)
    sn = jnp.where(act, jnp.sin(a) * sign, 0.0)
    c = jnp.concatenate([c, jnp.ones((ROW_TILE, LANES), F32)], axis=0)
    sn = jnp.concatenate([sn, jnp.zeros((ROW_TILE, LANES), F32)], axis=0)
    return (c * scale, sn * scale), (c, sn)


def _take_cols(w, idx):
    idx = np.asarray(idx)
    out = jnp.take(w, jnp.asarray(np.maximum(idx, 0)), axis=1)
    if (idx < 0).any():
        out = jnp.where(jnp.asarray(idx >= 0)[None, :], out, 0.0)
    return out


def _eo(d):
    return np.concatenate([np.arange(0, d, 2), np.arange(1, d, 2)])


def _pair_layout():
    ev, od = np.arange(0, 64, 2), np.arange(1, 64, 2)
    which = np.concatenate([np.zeros(32), np.ones(32), np.zeros(32), np.ones(32)]).astype(np.int64)
    dim = np.concatenate([ev, ev, od, od])
    return which, dim


def kernel(x, c, ctx, c_ctx, ada_w, ada_b, norm_g, ffn_w_in, ffn_w_out, da_w_in, da_lambda, da_subln, da_w_out,
           ga_w_in, ga_q_norm, ga_k_norm, ga_w_out, mla_w_in, mla_q_norm, mla_kv_norm, mla_w_uq, mla_w_ukv,
           mla_w_out, swa_w_in, swa_sink, swa_w_out):
    batch, seq, d = x.shape
    n_ctx = ctx.shape[1]
    depth = ada_w.shape[0]
    assert d == D_MODEL and seq % ROW_TILE == 0 and (batch * n_ctx) % ROW_TILE == 0
    assert n_ctx % 256 == 0 and seq % 512 == 0 and batch < MOD_ROWS and depth == 4
    geom = _Geom(batch, seq, n_ctx)

    cond = jnp.zeros((MOD_ROWS, d), F32).at[:batch].set(c).at[batch].set(c_ctx)
    mods = _ada_call(cond, ada_w, ada_b).reshape(depth, MOD_ROWS, 1, N_MOD * d)

    h = (x.reshape(batch * seq, d), ctx.reshape(batch * n_ctx, d))
    g8 = jnp.concatenate([norm_g, jnp.ones((depth, 2, d), F32)], axis=1)
    w_in_b = ffn_w_in.astype(BF16)
    w_out_b = ffn_w_out.astype(BF16)

    which, pdim = _pair_layout()
    tk = 512

    for i in range(depth):
        mod = mods[i]
        last = i == depth - 1
        h = _ffn_call(geom, h, mod, g8[i], w_in_b[i, 0], w_out_b[i, 0], k=0, gi=0, name=f"ffn_a{i}")

        if i == 0:
            w = da_w_in[0]
            hq = DA_HEADS * 2 * DA_QK
            qcols = np.concatenate([hd * 128 + which * 64 + pdim for hd in range(DA_HEADS)])
            wcat = jnp.concatenate([_take_cols(w, qcols), _take_cols(w, hq + qcols), w[:, 2 * hq:]], axis=1)
            tq_tab, tk_tab = _rope_tables(seq, DA_QK, np.arange(LANES) % 32, np.ones(LANES, bool), DA_QK ** -0.5)
            q, k, v = _proj_call(geom, functools.partial(_proj0_kernel, nq=8, nk=8, nv=8), h, mod, g8[i],
                                 [wcat.astype(BF16)], [*tq_tab, *tk_tab], 8, 8, 8, name="proj_diff")
            lam_init = 0.8 - 0.6 * math.exp(-0.3 * i)
            lead = [da_lambda[0], da_subln[0].reshape(1, LANES)]
            lead_specs = [pl.BlockSpec((4, DA_QK), lambda bi, g, qi: (0, 0)),
                          pl.BlockSpec((1, LANES), lambda bi, g, qi: (0, 0))]
            hs = 4
            mk = lambda has_lat: functools.partial(_attn_diff_kernel, has_lat=has_lat, n_heads=hs, n_ctx_keys=n_ctx,
                                                   n_lat_keys=seq, tk=tk, lam_init=lam_init)
            o_lat = _attn_call(geom, mk(True), lead, lead_specs, q, k, v, has_lat=True, tq=512, q_per=hs, k_per=hs,
                               v_per=hs, o_per=hs, n_groups=DA_HEADS // hs, name="attn_diff")
            o_ctx = _attn_call(geom, mk(False), lead, lead_specs, q, k, v, has_lat=False, tq=n_ctx, q_per=hs,
                               k_per=hs, v_per=hs, o_per=hs, n_groups=DA_HEADS // hs, name="attn_diff_ctx")
            w_mo = da_w_out[0]
        elif i == 1:
            w = ga_w_in[0]
            eo = _eo(GA_DIM)
            qcols = np.concatenate([hd * GA_DIM + eo for hd in range(GA_HEADS)])
            kcols = GA_HEADS * GA_DIM + np.concatenate([hd * GA_DIM + eo for hd in range(GA_KV)])
            wcat = jnp.concatenate([_take_cols(w, qcols), _take_cols(w, kcols), w[:, (GA_HEADS + GA_KV) * GA_DIM:]],
                                   axis=1)
            tq_tab, tk_tab = _rope_tables(seq, GA_DIM, np.arange(LANES) % 64, np.ones(LANES, bool), GA_DIM ** -0.5)
            consts = [wcat.astype(BF16), ga_q_norm[0][eo].reshape(1, LANES), ga_k_norm[0][eo].reshape(1, LANES)]
            q, k, v = _proj_call(geom, functools.partial(_proj1_kernel, nq=8, nk=2, nv=2), h, mod, g8[i],
                                 consts, [*tq_tab, *tk_tab], 8, 2, 2, name="proj_gqa")
            grp = GA_HEADS // GA_KV
            mk = lambda has_lat: functools.partial(_attn_gqa_kernel, has_lat=has_lat, grp=grp, n_ctx_keys=n_ctx,
                                                   n_lat_keys=seq, tk=tk)
            o_lat = _attn_call(geom, mk(True), [], [], q, k, v, has_lat=True, tq=256, q_per=grp, k_per=1, v_per=1,
                               o_per=grp, n_groups=GA_KV, name="attn_gqa")
            o_ctx = _attn_call(geom, mk(False), [], [], q, k, v, has_lat=False, tq=n_ctx, q_per=grp, k_per=1,
                               v_per=1, o_per=grp, n_groups=GA_KV, name="attn_gqa_ctx")
            w_mo = ga_w_out[0]
        elif i == 2:
            lanes = np.arange(LANES)
            is_rope = ((lanes % HALF) >= 32) & ((lanes % HALF) < 48)
            is_nope = (lanes % HALF) < 32
            rope_dim = 2 * ((lanes % HALF) - 32) + (lanes >= HALF)
            nope_dim = (lanes % HALF) + 32 * (lanes >= HALF)
            w = mla_w_in[0]
            kr_cols = np.where(is_rope, MLA_Q_RANK + MLA_KV_RANK + rope_dim, -1)
            w1 = jnp.concatenate([w[:, :MLA_Q_RANK + MLA_KV_RANK], _take_cols(w, kr_cols)], axis=1)
            qd = MLA_NOPE + MLA_ROPE
            q_blk = np.where(is_nope, nope_dim, np.where(is_rope, MLA_NOPE + rope_dim, -1))
            q_cols = np.concatenate([np.where(q_blk >= 0, hd * qd + q_blk, -1) for hd in range(MLA_HEADS)])
            kvd = MLA_NOPE + MLA_V
            k_blk = np.where(is_nope, nope_dim, -1)
            k_cols = np.concatenate([np.where(k_blk >= 0, hd * kvd + k_blk, -1) for hd in range(MLA_HEADS)])
            v_cols = np.concatenate([hd * kvd + MLA_NOPE + np.arange(MLA_V) for hd in range(MLA_HEADS)])
            consts = [w1.astype(BF16), mla_q_norm[0].reshape(1, -1), mla_kv_norm[0].reshape(1, -1),
                      _take_cols(mla_w_uq[0], q_cols).astype(BF16), _take_cols(mla_w_ukv[0], k_cols).astype(BF16),
                      _take_cols(mla_w_ukv[0], v_cols).astype(BF16)]
            aidx = np.where(is_rope, (lanes % HALF) - 32, 0)
            tq_tab, tk_tab = _rope_tables(seq, MLA_ROPE, aidx, is_rope, (MLA_NOPE + MLA_ROPE) ** -0.5)
            q, k, v = _proj_call(geom, _proj2_kernel, h, mod, g8[i], consts, [*tq_tab, *tk_tab],
                                 MLA_HEADS, MLA_HEADS, MLA_HEADS // 2, name="proj_mla")
            n_pairs = 4
            mk = lambda has_lat: functools.partial(_attn_mla_kernel, has_lat=has_lat, n_pairs=n_pairs,
                                                   n_ctx_keys=n_ctx, n_lat_keys=seq, tk=tk)
            o_lat = _attn_call(geom, mk(True), [], [], q, k, v, has_lat=True, tq=1024, q_per=2 * n_pairs,
                               k_per=2 * n_pairs, v_per=n_pairs, o_per=n_pairs,
                               n_groups=MLA_HEADS // (2 * n_pairs), name="attn_mla")
            o_ctx = _attn_call(geom, mk(False), [], [], q, k, v, has_lat=False, tq=n_ctx, q_per=2 * n_pairs,
                               k_per=2 * n_pairs, v_per=n_pairs, o_per=n_pairs,
                               n_groups=MLA_HEADS // (2 * n_pairs), name="attn_mla_ctx")
            w_mo = mla_w_out[0]
        else:
            w = swa_w_in[0]
            hq = SWA_HEADS * SWA_DIM
            qcols = np.concatenate([(2 * p + which) * SWA_DIM + pdim for p in range(SWA_HEADS // 2)])
            kcols = np.concatenate([hq + g * SWA_DIM + pdim for g in range(SWA_KV)])
            vcols = np.concatenate([hq + SWA_KV * SWA_DIM + g * SWA_DIM + (np.arange(LANES) % SWA_DIM)
                                    for g in range(SWA_KV)])
            wcat = jnp.concatenate([_take_cols(w, qcols), _take_cols(w, kcols), _take_cols(w, vcols)], axis=1)
            tq_tab, tk_tab = _rope_tables(seq, SWA_DIM, np.arange(LANES) % 32, np.ones(LANES, bool), SWA_DIM ** -0.5)
            q, k, v = _proj_call(geom, functools.partial(_proj0_kernel, nq=8, nk=2, nv=2), h, mod, g8[i],
                                 [wcat.astype(BF16)], [*tq_tab, *tk_tab], 8, 2, 2, name="proj_swa")
            hpg = SWA_HEADS // SWA_KV
            sink = jnp.broadcast_to((swa_sink[0].astype(F32) * LOG2E).reshape(SWA_KV, hpg, 1), (SWA_KV, hpg, LANES))
            tq = 256
            kern = functools.partial(_attn_swa_kernel, n_pairs=hpg // 2, pairs_per_stage=2, seq=seq,
                                     win=tq + 2 * WINDOW)
            o_lat = _attn_call(geom, kern, [sink], [pl.BlockSpec((None, hpg, LANES), lambda bi, g, qi: (g, 0, 0))],
                               q, k, v, has_lat=True, tq=tq, q_per=hpg // 2, k_per=1, v_per=1, o_per=hpg // 2,
                               n_groups=SWA_KV, name="attn_swa")
            o_ctx = None
            w_mo = swa_w_out[0]

        h = _ffn_call(geom, h, mod, g8[i], w_in_b[i, 1], w_out_b[i, 1], k=6, gi=4,
                      pre=(o_lat, o_ctx, w_mo.astype(BF16)), n_tiles=geom.n_lat if last else None,
                      name=f"ffn_b{i}")
    return h.reshape(batch, seq, d)
```

```python
import functools
import math

import numpy as np
import jax
import jax.numpy as jnp
from jax import lax
from jax.experimental import pallas as pl
from jax.experimental.pallas import tpu as pltpu

F32 = jnp.float32
BF16 = jnp.bfloat16

D_MODEL = 1024
D_FF = 2816
N_MOD = 9
NORM_EPS = 1e-6
GRID_W = 64
ROPE_THETA = 10000.0
NEG_INF = -1e30
WINDOW = 128
LOG2E = math.log2(math.e)

LANES = 128
HALF = LANES // 2
ROW_TILE = 512
FF_CHUNK = 256
MOD_ROWS = 16
VMEM_LIMIT = 52 * 1024 * 1024

DA_HEADS, DA_QK = 8, 64
GA_HEADS, GA_KV, GA_DIM = 8, 2, 128
MLA_HEADS, MLA_Q_RANK, MLA_KV_RANK, MLA_NOPE, MLA_ROPE, MLA_V = 16, 256, 128, 64, 32, 64
SWA_HEADS, SWA_KV, SWA_DIM = 16, 2, 64


def _cparams(n_axes):
    return pltpu.CompilerParams(dimension_semantics=("arbitrary",) * n_axes,
                                vmem_limit_bytes=VMEM_LIMIT)


def _rms(x, g):
    return x * lax.rsqrt(jnp.mean(x * x, axis=-1, keepdims=True) + NORM_EPS) * g


def _mod(mod_ref, k):
    return mod_ref[:, k * D_MODEL:(k + 1) * D_MODEL]


def _modnorm(x, g, mod_ref, k):
    return _rms(x, g) * (1.0 + _mod(mod_ref, k + 1)) + _mod(mod_ref, k)


def _rope(x, c, s):
    return x * c + pltpu.roll(x, HALF, 1) * s


def _ada_kernel(c_ref, w_ref, b_ref, o_ref):
    c = c_ref[...]
    sc = c / (1.0 + jnp.exp(-c))
    o_ref[...] = jnp.dot(sc, w_ref[...], precision=lax.Precision.HIGHEST,
                         preferred_element_type=F32) + b_ref[...]


def _ada_call(cond, ada_w, ada_b):
    depth, d, n = ada_w.shape
    cw = 1024
    return pl.pallas_call(
        _ada_kernel,
        grid=(depth, n // cw),
        in_specs=[pl.BlockSpec((MOD_ROWS, d), lambda l, j: (0, 0)),
                  pl.BlockSpec((None, d, cw), lambda l, j: (l, 0, j)),
                  pl.BlockSpec((None, 1, cw), lambda l, j: (l, 0, j))],
        out_specs=pl.BlockSpec((None, MOD_ROWS, cw), lambda l, j: (l, 0, j)),
        out_shape=jax.ShapeDtypeStruct((depth, MOD_ROWS, n), F32),
        compiler_params=_cparams(2),
        name="adaln",
    )(cond, ada_w, ada_b.reshape(depth, 1, n))


class _Geom:
    def __init__(self, batch, seq, ctx):
        self.batch, self.seq, self.ctx = batch, seq, ctx
        self.n_lat_rows = batch * seq
        self.n_rows = batch * (seq + ctx)
        self.n_lat = self.n_lat_rows // ROW_TILE
        self.n_tiles = self.n_rows // ROW_TILE
        self.tiles_per_batch = seq // ROW_TILE

    def mod_row(self, t):
        return jnp.where(t < self.n_lat, t // self.tiles_per_batch, self.batch)

    def rope_blk(self, t):
        return jnp.where(t < self.n_lat, t % self.tiles_per_batch, self.tiles_per_batch)


def _const_spec(shape):
    nd = len(shape)
    return pl.BlockSpec(shape, lambda *_: (0,) * nd, pipeline_mode=pl.Buffered(1))


def _ffn_kernel(*refs, k, gi, n_lat, split_in, has_pre, has_ctx):
    it = iter(refs)
    x_ref = next(it)
    xc_ref = next(it) if split_in else None
    mod_ref, g_ref, win_ref, wout_ref = (next(it) for _ in range(4))
    x = x_ref[...]
    if split_in:
        x = jnp.where(pl.program_id(0) < n_lat, x, xc_ref[...])
    if has_pre:
        ol_ref = next(it)
        oc_ref = next(it) if has_ctx else None
        wmo_ref = next(it)
        nblk = ol_ref.shape[0]
        o = jnp.concatenate([ol_ref[i] for i in range(nblk)], axis=1)
        if has_ctx:
            oc = jnp.concatenate([oc_ref[i] for i in range(nblk)], axis=1)
            o = jnp.where(pl.program_id(0) < n_lat, o, oc)
        y = jnp.dot(o, wmo_ref[...], preferred_element_type=F32)
        x = x + _mod(mod_ref, 5) * _rms(y, g_ref[3:4, :])
    out_ref = next(it)

    u = _modnorm(x, g_ref[gi:gi + 1, :], mod_ref, k).astype(BF16)

    def gate_up(c):
        lo = c * FF_CHUNK
        return (jnp.dot(u, win_ref[:, lo:lo + FF_CHUNK], preferred_element_type=F32),
                jnp.dot(u, win_ref[:, D_FF + lo:D_FF + lo + FF_CHUNK], preferred_element_type=F32))

    n_chunks = D_FF // FF_CHUNK
    acc = None
    nxt = gate_up(0)
    for c in range(n_chunks):
        a, b = nxt
        if c + 1 < n_chunks:
            nxt = gate_up(c + 1)
        act = (a / (1.0 + jnp.exp(-a)) * b).astype(BF16)
        part = jnp.dot(act, wout_ref[c * FF_CHUNK:(c + 1) * FF_CHUNK, :], preferred_element_type=F32)
        acc = part if acc is None else acc + part
    out_ref[...] = x + 0.5 * _mod(mod_ref, k + 2) * _rms(acc, g_ref[gi + 1:gi + 2, :])


def _ffn_call(geom, h, mod, g8, w_in, w_out, *, k, gi, pre=None, n_tiles=None, name):
    n_tiles = geom.n_tiles if n_tiles is None else n_tiles
    d = D_MODEL
    split_in = isinstance(h, tuple)
    if split_in:
        in_specs = [pl.BlockSpec((ROW_TILE, d), lambda t: (jnp.minimum(t, geom.n_lat - 1), 0)),
                    pl.BlockSpec((ROW_TILE, d), lambda t: (jnp.maximum(t - geom.n_lat, 0), 0))]
        args = list(h)
    else:
        in_specs = [pl.BlockSpec((ROW_TILE, d), lambda t: (t, 0))]
        args = [h]
    in_specs += [pl.BlockSpec((None, 1, N_MOD * d), lambda t: (geom.mod_row(t), 0, 0)),
                 _const_spec(g8.shape), _const_spec(w_in.shape), _const_spec(w_out.shape)]
    args += [mod, g8, w_in, w_out]
    has_pre = pre is not None
    has_ctx = False
    if has_pre:
        o_lat, o_ctx, w_mo = pre
        nblk = o_lat.shape[0]
        in_specs.append(pl.BlockSpec((nblk, ROW_TILE, LANES),
                                     lambda t: (0, jnp.minimum(t, geom.n_lat - 1), 0)))
        args.append(o_lat)
        has_ctx = o_ctx is not None
        if has_ctx:
            in_specs.append(pl.BlockSpec((nblk, ROW_TILE, LANES),
                                         lambda t: (0, jnp.maximum(t - geom.n_lat, 0), 0)))
            args.append(o_ctx)
        in_specs.append(_const_spec(w_mo.shape))
        args.append(w_mo)
    kern = functools.partial(_ffn_kernel, k=k, gi=gi, n_lat=geom.n_lat, split_in=split_in, has_pre=has_pre,
                             has_ctx=has_ctx)
    return pl.pallas_call(
        kern,
        grid=(n_tiles,),
        in_specs=in_specs,
        out_specs=pl.BlockSpec((ROW_TILE, d), lambda t: (t, 0)),
        out_shape=jax.ShapeDtypeStruct((n_tiles * ROW_TILE, d), F32),
        compiler_params=_cparams(1),
        name=name,
    )(*args)


def _store_blocks(dst_ref, x, n, fn=None, transpose=False):
    for i in range(n):
        blk = x[:, i * LANES:(i + 1) * LANES]
        if fn is not None:
            blk = fn(blk)
        dst_ref[i] = (blk.T if transpose else blk).astype(BF16)


def _proj0_kernel(x_ref, mod_ref, g_ref, w_ref, cq_ref, sq_ref, ck_ref, sk_ref, q_ref, k_ref, v_ref, *, nq, nk, nv,
                  v_transposed):
    u = _modnorm(x_ref[...], g_ref[2:3, :], mod_ref, 3).astype(BF16)
    cq, sq, ck, sk = cq_ref[...], sq_ref[...], ck_ref[...], sk_ref[...]
    t = jnp.dot(u, w_ref[...], preferred_element_type=F32)
    _store_blocks(q_ref, t, nq, lambda b: _rope(b, cq, sq))
    _store_blocks(k_ref, t[:, nq * LANES:], nk, lambda b: _rope(b, ck, sk))
    _store_blocks(v_ref, t[:, (nq + nk) * LANES:], nv, transpose=v_transposed)


def _proj1_kernel(x_ref, mod_ref, g_ref, w_ref, qn_ref, kn_ref, cq_ref, sq_ref, ck_ref, sk_ref,
                  q_ref, k_ref, v_ref, *, nq, nk, nv):
    u = _modnorm(x_ref[...], g_ref[2:3, :], mod_ref, 3).astype(BF16)
    cq, sq, ck, sk = cq_ref[...], sq_ref[...], ck_ref[...], sk_ref[...]
    qn, kn = qn_ref[...], kn_ref[...]
    t = jnp.dot(u, w_ref[...], preferred_element_type=F32)
    _store_blocks(q_ref, t, nq, lambda b: _rope(_rms(b, qn), cq, sq))
    _store_blocks(k_ref, t[:, nq * LANES:], nk, lambda b: _rope(_rms(b, kn), ck, sk))
    _store_blocks(v_ref, t[:, (nq + nk) * LANES:], nv, transpose=True)


def _proj2_kernel(x_ref, mod_ref, g_ref, w1_ref, qn_ref, kvn_ref, wuq_ref, wuk_ref, wuv_ref,
                  cq_ref, sq_ref, ck_ref, sk_ref, q_ref, k_ref, v_ref):
    u = _modnorm(x_ref[...], g_ref[2:3, :], mod_ref, 3).astype(BF16)
    cq, sq, ck, sk = cq_ref[...], sq_ref[...], ck_ref[...], sk_ref[...]
    t = jnp.dot(u, w1_ref[...], preferred_element_type=F32)
    cqv = _rms(t[:, :MLA_Q_RANK], qn_ref[...]).astype(BF16)
    ckv = _rms(t[:, MLA_Q_RANK:MLA_Q_RANK + MLA_KV_RANK], kvn_ref[...]).astype(BF16)
    kr = _rope(t[:, MLA_Q_RANK + MLA_KV_RANK:], ck, sk)
    qa = jnp.dot(cqv, wuq_ref[...], preferred_element_type=F32)
    _store_blocks(q_ref, qa, MLA_HEADS, lambda b: _rope(b, cq, sq))
    ka = jnp.dot(ckv, wuk_ref[...], preferred_element_type=F32)
    _store_blocks(k_ref, ka, MLA_HEADS, lambda b: b + kr)
    va = jnp.dot(ckv, wuv_ref[...], preferred_element_type=F32)
    _store_blocks(v_ref, va, MLA_HEADS // 2, transpose=True)


def _proj_call(geom, kern, h, mod, g8, consts, tables, nq, nk, nv, name, v_transposed=False):
    d = D_MODEL
    tile = lambda t: (t, 0)
    rope_spec = pl.BlockSpec((ROW_TILE, LANES), lambda t: (geom.rope_blk(t), 0))
    in_specs = ([pl.BlockSpec((ROW_TILE, d), tile),
                 pl.BlockSpec((None, 1, N_MOD * d), lambda t: (geom.mod_row(t), 0, 0)),
                 _const_spec(g8.shape)]
                + [_const_spec(c.shape) for c in consts]
                + [rope_spec] * 4)
    blk = lambda n: pl.BlockSpec((n, ROW_TILE, LANES), lambda t: (0, t, 0))
    shp = lambda n: jax.ShapeDtypeStruct((n, geom.n_rows, LANES), BF16)
    if v_transposed:
        v_blk = pl.BlockSpec((nv, LANES, ROW_TILE), lambda t: (0, 0, t))
        v_shp = jax.ShapeDtypeStruct((nv, LANES, geom.n_rows), BF16)
    else:
        v_blk, v_shp = blk(nv), shp(nv)
    return pl.pallas_call(
        kern,
        grid=(geom.n_tiles,),
        in_specs=in_specs,
        out_specs=[blk(nq), blk(nk), v_blk],
        out_shape=[shp(nq), shp(nk), v_shp],
        compiler_params=_cparams(1),
        name=name,
    )(h, mod, g8, *consts, *tables)


def _qk(qs, k):
    return lax.dot_general(qs, k, (((1,), (1,)), ((), ())), preferred_element_type=F32)


def _pv_with_sums(p, v):
    v_ext = jnp.concatenate([v, jnp.ones_like(v)], axis=1)
    return jnp.dot(p, v_ext, preferred_element_type=F32)


def _flash_stream(jobs):
    steps = [(j, c) for j, (_, chunks, _) in enumerate(jobs) for c in range(len(chunks))]
    qs_of = {}

    def scores(j, c):
        if j not in qs_of:
            qs_of[j] = jobs[j][0]()
        k_ref, hk, _, _, start, size = jobs[j][1][c]
        return _qk(qs_of[j], k_ref[hk, pl.ds(start, size), :])

    m = acc = None
    s_next = scores(*steps[0])
    for idx, (j, c) in enumerate(steps):
        s = s_next
        if idx + 1 < len(steps):
            s_next = scores(*steps[idx + 1])
        _, _, v_ref, hv, start, size = jobs[j][1][c]
        mc = jnp.max(s, axis=1, keepdims=True)
        m_new = mc if c == 0 else jnp.maximum(m, mc)
        p = jnp.exp2(s - m_new).astype(BF16)
        pv = _pv_with_sums(p, v_ref[hv, pl.ds(start, size), :])
        acc = pv if c == 0 else jnp.exp2(m - m_new) * acc + pv
        m = m_new
        if c == len(jobs[j][1]) - 1:
            qs_of.pop(j)
            jobs[j][2](acc[:, :LANES], acc[:, LANES:])


SUM_ROWS = 16


def _flash_stream_kmajor(jobs, dv):
    steps = [(j, c) for j, job in enumerate(jobs) for c in range(len(job[1]))]
    qs_of = {}

    def scores(j, c):
        if j not in qs_of:
            qs_of[j] = jobs[j][0]()
        k_ref, hk, _, _, start, size = jobs[j][1][c]
        return _qk(k_ref[hk, pl.ds(start, size), :], qs_of[j])

    m = acc = None
    s_next = scores(*steps[0])
    for idx, (j, c) in enumerate(steps):
        s = s_next
        if idx + 1 < len(steps):
            s_next = scores(*steps[idx + 1])
        _, _, v_ref, hv, start, size = jobs[j][1][c]
        mc = jnp.max(s, axis=0, keepdims=True)
        m_new = mc if c == 0 else jnp.maximum(m, mc)
        p = jnp.exp2(s - m_new).astype(BF16)
        v_row0 = jobs[j][3] if len(jobs[j]) > 3 else 0
        v_t = v_ref[hv, v_row0:v_row0 + dv, pl.ds(start, size)]
        v_ext = jnp.concatenate([v_t, jnp.ones((SUM_ROWS, size), BF16)], axis=0)
        pv = jnp.dot(v_ext, p, preferred_element_type=F32)
        acc = pv if c == 0 else jnp.exp2(m - m_new) * acc + pv
        m = m_new
        if c == len(jobs[j][1]) - 1:
            qs_of.pop(j)
            jobs[j][2](acc[:dv], acc[dv:dv + 1])


def _kv_chunks(kc_ref, vc_ref, kl_ref, vl_ref, hk, hv, n_ctx_keys, n_lat_keys, tk):
    chunks = [(kc_ref, hk, vc_ref, hv, 0, n_ctx_keys)]
    if kl_ref is not None:
        chunks += [(kl_ref, hk, vl_ref, hv, j * tk, tk) for j in range(n_lat_keys // tk)]
    return chunks


def _split_refs(refs, n_lead, has_lat):
    lead = refs[:n_lead]
    rest = refs[n_lead:]
    q_ref, kc_ref, vc_ref = rest[:3]
    if has_lat:
        kl_ref, vl_ref, o_ref = rest[3:6]
    else:
        kl_ref = vl_ref = None
        o_ref = rest[3]
    return lead, q_ref, kc_ref, vc_ref, kl_ref, vl_ref, o_ref


def _lane_lo(rows):
    lane = lax.broadcasted_iota(jnp.int32, (rows, LANES), 1)
    return (lane % HALF) < (HALF // 2)


def _attn_diff_kernel(*refs, has_lat, n_heads, n_ctx_keys, n_lat_keys, tk, lam_init):
    (lam_ref, subln_ref), q_ref, kc_ref, vc_ref, kl_ref, vl_ref, o_ref = _split_refs(refs, 2, has_lat)
    tq = q_ref.shape[1]
    lo = _lane_lo(tq)
    lf = lam_ref[...]
    lam = (jnp.exp(jnp.sum(lf[0:1] * lf[1:2], keepdims=True))
           - jnp.exp(jnp.sum(lf[2:3] * lf[3:4], keepdims=True)) + lam_init)
    subln = subln_ref[...]

    def job(h):
        def make_qs():
            q = q_ref[h]
            zero = jnp.zeros_like(q)
            return jnp.concatenate([jnp.where(lo, q, zero), jnp.where(lo, zero, q)], axis=0)

        def done(o_t, l):
            o_t = o_t / l
            o = (o_t[:, :tq] - lam * o_t[:, tq:]).T
            o_ref[h] = (_rms(o, subln) * (1.0 - lam_init)).astype(BF16)

        return make_qs, _kv_chunks(kc_ref, vc_ref, kl_ref, vl_ref, h, h, n_ctx_keys, n_lat_keys, tk), done

    def head_pair(i, carry):
        _flash_stream_kmajor([job(2 * i), job(2 * i + 1)], LANES)
        return carry

    lax.fori_loop(0, n_heads // 2, head_pair, 0)


def _attn_gqa_kernel(*refs, has_lat, grp, n_ctx_keys, n_lat_keys, tk):
    _, q_ref, kc_ref, vc_ref, kl_ref, vl_ref, o_ref = _split_refs(refs, 0, has_lat)
    tq = q_ref.shape[1]

    def done(o_t, l):
        o = (o_t / l).T
        for j in range(grp):
            o_ref[j] = o[j * tq:(j + 1) * tq].astype(BF16)

    _flash_stream_kmajor([(lambda: jnp.concatenate([q_ref[j] for j in range(grp)], axis=0),
                           _kv_chunks(kc_ref, vc_ref, kl_ref, vl_ref, 0, 0, n_ctx_keys, n_lat_keys, tk), done)],
                         LANES)


def _attn_mla_kernel(*refs, has_lat, n_pairs, n_ctx_keys, n_lat_keys, tk):
    _, q_ref, kc_ref, vc_ref, kl_ref, vl_ref, o_ref = _split_refs(refs, 0, has_lat)

    def pair(p, carry):
        even_out = []

        def job(e):
            h = 2 * p + e

            def done(o_t, l):
                if e == 0:
                    even_out.append(o_t / l)
                else:
                    o_ref[p] = jnp.concatenate([even_out[0], o_t / l], axis=0).T.astype(BF16)

            return ((lambda: q_ref[h]),
                    _kv_chunks(kc_ref, vc_ref, kl_ref, vl_ref, h, p, n_ctx_keys, n_lat_keys, tk), done, e * HALF)

        _flash_stream_kmajor([job(0), job(1)], HALF)
        return carry

    lax.fori_loop(0, n_pairs, pair, 0)


def _attn_swa_kernel(sink_ref, q_ref, kc_ref, vc_ref, kl_ref, vl_ref, o_ref, *, n_pairs, pairs_per_stage, seq, win):
    tq = q_ref.shape[1]
    i = pl.program_id(2)
    start = pl.multiple_of(jnp.clip(i * tq - WINDOW, 0, seq - win), WINDOW)
    lo = _lane_lo(tq)
    lane = lax.broadcasted_iota(jnp.int32, (tq, LANES), 1)
    qpos = i * tq + lax.broadcasted_iota(jnp.int32, (tq, win), 0)
    kpos = start + lax.broadcasted_iota(jnp.int32, (tq, win), 1)
    bias = jnp.where(jnp.abs(qpos - kpos) <= WINDOW, 0.0, NEG_INF)

    zero = jnp.zeros((tq, LANES), BF16)
    n_stage = n_pairs // pairs_per_stage
    heads = 2 * pairs_per_stage
    bias_rows = jnp.concatenate([bias] * heads, axis=0)

    def scores(t):
        q_rows = []
        for j in range(t * pairs_per_stage, (t + 1) * pairs_per_stage):
            q = q_ref[j]
            q_rows += [jnp.where(lo, q, zero), jnp.where(lo, zero, q)]
        qs = jnp.concatenate(q_rows, axis=0)
        return _qk(qs, kc_ref[0]), _qk(qs, kl_ref[0, pl.ds(start, win), :]) + bias_rows

    def softmax(t, s_ctx, s_band):
        sink = jnp.concatenate([jnp.broadcast_to(sink_ref[h:h + 1, 0:1], (tq, 1))
                                for h in range(t * heads, (t + 1) * heads)], axis=0)
        m = jnp.maximum(jnp.maximum(jnp.max(s_ctx, axis=1, keepdims=True),
                                    jnp.max(s_band, axis=1, keepdims=True)), sink)
        return jnp.exp2(s_ctx - m).astype(BF16), jnp.exp2(s_band - m).astype(BF16), jnp.exp2(sink - m)

    def finish(t, p_ctx, p_band, p_sink):
        acc = (_pv_with_sums(p_ctx, vc_ref[0]) + _pv_with_sums(p_band, vl_ref[0, pl.ds(start, win), :]))
        o = acc[:, :LANES] / (acc[:, LANES:] + p_sink)
        for j in range(pairs_per_stage):
            o_ref[t * pairs_per_stage + j] = jnp.where(
                lane < HALF, o[2 * j * tq:(2 * j + 1) * tq], o[(2 * j + 1) * tq:(2 * j + 2) * tq]).astype(BF16)

    s_vals, p_vals = {}, {}
    for t in range(n_stage + 2):
        if t < n_stage:
            s_vals[t] = scores(t)
        if 0 <= t - 1 < n_stage:
            p_vals[t - 1] = softmax(t - 1, *s_vals.pop(t - 1))
        if 0 <= t - 2 < n_stage:
            finish(t - 2, *p_vals.pop(t - 2))


def _attn_call(geom, kern, lead, lead_specs, q, k, v, *, has_lat, tq, q_per, k_per, v_per, o_per, n_groups, name,
               v_transposed=False):
    b, s, c = geom.batch, geom.seq, geom.ctx
    ctx_blk0 = geom.n_lat_rows // c
    if has_lat:
        n_q = s // tq
        q_map = lambda bi, g, i: (g, bi * n_q + i, 0)
        o_rows = geom.n_lat_rows
    else:
        assert tq == c
        n_q = 1
        q_map = lambda bi, g, i: (g, ctx_blk0 + bi, 0)
        o_rows = b * c
    o_map = lambda bi, g, i: (g, bi * n_q + i, 0)
    ctx_map = lambda bi, g, i: (g, ctx_blk0 + bi, 0)
    lat_map = lambda bi, g, i: (g, bi, 0)
    if v_transposed:
        v_ctx_spec = pl.BlockSpec((v_per, LANES, c), lambda bi, g, i: (g, 0, ctx_blk0 + bi))
        v_lat_spec = pl.BlockSpec((v_per, LANES, s), lambda bi, g, i: (g, 0, bi))
    else:
        v_ctx_spec = pl.BlockSpec((v_per, c, LANES), ctx_map)
        v_lat_spec = pl.BlockSpec((v_per, s, LANES), lat_map)
    in_specs = list(lead_specs) + [pl.BlockSpec((q_per, tq, LANES), q_map),
                                   pl.BlockSpec((k_per, c, LANES), ctx_map), v_ctx_spec]
    args = list(lead) + [q, k, v]
    if has_lat:
        in_specs += [pl.BlockSpec((k_per, s, LANES), lat_map), v_lat_spec]
        args += [k, v]
    return pl.pallas_call(
        kern,
        grid=(b, n_groups, n_q),
        in_specs=in_specs,
        out_specs=pl.BlockSpec((o_per, tq, LANES), o_map),
        out_shape=jax.ShapeDtypeStruct((o_per * n_groups, o_rows, LANES), BF16),
        compiler_params=_cparams(3),
        name=name,
    )(*args)


def _rope_tables(seq, rot_dim, aidx, active, scale):
    scale = scale * LOG2E
    pos = np.arange(seq)
    row = jnp.asarray(pos // GRID_W, F32)
    col = jnp.asarray(pos % GRID_W, F32)
    n_axis = rot_dim // 4
    freqs = ROPE_THETA ** (-jnp.arange(n_axis, dtype=F32) / n_axis)
    ang = jnp.concatenate([row[:, None] * freqs, col[:, None] * freqs], axis=-1)
    a = ang[:, np.asarray(aidx)]
    act = jnp.asarray(active)[None, :]
    sign = jnp.asarray(np.where(np.arange(LANES) < HALF, -1.0, 1.0), F32)[None, :]
    c = jnp.where(act, jnp.cos(a), 1.0)
    sn = jnp.where(act, jnp.sin(a) * sign, 0.0)
    c = jnp.concatenate([c, jnp.ones((ROW_TILE, LANES), F32)], axis=0)
    sn = jnp.concatenate([sn, jnp.zeros((ROW_TILE, LANES), F32)], axis=0)
    return (c * scale, sn * scale), (c, sn)


def _take_cols(w, idx):
    idx = np.asarray(idx)
    out = jnp.take(w, jnp.asarray(np.maximum(idx, 0)), axis=1)
    if (idx < 0).any():
        out = jnp.where(jnp.asarray(idx >= 0)[None, :], out, 0.0)
    return out


def _eo(d):
    return np.concatenate([np.arange(0, d, 2), np.arange(1, d, 2)])


def _pair_layout():
    ev, od = np.arange(0, 64, 2), np.arange(1, 64, 2)
    which = np.concatenate([np.zeros(32), np.ones(32), np.zeros(32), np.ones(32)]).astype(np.int64)
    dim = np.concatenate([ev, ev, od, od])
    return which, dim


def kernel(x, c, ctx, c_ctx, ada_w, ada_b, norm_g, ffn_w_in, ffn_w_out, da_w_in, da_lambda, da_subln, da_w_out,
           ga_w_in, ga_q_norm, ga_k_norm, ga_w_out, mla_w_in, mla_q_norm, mla_kv_norm, mla_w_uq, mla_w_ukv,
           mla_w_out, swa_w_in, swa_sink, swa_w_out):
    batch, seq, d = x.shape
    n_ctx = ctx.shape[1]
    depth = ada_w.shape[0]
    assert d == D_MODEL and seq % ROW_TILE == 0 and (batch * n_ctx) % ROW_TILE == 0
    assert n_ctx % 256 == 0 and seq % 512 == 0 and batch < MOD_ROWS and depth == 4
    geom = _Geom(batch, seq, n_ctx)

    cond = jnp.zeros((MOD_ROWS, d), F32).at[:batch].set(c).at[batch].set(c_ctx)
    mods = _ada_call(cond, ada_w, ada_b).reshape(depth, MOD_ROWS, 1, N_MOD * d)

    h = (x.reshape(batch * seq, d), ctx.reshape(batch * n_ctx, d))
    g8 = jnp.concatenate([norm_g, jnp.ones((depth, 2, d), F32)], axis=1)
    w_in_b = ffn_w_in.astype(BF16)
    w_out_b = ffn_w_out.astype(BF16)

    which, pdim = _pair_layout()
    tk = 512

    for i in range(depth):
        mod = mods[i]
        last = i == depth - 1
        h = _ffn_call(geom, h, mod, g8[i], w_in_b[i, 0], w_out_b[i, 0], k=0, gi=0, name=f"ffn_a{i}")

        if i == 0:
            w = da_w_in[0]
            hq = DA_HEADS * 2 * DA_QK
            qcols = np.concatenate([hd * 128 + which * 64 + pdim for hd in range(DA_HEADS)])
            wcat = jnp.concatenate([_take_cols(w, qcols), _take_cols(w, hq + qcols), w[:, 2 * hq:]], axis=1)
            tq_tab, tk_tab = _rope_tables(seq, DA_QK, np.arange(LANES) % 32, np.ones(LANES, bool), DA_QK ** -0.5)
            q, k, v = _proj_call(geom, functools.partial(_proj0_kernel, nq=8, nk=8, nv=8, v_transposed=True), h, mod,
                                 g8[i], [wcat.astype(BF16)], [*tq_tab, *tk_tab], 8, 8, 8, name="proj_diff",
                                 v_transposed=True)
            lam_init = 0.8 - 0.6 * math.exp(-0.3 * i)
            lead = [da_lambda[0], da_subln[0].reshape(1, LANES)]
            lead_specs = [pl.BlockSpec((4, DA_QK), lambda bi, g, qi: (0, 0)),
                          pl.BlockSpec((1, LANES), lambda bi, g, qi: (0, 0))]
            hs = 4
            mk = lambda has_lat: functools.partial(_attn_diff_kernel, has_lat=has_lat, n_heads=hs, n_ctx_keys=n_ctx,
                                                   n_lat_keys=seq, tk=tk, lam_init=lam_init)
            o_lat = _attn_call(geom, mk(True), lead, lead_specs, q, k, v, has_lat=True, tq=512, q_per=hs, k_per=hs,
                               v_per=hs, o_per=hs, n_groups=DA_HEADS // hs, name="attn_diff", v_transposed=True)
            o_ctx = _attn_call(geom, mk(False), lead, lead_specs, q, k, v, has_lat=False, tq=n_ctx, q_per=hs,
                               k_per=hs, v_per=hs, o_per=hs, n_groups=DA_HEADS // hs, name="attn_diff_ctx",
                               v_transposed=True)
            w_mo = da_w_out[0]
        elif i == 1:
            w = ga_w_in[0]
            eo = _eo(GA_DIM)
            qcols = np.concatenate([hd * GA_DIM + eo for hd in range(GA_HEADS)])
            kcols = GA_HEADS * GA_DIM + np.concatenate([hd * GA_DIM + eo for hd in range(GA_KV)])
            wcat = jnp.concatenate([_take_cols(w, qcols), _take_cols(w, kcols), w[:, (GA_HEADS + GA_KV) * GA_DIM:]],
                                   axis=1)
            tq_tab, tk_tab = _rope_tables(seq, GA_DIM, np.arange(LANES) % 64, np.ones(LANES, bool), GA_DIM ** -0.5)
            consts = [wcat.astype(BF16), ga_q_norm[0][eo].reshape(1, LANES), ga_k_norm[0][eo].reshape(1, LANES)]
            q, k, v = _proj_call(geom, functools.partial(_proj1_kernel, nq=8, nk=2, nv=2), h, mod, g8[i],
                                 consts, [*tq_tab, *tk_tab], 8, 2, 2, name="proj_gqa", v_transposed=True)
            grp = GA_HEADS // GA_KV
            mk = lambda has_lat: functools.partial(_attn_gqa_kernel, has_lat=has_lat, grp=grp, n_ctx_keys=n_ctx,
                                                   n_lat_keys=seq, tk=tk)
            o_lat = _attn_call(geom, mk(True), [], [], q, k, v, has_lat=True, tq=256, q_per=grp, k_per=1, v_per=1,
                               o_per=grp, n_groups=GA_KV, name="attn_gqa", v_transposed=True)
            o_ctx = _attn_call(geom, mk(False), [], [], q, k, v, has_lat=False, tq=n_ctx, q_per=grp, k_per=1,
                               v_per=1, o_per=grp, n_groups=GA_KV, name="attn_gqa_ctx", v_transposed=True)
            w_mo = ga_w_out[0]
        elif i == 2:
            lanes = np.arange(LANES)
            is_rope = ((lanes % HALF) >= 32) & ((lanes % HALF) < 48)
            is_nope = (lanes % HALF) < 32
            rope_dim = 2 * ((lanes % HALF) - 32) + (lanes >= HALF)
            nope_dim = (lanes % HALF) + 32 * (lanes >= HALF)
            w = mla_w_in[0]
            kr_cols = np.where(is_rope, MLA_Q_RANK + MLA_KV_RANK + rope_dim, -1)
            w1 = jnp.concatenate([w[:, :MLA_Q_RANK + MLA_KV_RANK], _take_cols(w, kr_cols)], axis=1)
            qd = MLA_NOPE + MLA_ROPE
            q_blk = np.where(is_nope, nope_dim, np.where(is_rope, MLA_NOPE + rope_dim, -1))
            q_cols = np.concatenate([np.where(q_blk >= 0, hd * qd + q_blk, -1) for hd in range(MLA_HEADS)])
            kvd = MLA_NOPE + MLA_V
            k_blk = np.where(is_nope, nope_dim, -1)
            k_cols = np.concatenate([np.where(k_blk >= 0, hd * kvd + k_blk, -1) for hd in range(MLA_HEADS)])
            v_cols = np.concatenate([hd * kvd + MLA_NOPE + np.arange(MLA_V) for hd in range(MLA_HEADS)])
            consts = [w1.astype(BF16), mla_q_norm[0].reshape(1, -1), mla_kv_norm[0].reshape(1, -1),
                      _take_cols(mla_w_uq[0], q_cols).astype(BF16), _take_cols(mla_w_ukv[0], k_cols).astype(BF16),
                      _take_cols(mla_w_ukv[0], v_cols).astype(BF16)]
            aidx = np.where(is_rope, (lanes % HALF) - 32, 0)
            tq_tab, tk_tab = _rope_tables(seq, MLA_ROPE, aidx, is_rope, (MLA_NOPE + MLA_ROPE) ** -0.5)
            q, k, v = _proj_call(geom, _proj2_kernel, h, mod, g8[i], consts, [*tq_tab, *tk_tab],
                                 MLA_HEADS, MLA_HEADS, MLA_HEADS // 2, name="proj_mla", v_transposed=True)
            n_pairs = 4
            mk = lambda has_lat: functools.partial(_attn_mla_kernel, has_lat=has_lat, n_pairs=n_pairs,
                                                   n_ctx_keys=n_ctx, n_lat_keys=seq, tk=tk)
            o_lat = _attn_call(geom, mk(True), [], [], q, k, v, has_lat=True, tq=1024, q_per=2 * n_pairs,
                               k_per=2 * n_pairs, v_per=n_pairs, o_per=n_pairs,
                               n_groups=MLA_HEADS // (2 * n_pairs), name="attn_mla", v_transposed=True)
            o_ctx = _attn_call(geom, mk(False), [], [], q, k, v, has_lat=False, tq=n_ctx, q_per=2 * n_pairs,
                               k_per=2 * n_pairs, v_per=n_pairs, o_per=n_pairs,
                               n_groups=MLA_HEADS // (2 * n_pairs), name="attn_mla_ctx", v_transposed=True)
            w_mo = mla_w_out[0]
        else:
            w = swa_w_in[0]
            hq = SWA_HEADS * SWA_DIM
            qcols = np.concatenate([(2 * p + which) * SWA_DIM + pdim for p in range(SWA_HEADS // 2)])
            kcols = np.concatenate([hq + g * SWA_DIM + pdim for g in range(SWA_KV)])
            vcols = np.concatenate([hq + SWA_KV * SWA_DIM + g * SWA_DIM + (np.arange(LANES) % SWA_DIM)
                                    for g in range(SWA_KV)])
            wcat = jnp.concatenate([_take_cols(w, qcols), _take_cols(w, kcols), _take_cols(w, vcols)], axis=1)
            tq_tab, tk_tab = _rope_tables(seq, SWA_DIM, np.arange(LANES) % 32, np.ones(LANES, bool), SWA_DIM ** -0.5)
            q, k, v = _proj_call(geom, functools.partial(_proj0_kernel, nq=8, nk=2, nv=2, v_transposed=False), h, mod, g8[i],
                                 [wcat.astype(BF16)], [*tq_tab, *tk_tab], 8, 2, 2, name="proj_swa")
            hpg = SWA_HEADS // SWA_KV
            sink = jnp.broadcast_to((swa_sink[0].astype(F32) * LOG2E).reshape(SWA_KV, hpg, 1), (SWA_KV, hpg, LANES))
            tq = 256
            kern = functools.partial(_attn_swa_kernel, n_pairs=hpg // 2, pairs_per_stage=2, seq=seq,
                                     win=tq + 2 * WINDOW)
            o_lat = _attn_call(geom, kern, [sink], [pl.BlockSpec((None, hpg, LANES), lambda bi, g, qi: (g, 0, 0))],
                               q, k, v, has_lat=True, tq=tq, q_per=hpg // 2, k_per=1, v_per=1, o_per=hpg // 2,
                               n_groups=SWA_KV, name="attn_swa")
            o_ctx = None
            w_mo = swa_w_out[0]

        h = _ffn_call(geom, h, mod, g8[i], w_in_b[i, 1], w_out_b[i, 1], k=6, gi=4,
                      pre=(o_lat, o_ctx, w_mo.astype(BF16)), n_tiles=geom.n_lat if last else None,
                      name=f"ffn_b{i}")
    return h.reshape(batch, seq, d)
```

```python
import functools
import math

import numpy as np
import jax
import jax.numpy as jnp
from jax import lax
from jax.experimental import pallas as pl
from jax.experimental.pallas import tpu as pltpu

F32 = jnp.float32
BF16 = jnp.bfloat16

D_MODEL = 1024
D_FF = 2816
N_MOD = 9
NORM_EPS = 1e-6
GRID_W = 64
ROPE_THETA = 10000.0
NEG_INF = -1e30
WINDOW = 128
LOG2E = math.log2(math.e)

LANES = 128
HALF = LANES // 2
ROW_TILE = 512
FFN_HALF = 512
FFN_TILE = 2 * FFN_HALF
FF_CHUNK = 256
MOD_ROWS = 16
VMEM_LIMIT = 52 * 1024 * 1024
FFN_VMEM_LIMIT = 58 * 1024 * 1024

DA_HEADS, DA_QK = 8, 64
GA_HEADS, GA_KV, GA_DIM = 8, 2, 128
MLA_HEADS, MLA_Q_RANK, MLA_KV_RANK, MLA_NOPE, MLA_ROPE, MLA_V = 16, 256, 128, 64, 32, 64
SWA_HEADS, SWA_KV, SWA_DIM = 16, 2, 64


def _cparams(n_axes, vmem_limit=VMEM_LIMIT):
    return pltpu.CompilerParams(dimension_semantics=("arbitrary",) * n_axes,
                                vmem_limit_bytes=vmem_limit)


def _rms(x, g):
    return x * lax.rsqrt(jnp.mean(x * x, axis=-1, keepdims=True) + NORM_EPS) * g


def _mod(mod_ref, k):
    return mod_ref[:, k * D_MODEL:(k + 1) * D_MODEL]


def _modnorm(x, g, mod_ref, k):
    return _rms(x, g) * (1.0 + _mod(mod_ref, k + 1)) + _mod(mod_ref, k)


def _rope(x, c, s):
    return x * c + pltpu.roll(x, HALF, 1) * s


def _ada_kernel(c_ref, w_ref, b_ref, o_ref):
    c = c_ref[...]
    sc = c / (1.0 + jnp.exp(-c))
    o_ref[...] = jnp.dot(sc, w_ref[...], precision=lax.Precision.HIGHEST,
                         preferred_element_type=F32) + b_ref[...]


def _ada_call(cond, ada_w, ada_b):
    depth, d, n = ada_w.shape
    cw = 1024
    return pl.pallas_call(
        _ada_kernel,
        grid=(depth, n // cw),
        in_specs=[pl.BlockSpec((MOD_ROWS, d), lambda l, j: (0, 0)),
                  pl.BlockSpec((None, d, cw), lambda l, j: (l, 0, j)),
                  pl.BlockSpec((None, 1, cw), lambda l, j: (l, 0, j))],
        out_specs=pl.BlockSpec((None, MOD_ROWS, cw), lambda l, j: (l, 0, j)),
        out_shape=jax.ShapeDtypeStruct((depth, MOD_ROWS, n), F32),
        compiler_params=_cparams(2),
        name="adaln",
    )(cond, ada_w, ada_b.reshape(depth, 1, n))


class _Geom:
    def __init__(self, batch, seq, ctx, tile):
        self.batch, self.seq, self.ctx, self.tile = batch, seq, ctx, tile
        self.n_lat_rows = batch * seq
        self.n_rows = batch * (seq + ctx)
        self.n_lat = self.n_lat_rows // tile
        self.n_tiles = self.n_rows // tile
        self.tiles_per_batch = seq // tile

    def mod_row(self, t):
        return jnp.where(t < self.n_lat, t // self.tiles_per_batch, self.batch)

    def rope_blk(self, t):
        return jnp.where(t < self.n_lat, t % self.tiles_per_batch, self.tiles_per_batch)


def _const_spec(shape):
    nd = len(shape)
    return pl.BlockSpec(shape, lambda *_: (0,) * nd, pipeline_mode=pl.Buffered(1))


def _ffn_kernel(*refs, k, gi, n_lat, split_in, has_pre, has_ctx):
    it = iter(refs)
    x_ref = next(it)
    xc_ref = next(it) if split_in else None
    mod_ref, g_ref, win_ref, wout_ref = (next(it) for _ in range(4))
    if has_pre:
        ol_ref = next(it)
        oc_ref = next(it) if has_ctx else None
        wmo_ref = next(it)
    out_ref = next(it)
    is_lat = pl.program_id(0) < n_lat

    def prologue(r0):
        rows = slice(r0, r0 + FFN_HALF)
        x = x_ref[rows, :]
        if split_in:
            x = jnp.where(is_lat, x, xc_ref[rows, :])
        if has_pre:
            nblk = ol_ref.shape[0]
            o = jnp.concatenate([ol_ref[i, rows, :] for i in range(nblk)], axis=1)
            if has_ctx:
                o = jnp.where(is_lat, o, jnp.concatenate([oc_ref[i, rows, :] for i in range(nblk)], axis=1))
            y = jnp.dot(o, wmo_ref[...], preferred_element_type=F32)
            x = x + _mod(mod_ref, 5) * _rms(y, g_ref[3:4, :])
        return x, _modnorm(x, g_ref[gi:gi + 1, :], mod_ref, k).astype(BF16)

    def gate_up(u, c):
        lo = c * FF_CHUNK
        return (jnp.dot(u, win_ref[:, lo:lo + FF_CHUNK], preferred_element_type=F32),
                jnp.dot(u, win_ref[:, D_FF + lo:D_FF + lo + FF_CHUNK], preferred_element_type=F32))

    n_chunks = D_FF // FF_CHUNK
    steps = [(blk, c) for blk in range(FFN_TILE // FFN_HALF) for c in range(n_chunks)]
    xs, us = {}, {}
    xs[0], us[0] = prologue(0)
    nxt = gate_up(us[0], 0)
    xs[1], us[1] = prologue(FFN_HALF)
    acc = None
    for idx, (blk, c) in enumerate(steps):
        a, b = nxt
        if idx + 1 < len(steps):
            nxt = gate_up(us[steps[idx + 1][0]], steps[idx + 1][1])
        act = (a / (1.0 + jnp.exp(-a)) * b).astype(BF16)
        part = jnp.dot(act, wout_ref[c * FF_CHUNK:(c + 1) * FF_CHUNK, :], preferred_element_type=F32)
        acc = part if c == 0 else acc + part
        if c == n_chunks - 1:
            r0 = blk * FFN_HALF
            out_ref[r0:r0 + FFN_HALF, :] = (xs.pop(blk)
                                            + 0.5 * _mod(mod_ref, k + 2) * _rms(acc, g_ref[gi + 1:gi + 2, :]))


def _ffn_call(geom, h, mod, g8, w_in, w_out, *, k, gi, pre=None, n_tiles=None, name):
    n_tiles = geom.n_tiles if n_tiles is None else n_tiles
    d = D_MODEL
    split_in = isinstance(h, tuple)
    if split_in:
        in_specs = [pl.BlockSpec((FFN_TILE, d), lambda t: (jnp.minimum(t, geom.n_lat - 1), 0)),
                    pl.BlockSpec((FFN_TILE, d), lambda t: (jnp.maximum(t - geom.n_lat, 0), 0))]
        args = list(h)
    else:
        in_specs = [pl.BlockSpec((FFN_TILE, d), lambda t: (t, 0))]
        args = [h]
    in_specs += [pl.BlockSpec((None, 1, N_MOD * d), lambda t: (geom.mod_row(t), 0, 0)),
                 _const_spec(g8.shape), _const_spec(w_in.shape), _const_spec(w_out.shape)]
    args += [mod, g8, w_in, w_out]
    has_pre = pre is not None
    has_ctx = False
    if has_pre:
        o_lat, o_ctx, w_mo = pre
        nblk = o_lat.shape[0]
        in_specs.append(pl.BlockSpec((nblk, FFN_TILE, LANES),
                                     lambda t: (0, jnp.minimum(t, geom.n_lat - 1), 0)))
        args.append(o_lat)
        has_ctx = o_ctx is not None
        if has_ctx:
            in_specs.append(pl.BlockSpec((nblk, FFN_TILE, LANES),
                                         lambda t: (0, jnp.maximum(t - geom.n_lat, 0), 0)))
            args.append(o_ctx)
        in_specs.append(_const_spec(w_mo.shape))
        args.append(w_mo)
    kern = functools.partial(_ffn_kernel, k=k, gi=gi, n_lat=geom.n_lat, split_in=split_in, has_pre=has_pre,
                             has_ctx=has_ctx)
    return pl.pallas_call(
        kern,
        grid=(n_tiles,),
        in_specs=in_specs,
        out_specs=pl.BlockSpec((FFN_TILE, d), lambda t: (t, 0)),
        out_shape=jax.ShapeDtypeStruct((n_tiles * FFN_TILE, d), F32),
        compiler_params=_cparams(1, FFN_VMEM_LIMIT),
        name=name,
    )(*args)


def _store_blocks(dst_ref, x, n, fn=None):
    for i in range(n):
        blk = x[:, i * LANES:(i + 1) * LANES]
        if fn is not None:
            blk = fn(blk)
        dst_ref[i] = blk.astype(BF16)


def _proj0_kernel(x_ref, mod_ref, g_ref, w_ref, cq_ref, sq_ref, ck_ref, sk_ref, q_ref, k_ref, v_ref, *, nq, nk, nv):
    u = _modnorm(x_ref[...], g_ref[2:3, :], mod_ref, 3).astype(BF16)
    cq, sq, ck, sk = cq_ref[...], sq_ref[...], ck_ref[...], sk_ref[...]
    t = jnp.dot(u, w_ref[...], preferred_element_type=F32)
    _store_blocks(q_ref, t, nq, lambda b: _rope(b, cq, sq))
    _store_blocks(k_ref, t[:, nq * LANES:], nk, lambda b: _rope(b, ck, sk))
    _store_blocks(v_ref, t[:, (nq + nk) * LANES:], nv)


def _proj1_kernel(x_ref, mod_ref, g_ref, w_ref, qn_ref, kn_ref, cq_ref, sq_ref, ck_ref, sk_ref,
                  q_ref, k_ref, v_ref, *, nq, nk, nv):
    u = _modnorm(x_ref[...], g_ref[2:3, :], mod_ref, 3).astype(BF16)
    cq, sq, ck, sk = cq_ref[...], sq_ref[...], ck_ref[...], sk_ref[...]
    qn, kn = qn_ref[...], kn_ref[...]
    t = jnp.dot(u, w_ref[...], preferred_element_type=F32)
    _store_blocks(q_ref, t, nq, lambda b: _rope(_rms(b, qn), cq, sq))
    _store_blocks(k_ref, t[:, nq * LANES:], nk, lambda b: _rope(_rms(b, kn), ck, sk))
    _store_blocks(v_ref, t[:, (nq + nk) * LANES:], nv)


def _proj2_kernel(x_ref, mod_ref, g_ref, w1_ref, qn_ref, kvn_ref, wuq_ref, wuk_ref, wuv_ref,
                  cq_ref, sq_ref, ck_ref, sk_ref, q_ref, k_ref, v_ref):
    u = _modnorm(x_ref[...], g_ref[2:3, :], mod_ref, 3).astype(BF16)
    cq, sq, ck, sk = cq_ref[...], sq_ref[...], ck_ref[...], sk_ref[...]
    t = jnp.dot(u, w1_ref[...], preferred_element_type=F32)
    cqv = _rms(t[:, :MLA_Q_RANK], qn_ref[...]).astype(BF16)
    ckv = _rms(t[:, MLA_Q_RANK:MLA_Q_RANK + MLA_KV_RANK], kvn_ref[...]).astype(BF16)
    kr = _rope(t[:, MLA_Q_RANK + MLA_KV_RANK:], ck, sk)
    qa = jnp.dot(cqv, wuq_ref[...], preferred_element_type=F32)
    _store_blocks(q_ref, qa, MLA_HEADS, lambda b: _rope(b, cq, sq))
    ka = jnp.dot(ckv, wuk_ref[...], preferred_element_type=F32)
    _store_blocks(k_ref, ka, MLA_HEADS, lambda b: b + kr)
    va = jnp.dot(ckv, wuv_ref[...], preferred_element_type=F32)
    _store_blocks(v_ref, va, MLA_HEADS // 2)


def _proj_call(geom, kern, h, mod, g8, consts, tables, nq, nk, nv, name):
    d = D_MODEL
    tile = lambda t: (t, 0)
    rope_spec = pl.BlockSpec((ROW_TILE, LANES), lambda t: (geom.rope_blk(t), 0))
    in_specs = ([pl.BlockSpec((ROW_TILE, d), tile),
                 pl.BlockSpec((None, 1, N_MOD * d), lambda t: (geom.mod_row(t), 0, 0)),
                 _const_spec(g8.shape)]
                + [_const_spec(c.shape) for c in consts]
                + [rope_spec] * 4)
    blk = lambda n: pl.BlockSpec((n, ROW_TILE, LANES), lambda t: (0, t, 0))
    shp = lambda n: jax.ShapeDtypeStruct((n, geom.n_rows, LANES), BF16)
    return pl.pallas_call(
        kern,
        grid=(geom.n_tiles,),
        in_specs=in_specs,
        out_specs=[blk(nq), blk(nk), blk(nv)],
        out_shape=[shp(nq), shp(nk), shp(nv)],
        compiler_params=_cparams(1),
        name=name,
    )(h, mod, g8, *consts, *tables)


def _qk(qs, k):
    return lax.dot_general(qs, k, (((1,), (1,)), ((), ())), preferred_element_type=F32)


def _pv_with_sums(p, v):
    v_ext = jnp.concatenate([v, jnp.ones_like(v)], axis=1)
    return jnp.dot(p, v_ext, preferred_element_type=F32)


def _flash_stream(jobs):
    steps = [(j, c) for j, (_, chunks, _) in enumerate(jobs) for c in range(len(chunks))]
    qs_of = {}

    def scores(j, c):
        if j not in qs_of:
            qs_of[j] = jobs[j][0]()
        k_ref, hk, _, _, start, size = jobs[j][1][c]
        return _qk(qs_of[j], k_ref[hk, pl.ds(start, size), :])

    m = acc = None
    s_next = scores(*steps[0])
    for idx, (j, c) in enumerate(steps):
        s = s_next
        if idx + 1 < len(steps):
            s_next = scores(*steps[idx + 1])
        _, _, v_ref, hv, start, size = jobs[j][1][c]
        mc = jnp.max(s, axis=1, keepdims=True)
        m_new = mc if c == 0 else jnp.maximum(m, mc)
        p = jnp.exp2(s - m_new).astype(BF16)
        pv = _pv_with_sums(p, v_ref[hv, pl.ds(start, size), :])
        acc = pv if c == 0 else jnp.exp2(m - m_new) * acc + pv
        m = m_new
        if c == len(jobs[j][1]) - 1:
            qs_of.pop(j)
            jobs[j][2](acc[:, :LANES], acc[:, LANES:])


def _kv_chunks(kc_ref, vc_ref, kl_ref, vl_ref, hk, hv, n_ctx_keys, n_lat_keys, tk):
    chunks = [(kc_ref, hk, vc_ref, hv, 0, n_ctx_keys)]
    if kl_ref is not None:
        chunks += [(kl_ref, hk, vl_ref, hv, j * tk, tk) for j in range(n_lat_keys // tk)]
    return chunks


def _split_refs(refs, n_lead, has_lat):
    lead = refs[:n_lead]
    rest = refs[n_lead:]
    q_ref, kc_ref, vc_ref = rest[:3]
    if has_lat:
        kl_ref, vl_ref, o_ref = rest[3:6]
    else:
        kl_ref = vl_ref = None
        o_ref = rest[3]
    return lead, q_ref, kc_ref, vc_ref, kl_ref, vl_ref, o_ref


def _lane_lo(rows):
    lane = lax.broadcasted_iota(jnp.int32, (rows, LANES), 1)
    return (lane % HALF) < (HALF // 2)


def _attn_diff_kernel(*refs, has_lat, n_heads, n_ctx_keys, n_lat_keys, tk, lam_init):
    (lam_ref, subln_ref), q_ref, kc_ref, vc_ref, kl_ref, vl_ref, o_ref = _split_refs(refs, 2, has_lat)
    tq = q_ref.shape[1]
    lo = _lane_lo(tq)
    lf = lam_ref[...]
    lam = (jnp.exp(jnp.sum(lf[0:1] * lf[1:2], keepdims=True))
           - jnp.exp(jnp.sum(lf[2:3] * lf[3:4], keepdims=True)) + lam_init)
    subln = subln_ref[...]

    def job(h):
        def make_qs():
            q = q_ref[h]
            zero = jnp.zeros_like(q)
            return jnp.concatenate([jnp.where(lo, q, zero), jnp.where(lo, zero, q)], axis=0)

        def done(acc, l):
            o = acc / l
            o = o[:tq] - lam * o[tq:]
            o_ref[h] = (_rms(o, subln) * (1.0 - lam_init)).astype(BF16)

        return make_qs, _kv_chunks(kc_ref, vc_ref, kl_ref, vl_ref, h, h, n_ctx_keys, n_lat_keys, tk), done

    def head_pair(i, carry):
        _flash_stream([job(2 * i), job(2 * i + 1)])
        return carry

    lax.fori_loop(0, n_heads // 2, head_pair, 0)


def _attn_gqa_kernel(*refs, has_lat, grp, n_ctx_keys, n_lat_keys, tk):
    _, q_ref, kc_ref, vc_ref, kl_ref, vl_ref, o_ref = _split_refs(refs, 0, has_lat)
    tq = q_ref.shape[1]

    def done(acc, l):
        o = acc / l
        for j in range(grp):
            o_ref[j] = o[j * tq:(j + 1) * tq].astype(BF16)

    _flash_stream([(lambda: jnp.concatenate([q_ref[j] for j in range(grp)], axis=0),
                    _kv_chunks(kc_ref, vc_ref, kl_ref, vl_ref, 0, 0, n_ctx_keys, n_lat_keys, tk), done)])


def _attn_mla_kernel(*refs, has_lat, n_pairs, n_ctx_keys, n_lat_keys, tk):
    _, q_ref, kc_ref, vc_ref, kl_ref, vl_ref, o_ref = _split_refs(refs, 0, has_lat)
    tq = q_ref.shape[1]
    lane = lax.broadcasted_iota(jnp.int32, (tq, LANES), 1)

    def pair(p, carry):
        even_out = []

        def job(e):
            h = 2 * p + e

            def done(acc, l):
                if e == 0:
                    even_out.append(acc / l)
                else:
                    o_ref[p] = jnp.where(lane < HALF, even_out[0], acc / l).astype(BF16)

            return ((lambda: q_ref[h]),
                    _kv_chunks(kc_ref, vc_ref, kl_ref, vl_ref, h, p, n_ctx_keys, n_lat_keys, tk), done)

        _flash_stream([job(0), job(1)])
        return carry

    lax.fori_loop(0, n_pairs, pair, 0)


def _attn_swa_kernel(sink_ref, q_ref, kc_ref, vc_ref, kl_ref, vl_ref, o_ref, *, n_pairs, pairs_per_stage, seq, win):
    tq = q_ref.shape[1]
    i = pl.program_id(2)
    start = pl.multiple_of(jnp.clip(i * tq - WINDOW, 0, seq - win), WINDOW)
    lo = _lane_lo(tq)
    lane = lax.broadcasted_iota(jnp.int32, (tq, LANES), 1)
    qpos = i * tq + lax.broadcasted_iota(jnp.int32, (tq, win), 0)
    kpos = start + lax.broadcasted_iota(jnp.int32, (tq, win), 1)
    bias = jnp.where(jnp.abs(qpos - kpos) <= WINDOW, 0.0, NEG_INF)

    zero = jnp.zeros((tq, LANES), BF16)
    n_stage = n_pairs // pairs_per_stage
    heads = 2 * pairs_per_stage
    bias_rows = jnp.concatenate([bias] * heads, axis=0)

    def scores(t):
        q_rows = []
        for j in range(t * pairs_per_stage, (t + 1) * pairs_per_stage):
            q = q_ref[j]
            q_rows += [jnp.where(lo, q, zero), jnp.where(lo, zero, q)]
        qs = jnp.concatenate(q_rows, axis=0)
        return _qk(qs, kc_ref[0]), _qk(qs, kl_ref[0, pl.ds(start, win), :]) + bias_rows

    def softmax(t, s_ctx, s_band):
        sink = jnp.concatenate([jnp.broadcast_to(sink_ref[h:h + 1, 0:1], (tq, 1))
                                for h in range(t * heads, (t + 1) * heads)], axis=0)
        m = jnp.maximum(jnp.maximum(jnp.max(s_ctx, axis=1, keepdims=True),
                                    jnp.max(s_band, axis=1, keepdims=True)), sink)
        return jnp.exp2(s_ctx - m).astype(BF16), jnp.exp2(s_band - m).astype(BF16), jnp.exp2(sink - m)

    def finish(t, p_ctx, p_band, p_sink):
        acc = (_pv_with_sums(p_ctx, vc_ref[0]) + _pv_with_sums(p_band, vl_ref[0, pl.ds(start, win), :]))
        o = acc[:, :LANES] / (acc[:, LANES:] + p_sink)
        for j in range(pairs_per_stage):
            o_ref[t * pairs_per_stage + j] = jnp.where(
                lane < HALF, o[2 * j * tq:(2 * j + 1) * tq], o[(2 * j + 1) * tq:(2 * j + 2) * tq]).astype(BF16)

    s_vals, p_vals = {}, {}
    for t in range(n_stage + 2):
        if t < n_stage:
            s_vals[t] = scores(t)
        if 0 <= t - 1 < n_stage:
            p_vals[t - 1] = softmax(t - 1, *s_vals.pop(t - 1))
        if 0 <= t - 2 < n_stage:
            finish(t - 2, *p_vals.pop(t - 2))


def _attn_call(geom, kern, lead, lead_specs, q, k, v, *, has_lat, tq, q_per, k_per, v_per, o_per, n_groups, name):
    b, s, c = geom.batch, geom.seq, geom.ctx
    ctx_blk0 = geom.n_lat_rows // c
    if has_lat:
        n_q = s // tq
        q_map = lambda bi, g, i: (g, bi * n_q + i, 0)
        o_rows = geom.n_lat_rows
    else:
        assert tq == c
        n_q = 1
        q_map = lambda bi, g, i: (g, ctx_blk0 + bi, 0)
        o_rows = b * c
    o_map = lambda bi, g, i: (g, bi * n_q + i, 0)
    ctx_map = lambda bi, g, i: (g, ctx_blk0 + bi, 0)
    lat_map = lambda bi, g, i: (g, bi, 0)
    in_specs = list(lead_specs) + [pl.BlockSpec((q_per, tq, LANES), q_map),
                                   pl.BlockSpec((k_per, c, LANES), ctx_map),
                                   pl.BlockSpec((v_per, c, LANES), ctx_map)]
    args = list(lead) + [q, k, v]
    if has_lat:
        in_specs += [pl.BlockSpec((k_per, s, LANES), lat_map), pl.BlockSpec((v_per, s, LANES), lat_map)]
        args += [k, v]
    return pl.pallas_call(
        kern,
        grid=(b, n_groups, n_q),
        in_specs=in_specs,
        out_specs=pl.BlockSpec((o_per, tq, LANES), o_map),
        out_shape=jax.ShapeDtypeStruct((o_per * n_groups, o_rows, LANES), BF16),
        compiler_params=_cparams(3),
        name=name,
    )(*args)


def _rope_tables(seq, rot_dim, aidx, active, scale):
    scale = scale * LOG2E
    pos = np.arange(seq)
    row = jnp.asarray(pos // GRID_W, F32)
    col = jnp.asarray(pos % GRID_W, F32)
    n_axis = rot_dim // 4
    freqs = ROPE_THETA ** (-jnp.arange(n_axis, dtype=F32) / n_axis)
    ang = jnp.concatenate([row[:, None] * freqs, col[:, None] * freqs], axis=-1)
    a = ang[:, np.asarray(aidx)]
    act = jnp.asarray(active)[None, :]
    sign = jnp.asarray(np.where(np.arange(LANES) < HALF, -1.0, 1.0), F32)[None, :]
    c = jnp.where(act, jnp.cos(a), 1.0)
    sn = jnp.where(act, jnp.sin(a) * sign, 0.0)
    c = jnp.concatenate([c, jnp.ones((ROW_TILE, LANES), F32)], axis=0)
    sn = jnp.concatenate([sn, jnp.zeros((ROW_TILE, LANES), F32)], axis=0)
    return (c * scale, sn * scale), (c, sn)


def _take_cols(w, idx):
    idx = np.asarray(idx)
    out = jnp.take(w, jnp.asarray(np.maximum(idx, 0)), axis=1)
    if (idx < 0).any():
        out = jnp.where(jnp.asarray(idx >= 0)[None, :], out, 0.0)
    return out


def _eo(d):
    return np.concatenate([np.arange(0, d, 2), np.arange(1, d, 2)])


def _pair_layout():
    ev, od = np.arange(0, 64, 2), np.arange(1, 64, 2)
    which = np.concatenate([np.zeros(32), np.ones(32), np.zeros(32), np.ones(32)]).astype(np.int64)
    dim = np.concatenate([ev, ev, od, od])
    return which, dim


def kernel(x, c, ctx, c_ctx, ada_w, ada_b, norm_g, ffn_w_in, ffn_w_out, da_w_in, da_lambda, da_subln, da_w_out,
           ga_w_in, ga_q_norm, ga_k_norm, ga_w_out, mla_w_in, mla_q_norm, mla_kv_norm, mla_w_uq, mla_w_ukv,
           mla_w_out, swa_w_in, swa_sink, swa_w_out):
    batch, seq, d = x.shape
    n_ctx = ctx.shape[1]
    depth = ada_w.shape[0]
    assert d == D_MODEL and seq % FFN_TILE == 0 and (batch * n_ctx) % FFN_TILE == 0
    assert n_ctx % 256 == 0 and batch < MOD_ROWS and depth == 4
    geom = _Geom(batch, seq, n_ctx, ROW_TILE)
    fgeom = _Geom(batch, seq, n_ctx, FFN_TILE)

    cond = jnp.zeros((MOD_ROWS, d), F32).at[:batch].set(c).at[batch].set(c_ctx)
    mods = _ada_call(cond, ada_w, ada_b).reshape(depth, MOD_ROWS, 1, N_MOD * d)

    h = (x.reshape(batch * seq, d), ctx.reshape(batch * n_ctx, d))
    g8 = jnp.concatenate([norm_g, jnp.ones((depth, 2, d), F32)], axis=1)
    w_in_b = ffn_w_in.astype(BF16)
    w_out_b = ffn_w_out.astype(BF16)

    which, pdim = _pair_layout()
    tk = 512

    for i in range(depth):
        mod = mods[i]
        last = i == depth - 1
        h = _ffn_call(fgeom, h, mod, g8[i], w_in_b[i, 0], w_out_b[i, 0], k=0, gi=0, name=f"ffn_a{i}")

        if i == 0:
            w = da_w_in[0]
            hq = DA_HEADS * 2 * DA_QK
            qcols = np.concatenate([hd * 128 + which * 64 + pdim for hd in range(DA_HEADS)])
            wcat = jnp.concatenate([_take_cols(w, qcols), _take_cols(w, hq + qcols), w[:, 2 * hq:]], axis=1)
            tq_tab, tk_tab = _rope_tables(seq, DA_QK, np.arange(LANES) % 32, np.ones(LANES, bool), DA_QK ** -0.5)
            q, k, v = _proj_call(geom, functools.partial(_proj0_kernel, nq=8, nk=8, nv=8), h, mod, g8[i],
                                 [wcat.astype(BF16)], [*tq_tab, *tk_tab], 8, 8, 8, name="proj_diff")
            lam_init = 0.8 - 0.6 * math.exp(-0.3 * i)
            lead = [da_lambda[0], da_subln[0].reshape(1, LANES)]
            lead_specs = [pl.BlockSpec((4, DA_QK), lambda bi, g, qi: (0, 0)),
                          pl.BlockSpec((1, LANES), lambda bi, g, qi: (0, 0))]
            hs = 4
            mk = lambda has_lat: functools.partial(_attn_diff_kernel, has_lat=has_lat, n_heads=hs, n_ctx_keys=n_ctx,
                                                   n_lat_keys=seq, tk=tk, lam_init=lam_init)
            o_lat = _attn_call(geom, mk(True), lead, lead_specs, q, k, v, has_lat=True, tq=512, q_per=hs, k_per=hs,
                               v_per=hs, o_per=hs, n_groups=DA_HEADS // hs, name="attn_diff")
            o_ctx = _attn_call(geom, mk(False), lead, lead_specs, q, k, v, has_lat=False, tq=n_ctx, q_per=hs,
                               k_per=hs, v_per=hs, o_per=hs, n_groups=DA_HEADS // hs, name="attn_diff_ctx")
            w_mo = da_w_out[0]
        elif i == 1:
            w = ga_w_in[0]
            eo = _eo(GA_DIM)
            qcols = np.concatenate([hd * GA_DIM + eo for hd in range(GA_HEADS)])
            kcols = GA_HEADS * GA_DIM + np.concatenate([hd * GA_DIM + eo for hd in range(GA_KV)])
            wcat = jnp.concatenate([_take_cols(w, qcols), _take_cols(w, kcols), w[:, (GA_HEADS + GA_KV) * GA_DIM:]],
                                   axis=1)
            tq_tab, tk_tab = _rope_tables(seq, GA_DIM, np.arange(LANES) % 64, np.ones(LANES, bool), GA_DIM ** -0.5)
            consts = [wcat.astype(BF16), ga_q_norm[0][eo].reshape(1, LANES), ga_k_norm[0][eo].reshape(1, LANES)]
            q, k, v = _proj_call(geom, functools.partial(_proj1_kernel, nq=8, nk=2, nv=2), h, mod, g8[i],
                                 consts, [*tq_tab, *tk_tab], 8, 2, 2, name="proj_gqa")
            grp = GA_HEADS // GA_KV
            mk = lambda has_lat: functools.partial(_attn_gqa_kernel, has_lat=has_lat, grp=grp, n_ctx_keys=n_ctx,
                                                   n_lat_keys=seq, tk=tk)
            o_lat = _attn_call(geom, mk(True), [], [], q, k, v, has_lat=True, tq=256, q_per=grp, k_per=1, v_per=1,
                               o_per=grp, n_groups=GA_KV, name="attn_gqa")
            o_ctx = _attn_call(geom, mk(False), [], [], q, k, v, has_lat=False, tq=n_ctx, q_per=grp, k_per=1,
                               v_per=1, o_per=grp, n_groups=GA_KV, name="attn_gqa_ctx")
            w_mo = ga_w_out[0]
        elif i == 2:
            lanes = np.arange(LANES)
            is_rope = ((lanes % HALF) >= 32) & ((lanes % HALF) < 48)
            is_nope = (lanes % HALF) < 32
            rope_dim = 2 * ((lanes % HALF) - 32) + (lanes >= HALF)
            nope_dim = (lanes % HALF) + 32 * (lanes >= HALF)
            w = mla_w_in[0]
            kr_cols = np.where(is_rope, MLA_Q_RANK + MLA_KV_RANK + rope_dim, -1)
            w1 = jnp.concatenate([w[:, :MLA_Q_RANK + MLA_KV_RANK], _take_cols(w, kr_cols)], axis=1)
            qd = MLA_NOPE + MLA_ROPE
            q_blk = np.where(is_nope, nope_dim, np.where(is_rope, MLA_NOPE + rope_dim, -1))
            q_cols = np.concatenate([np.where(q_blk >= 0, hd * qd + q_blk, -1) for hd in range(MLA_HEADS)])
            kvd = MLA_NOPE + MLA_V
            k_blk = np.where(is_nope, nope_dim, -1)
            k_cols = np.concatenate([np.where(k_blk >= 0, hd * kvd + k_blk, -1) for hd in range(MLA_HEADS)])
            v_cols = np.concatenate([hd * kvd + MLA_NOPE + np.arange(MLA_V) for hd in range(MLA_HEADS)])
            consts = [w1.astype(BF16), mla_q_norm[0].reshape(1, -1), mla_kv_norm[0].reshape(1, -1),
                      _take_cols(mla_w_uq[0], q_cols).astype(BF16), _take_cols(mla_w_ukv[0], k_cols).astype(BF16),
                      _take_cols(mla_w_ukv[0], v_cols).astype(BF16)]
            aidx = np.where(is_rope, (lanes % HALF) - 32, 0)
            tq_tab, tk_tab = _rope_tables(seq, MLA_ROPE, aidx, is_rope, (MLA_NOPE + MLA_ROPE) ** -0.5)
            q, k, v = _proj_call(geom, _proj2_kernel, h, mod, g8[i], consts, [*tq_tab, *tk_tab],
                                 MLA_HEADS, MLA_HEADS, MLA_HEADS // 2, name="proj_mla")
            n_pairs = 4
            mk = lambda has_lat: functools.partial(_attn_mla_kernel, has_lat=has_lat, n_pairs=n_pairs,
                                                   n_ctx_keys=n_ctx, n_lat_keys=seq, tk=tk)
            o_lat = _attn_call(geom, mk(True), [], [], q, k, v, has_lat=True, tq=1024, q_per=2 * n_pairs,
                               k_per=2 * n_pairs, v_per=n_pairs, o_per=n_pairs,
                               n_groups=MLA_HEADS // (2 * n_pairs), name="attn_mla")
            o_ctx = _attn_call(geom, mk(False), [], [], q, k, v, has_lat=False, tq=n_ctx, q_per=2 * n_pairs,
                               k_per=2 * n_pairs, v_per=n_pairs, o_per=n_pairs,
                               n_groups=MLA_HEADS // (2 * n_pairs), name="attn_mla_ctx")
            w_mo = mla_w_out[0]
        else:
            w = swa_w_in[0]
            hq = SWA_HEADS * SWA_DIM
            qcols = np.concatenate([(2 * p + which) * SWA_DIM + pdim for p in range(SWA_HEADS // 2)])
            kcols = np.concatenate([hq + g * SWA_DIM + pdim for g in range(SWA_KV)])
            vcols = np.concatenate([hq + SWA_KV * SWA_DIM + g * SWA_DIM + (np.arange(LANES) % SWA_DIM)
                                    for g in range(SWA_KV)])
            wcat = jnp.concatenate([_take_cols(w, qcols), _take_cols(w, kcols), _take_cols(w, vcols)], axis=1)
            tq_tab, tk_tab = _rope_tables(seq, SWA_DIM, np.arange(LANES) % 32, np.ones(LANES, bool), SWA_DIM ** -0.5)
            q, k, v = _proj_call(geom, functools.partial(_proj0_kernel, nq=8, nk=2, nv=2), h, mod, g8[i],
                                 [wcat.astype(BF16)], [*tq_tab, *tk_tab], 8, 2, 2, name="proj_swa")
            hpg = SWA_HEADS // SWA_KV
            sink = jnp.broadcast_to((swa_sink[0].astype(F32) * LOG2E).reshape(SWA_KV, hpg, 1), (SWA_KV, hpg, LANES))
            tq = 256
            kern = functools.partial(_attn_swa_kernel, n_pairs=hpg // 2, pairs_per_stage=2, seq=seq,
                                     win=tq + 2 * WINDOW)
            o_lat = _attn_call(geom, kern, [sink], [pl.BlockSpec((None, hpg, LANES), lambda bi, g, qi: (g, 0, 0))],
                               q, k, v, has_lat=True, tq=tq, q_per=hpg // 2, k_per=1, v_per=1, o_per=hpg // 2,
                               n_groups=SWA_KV, name="attn_swa")
            o_ctx = None
            w_mo = swa_w_out[0]

        h = _ffn_call(fgeom, h, mod, g8[i], w_in_b[i, 1], w_out_b[i, 1], k=6, gi=4,
                      pre=(o_lat, o_ctx, w_mo.astype(BF16)), n_tiles=fgeom.n_lat if last else None,
                      name=f"ffn_b{i}")
    return h.reshape(batch, seq, d)
```

```python
import functools
import math

import numpy as np
import jax
import jax.numpy as jnp
from jax import lax
from jax.experimental import pallas as pl
from jax.experimental.pallas import tpu as pltpu

F32 = jnp.float32
BF16 = jnp.bfloat16

D_MODEL = 1024
D_FF = 2816
N_MOD = 9
NORM_EPS = 1e-6
GRID_W = 64
ROPE_THETA = 10000.0
NEG_INF = -1e30
WINDOW = 128
LOG2E = math.log2(math.e)

LANES = 128
HALF = LANES // 2
ROW_TILE = 512
FFN_HALF = 512
FFN_TILE = 2 * FFN_HALF
FF_CHUNK = 256
MOD_ROWS = 16
VMEM_LIMIT = 52 * 1024 * 1024
FFN_VMEM_LIMIT = 58 * 1024 * 1024

DA_HEADS, DA_QK = 8, 64
GA_HEADS, GA_KV, GA_DIM = 8, 2, 128
MLA_HEADS, MLA_Q_RANK, MLA_KV_RANK, MLA_NOPE, MLA_ROPE, MLA_V = 16, 256, 128, 64, 32, 64
SWA_HEADS, SWA_KV, SWA_DIM = 16, 2, 64


def _cparams(n_axes, vmem_limit=VMEM_LIMIT):
    return pltpu.CompilerParams(dimension_semantics=("arbitrary",) * n_axes,
                                vmem_limit_bytes=vmem_limit)


def _rms(x, g):
    return x * lax.rsqrt(jnp.mean(x * x, axis=-1, keepdims=True) + NORM_EPS) * g


def _mod(mod_ref, k):
    return mod_ref[:, k * D_MODEL:(k + 1) * D_MODEL]


def _modnorm(x, g, mod_ref, k):
    return _rms(x, g) * (1.0 + _mod(mod_ref, k + 1)) + _mod(mod_ref, k)


def _rope(x, c, s):
    return x * c + pltpu.roll(x, HALF, 1) * s


def _ada_kernel(c_ref, w_ref, b_ref, o_ref):
    c = c_ref[...]
    sc = c / (1.0 + jnp.exp(-c))
    o_ref[...] = jnp.dot(sc, w_ref[...], precision=lax.Precision.HIGHEST,
                         preferred_element_type=F32) + b_ref[...]


def _ada_call(cond, ada_w, ada_b):
    depth, d, n = ada_w.shape
    cw = 1024
    return pl.pallas_call(
        _ada_kernel,
        grid=(depth, n // cw),
        in_specs=[pl.BlockSpec((MOD_ROWS, d), lambda l, j: (0, 0)),
                  pl.BlockSpec((None, d, cw), lambda l, j: (l, 0, j)),
                  pl.BlockSpec((None, 1, cw), lambda l, j: (l, 0, j))],
        out_specs=pl.BlockSpec((None, MOD_ROWS, cw), lambda l, j: (l, 0, j)),
        out_shape=jax.ShapeDtypeStruct((depth, MOD_ROWS, n), F32),
        compiler_params=_cparams(2),
        name="adaln",
    )(cond, ada_w, ada_b.reshape(depth, 1, n))


class _Geom:
    def __init__(self, batch, seq, ctx, tile):
        self.batch, self.seq, self.ctx, self.tile = batch, seq, ctx, tile
        self.n_lat_rows = batch * seq
        self.n_rows = batch * (seq + ctx)
        self.n_lat = self.n_lat_rows // tile
        self.n_tiles = self.n_rows // tile
        self.tiles_per_batch = seq // tile

    def mod_row(self, t):
        return jnp.where(t < self.n_lat, t // self.tiles_per_batch, self.batch)

    def rope_blk(self, t):
        return jnp.where(t < self.n_lat, t % self.tiles_per_batch, self.tiles_per_batch)


def _const_spec(shape):
    nd = len(shape)
    return pl.BlockSpec(shape, lambda *_: (0,) * nd, pipeline_mode=pl.Buffered(1))


def _ffn_kernel(*refs, k, gi, n_lat, split_in, has_pre, has_ctx):
    it = iter(refs)
    x_ref = next(it)
    xc_ref = next(it) if split_in else None
    mod_ref, g_ref, win_ref, wout_ref = (next(it) for _ in range(4))
    if has_pre:
        ol_ref = next(it)
        oc_ref = next(it) if has_ctx else None
        wmo_ref = next(it)
    out_ref = next(it)
    is_lat = pl.program_id(0) < n_lat

    def prologue(r0):
        rows = slice(r0, r0 + FFN_HALF)
        x = x_ref[rows, :]
        if split_in:
            x = jnp.where(is_lat, x, xc_ref[rows, :])
        if has_pre:
            nblk = ol_ref.shape[0]
            o = jnp.concatenate([ol_ref[i, rows, :] for i in range(nblk)], axis=1)
            if has_ctx:
                o = jnp.where(is_lat, o, jnp.concatenate([oc_ref[i, rows, :] for i in range(nblk)], axis=1))
            y = jnp.dot(o, wmo_ref[...], preferred_element_type=F32)
            x = x + _mod(mod_ref, 5) * _rms(y, g_ref[3:4, :])
        return x, _modnorm(x, g_ref[gi:gi + 1, :], mod_ref, k).astype(BF16)

    def gate_up(u, c):
        lo = c * FF_CHUNK
        return (jnp.dot(u, win_ref[:, lo:lo + FF_CHUNK], preferred_element_type=F32),
                jnp.dot(u, win_ref[:, D_FF + lo:D_FF + lo + FF_CHUNK], preferred_element_type=F32))

    n_chunks = D_FF // FF_CHUNK
    steps = [(blk, c) for blk in range(FFN_TILE // FFN_HALF) for c in range(n_chunks)]
    xs, us = {}, {}
    xs[0], us[0] = prologue(0)
    nxt = gate_up(us[0], 0)
    xs[1], us[1] = prologue(FFN_HALF)
    acc = None
    for idx, (blk, c) in enumerate(steps):
        a, b = nxt
        if idx + 1 < len(steps):
            nxt = gate_up(us[steps[idx + 1][0]], steps[idx + 1][1])
        act = (a / (1.0 + jnp.exp(-a)) * b).astype(BF16)
        part = jnp.dot(act, wout_ref[c * FF_CHUNK:(c + 1) * FF_CHUNK, :], preferred_element_type=F32)
        acc = part if c == 0 else acc + part
        if c == n_chunks - 1:
            r0 = blk * FFN_HALF
            out_ref[r0:r0 + FFN_HALF, :] = (xs.pop(blk)
                                            + 0.5 * _mod(mod_ref, k + 2) * _rms(acc, g_ref[gi + 1:gi + 2, :]))


def _ffn_call(geom, h, mod, g8, w_in, w_out, layer, half, *, k, gi, pre=None, n_tiles=None, name):
    n_tiles = geom.n_tiles if n_tiles is None else n_tiles
    d = D_MODEL
    w_spec = lambda w: pl.BlockSpec((None, None) + w.shape[2:], lambda t: (layer, half, 0, 0),
                                    pipeline_mode=pl.Buffered(1))
    split_in = isinstance(h, tuple)
    if split_in:
        in_specs = [pl.BlockSpec((FFN_TILE, d), lambda t: (jnp.minimum(t, geom.n_lat - 1), 0)),
                    pl.BlockSpec((FFN_TILE, d), lambda t: (jnp.maximum(t - geom.n_lat, 0), 0))]
        args = list(h)
    else:
        in_specs = [pl.BlockSpec((FFN_TILE, d), lambda t: (t, 0))]
        args = [h]
    in_specs += [pl.BlockSpec((None, 1, N_MOD * d), lambda t: (geom.mod_row(t), 0, 0)),
                 _const_spec(g8.shape), w_spec(w_in), w_spec(w_out)]
    args += [mod, g8, w_in, w_out]
    has_pre = pre is not None
    has_ctx = False
    if has_pre:
        o_lat, o_ctx, w_mo = pre
        nblk = o_lat.shape[0]
        in_specs.append(pl.BlockSpec((nblk, FFN_TILE, LANES),
                                     lambda t: (0, jnp.minimum(t, geom.n_lat - 1), 0)))
        args.append(o_lat)
        has_ctx = o_ctx is not None
        if has_ctx:
            in_specs.append(pl.BlockSpec((nblk, FFN_TILE, LANES),
                                         lambda t: (0, jnp.maximum(t - geom.n_lat, 0), 0)))
            args.append(o_ctx)
        in_specs.append(_const_spec(w_mo.shape))
        args.append(w_mo)
    kern = functools.partial(_ffn_kernel, k=k, gi=gi, n_lat=geom.n_lat, split_in=split_in, has_pre=has_pre,
                             has_ctx=has_ctx)
    return pl.pallas_call(
        kern,
        grid=(n_tiles,),
        in_specs=in_specs,
        out_specs=pl.BlockSpec((FFN_TILE, d), lambda t: (t, 0)),
        out_shape=jax.ShapeDtypeStruct((n_tiles * FFN_TILE, d), F32),
        compiler_params=_cparams(1, FFN_VMEM_LIMIT),
        name=name,
    )(*args)


def _store_blocks(dst_ref, x, n, fn=None):
    for i in range(n):
        blk = x[:, i * LANES:(i + 1) * LANES]
        if fn is not None:
            blk = fn(blk)
        dst_ref[i] = blk.astype(BF16)


def _proj0_kernel(x_ref, mod_ref, g_ref, w_ref, cq_ref, sq_ref, ck_ref, sk_ref, q_ref, k_ref, v_ref, *, nq, nk, nv):
    u = _modnorm(x_ref[...], g_ref[2:3, :], mod_ref, 3).astype(BF16)
    cq, sq, ck, sk = cq_ref[...], sq_ref[...], ck_ref[...], sk_ref[...]
    t = jnp.dot(u, w_ref[...], preferred_element_type=F32)
    _store_blocks(q_ref, t, nq, lambda b: _rope(b, cq, sq))
    _store_blocks(k_ref, t[:, nq * LANES:], nk, lambda b: _rope(b, ck, sk))
    _store_blocks(v_ref, t[:, (nq + nk) * LANES:], nv)


def _proj1_kernel(x_ref, mod_ref, g_ref, w_ref, qn_ref, kn_ref, cq_ref, sq_ref, ck_ref, sk_ref,
                  q_ref, k_ref, v_ref, *, nq, nk, nv):
    u = _modnorm(x_ref[...], g_ref[2:3, :], mod_ref, 3).astype(BF16)
    cq, sq, ck, sk = cq_ref[...], sq_ref[...], ck_ref[...], sk_ref[...]
    qn, kn = qn_ref[...], kn_ref[...]
    t = jnp.dot(u, w_ref[...], preferred_element_type=F32)
    _store_blocks(q_ref, t, nq, lambda b: _rope(_rms(b, qn), cq, sq))
    _store_blocks(k_ref, t[:, nq * LANES:], nk, lambda b: _rope(_rms(b, kn), ck, sk))
    _store_blocks(v_ref, t[:, (nq + nk) * LANES:], nv)


def _proj2_kernel(x_ref, mod_ref, g_ref, w1_ref, qn_ref, kvn_ref, wuq_ref, wuk_ref, wuv_ref,
                  cq_ref, sq_ref, ck_ref, sk_ref, q_ref, k_ref, v_ref):
    u = _modnorm(x_ref[...], g_ref[2:3, :], mod_ref, 3).astype(BF16)
    cq, sq, ck, sk = cq_ref[...], sq_ref[...], ck_ref[...], sk_ref[...]
    t = jnp.dot(u, w1_ref[...], preferred_element_type=F32)
    cqv = _rms(t[:, :MLA_Q_RANK], qn_ref[...]).astype(BF16)
    ckv = _rms(t[:, MLA_Q_RANK:MLA_Q_RANK + MLA_KV_RANK], kvn_ref[...]).astype(BF16)
    kr = _rope(t[:, MLA_Q_RANK + MLA_KV_RANK:], ck, sk)
    qa = jnp.dot(cqv, wuq_ref[...], preferred_element_type=F32)
    _store_blocks(q_ref, qa, MLA_HEADS, lambda b: _rope(b, cq, sq))
    ka = jnp.dot(ckv, wuk_ref[...], preferred_element_type=F32)
    _store_blocks(k_ref, ka, MLA_HEADS, lambda b: b + kr)
    va = jnp.dot(ckv, wuv_ref[...], preferred_element_type=F32)
    _store_blocks(v_ref, va, MLA_HEADS // 2)


def _proj_call(geom, kern, h, mod, g8, consts, tables, nq, nk, nv, name):
    d = D_MODEL
    tile = lambda t: (t, 0)
    rope_spec = pl.BlockSpec((ROW_TILE, LANES), lambda t: (geom.rope_blk(t), 0))
    in_specs = ([pl.BlockSpec((ROW_TILE, d), tile),
                 pl.BlockSpec((None, 1, N_MOD * d), lambda t: (geom.mod_row(t), 0, 0)),
                 _const_spec(g8.shape)]
                + [_const_spec(c.shape) for c in consts]
                + [rope_spec] * 4)
    blk = lambda n: pl.BlockSpec((n, ROW_TILE, LANES), lambda t: (0, t, 0))
    shp = lambda n: jax.ShapeDtypeStruct((n, geom.n_rows, LANES), BF16)
    return pl.pallas_call(
        kern,
        grid=(geom.n_tiles,),
        in_specs=in_specs,
        out_specs=[blk(nq), blk(nk), blk(nv)],
        out_shape=[shp(nq), shp(nk), shp(nv)],
        compiler_params=_cparams(1),
        name=name,
    )(h, mod, g8, *consts, *tables)


def _qk(qs, k):
    return lax.dot_general(qs, k, (((1,), (1,)), ((), ())), preferred_element_type=F32)


def _pv_with_sums(p, v):
    v_ext = jnp.concatenate([v, jnp.ones_like(v)], axis=1)
    return jnp.dot(p, v_ext, preferred_element_type=F32)


def _flash_stream(jobs):
    steps = [(j, c) for j, (_, chunks, _) in enumerate(jobs) for c in range(len(chunks))]
    qs_of = {}

    def scores(j, c):
        if j not in qs_of:
            qs_of[j] = jobs[j][0]()
        k_ref, hk, _, _, start, size = jobs[j][1][c]
        return _qk(qs_of[j], k_ref[hk, pl.ds(start, size), :])

    m = acc = None
    s_next = scores(*steps[0])
    for idx, (j, c) in enumerate(steps):
        s = s_next
        if idx + 1 < len(steps):
            s_next = scores(*steps[idx + 1])
        _, _, v_ref, hv, start, size = jobs[j][1][c]
        mc = jnp.max(s, axis=1, keepdims=True)
        m_new = mc if c == 0 else jnp.maximum(m, mc)
        p = jnp.exp2(s - m_new).astype(BF16)
        pv = _pv_with_sums(p, v_ref[hv, pl.ds(start, size), :])
        acc = pv if c == 0 else jnp.exp2(m - m_new) * acc + pv
        m = m_new
        if c == len(jobs[j][1]) - 1:
            qs_of.pop(j)
            jobs[j][2](acc[:, :LANES], acc[:, LANES:])


def _kv_chunks(kc_ref, vc_ref, kl_ref, vl_ref, hk, hv, n_ctx_keys, n_lat_keys, tk):
    chunks = [(kc_ref, hk, vc_ref, hv, 0, n_ctx_keys)]
    if kl_ref is not None:
        chunks += [(kl_ref, hk, vl_ref, hv, j * tk, tk) for j in range(n_lat_keys // tk)]
    return chunks


def _split_refs(refs, n_lead, has_lat):
    lead = refs[:n_lead]
    rest = refs[n_lead:]
    q_ref, kc_ref, vc_ref = rest[:3]
    if has_lat:
        kl_ref, vl_ref, o_ref = rest[3:6]
    else:
        kl_ref = vl_ref = None
        o_ref = rest[3]
    return lead, q_ref, kc_ref, vc_ref, kl_ref, vl_ref, o_ref


def _lane_lo(rows):
    lane = lax.broadcasted_iota(jnp.int32, (rows, LANES), 1)
    return (lane % HALF) < (HALF // 2)


def _attn_diff_kernel(*refs, has_lat, n_heads, n_ctx_keys, n_lat_keys, tk, lam_init):
    (lam_ref, subln_ref), q_ref, kc_ref, vc_ref, kl_ref, vl_ref, o_ref = _split_refs(refs, 2, has_lat)
    tq = q_ref.shape[1]
    lo = _lane_lo(tq)
    lf = lam_ref[...]
    lam = (jnp.exp(jnp.sum(lf[0:1] * lf[1:2], keepdims=True))
           - jnp.exp(jnp.sum(lf[2:3] * lf[3:4], keepdims=True)) + lam_init)
    subln = subln_ref[...]

    def job(h):
        def make_qs():
            q = q_ref[h]
            zero = jnp.zeros_like(q)
            return jnp.concatenate([jnp.where(lo, q, zero), jnp.where(lo, zero, q)], axis=0)

        def done(acc, l):
            o = acc / l
            o = o[:tq] - lam * o[tq:]
            o_ref[h] = (_rms(o, subln) * (1.0 - lam_init)).astype(BF16)

        return make_qs, _kv_chunks(kc_ref, vc_ref, kl_ref, vl_ref, h, h, n_ctx_keys, n_lat_keys, tk), done

    def head_pair(i, carry):
        _flash_stream([job(2 * i), job(2 * i + 1)])
        return carry

    lax.fori_loop(0, n_heads // 2, head_pair, 0)


def _attn_gqa_kernel(*refs, has_lat, grp, n_ctx_keys, n_lat_keys, tk):
    _, q_ref, kc_ref, vc_ref, kl_ref, vl_ref, o_ref = _split_refs(refs, 0, has_lat)
    tq = q_ref.shape[1]

    def done(acc, l):
        o = acc / l
        for j in range(grp):
            o_ref[j] = o[j * tq:(j + 1) * tq].astype(BF16)

    _flash_stream([(lambda: jnp.concatenate([q_ref[j] for j in range(grp)], axis=0),
                    _kv_chunks(kc_ref, vc_ref, kl_ref, vl_ref, 0, 0, n_ctx_keys, n_lat_keys, tk), done)])


def _attn_mla_kernel(*refs, has_lat, n_pairs, n_ctx_keys, n_lat_keys, tk):
    _, q_ref, kc_ref, vc_ref, kl_ref, vl_ref, o_ref = _split_refs(refs, 0, has_lat)
    tq = q_ref.shape[1]
    lane = lax.broadcasted_iota(jnp.int32, (tq, LANES), 1)

    def pair(p, carry):
        even_out = []

        def job(e):
            h = 2 * p + e

            def done(acc, l):
                if e == 0:
                    even_out.append(acc / l)
                else:
                    o_ref[p] = jnp.where(lane < HALF, even_out[0], acc / l).astype(BF16)

            return ((lambda: q_ref[h]),
                    _kv_chunks(kc_ref, vc_ref, kl_ref, vl_ref, h, p, n_ctx_keys, n_lat_keys, tk), done)

        _flash_stream([job(0), job(1)])
        return carry

    lax.fori_loop(0, n_pairs, pair, 0)


def _attn_swa_kernel(sink_ref, q_ref, kc_ref, vc_ref, kl_ref, vl_ref, o_ref, *, n_pairs, pairs_per_stage, seq, win):
    tq = q_ref.shape[1]
    i = pl.program_id(2)
    start = pl.multiple_of(jnp.clip(i * tq - WINDOW, 0, seq - win), WINDOW)
    lo = _lane_lo(tq)
    kpos = start + lax.broadcasted_iota(jnp.int32, (win, tq), 0)
    qpos = i * tq + lax.broadcasted_iota(jnp.int32, (win, tq), 1)
    bias = jnp.where(jnp.abs(qpos - kpos) <= WINDOW, 0.0, NEG_INF)

    zero = jnp.zeros((tq, LANES), BF16)
    n_stage = n_pairs // pairs_per_stage
    heads = 2 * pairs_per_stage
    bias_cols = jnp.concatenate([bias] * heads, axis=1)
    k_ctx, k_win = kc_ref[0], kl_ref[0, pl.ds(start, win), :]

    def values_and_ones(v):
        lane = lax.broadcasted_iota(jnp.int32, v.shape, 1)
        return jnp.where(lane < HALF, v, jnp.ones_like(v))

    v_ctx, v_win = values_and_ones(vc_ref[0]), values_and_ones(vl_ref[0, pl.ds(start, win), :])

    def scores(t):
        q_rows = []
        for j in range(t * pairs_per_stage, (t + 1) * pairs_per_stage):
            q = q_ref[j]
            q_rows += [jnp.where(lo, q, zero), jnp.where(lo, zero, q)]
        qs = jnp.concatenate(q_rows, axis=0)
        return _qk(k_ctx, qs), _qk(k_win, qs) + bias_cols

    def softmax(t, s_ctx, s_band):
        sink = jnp.concatenate([sink_ref[h:h + 1, :] for h in range(t * heads, (t + 1) * heads)
                                for _ in range(tq // LANES)], axis=1)
        m = jnp.maximum(jnp.maximum(jnp.max(s_ctx, axis=0, keepdims=True),
                                    jnp.max(s_band, axis=0, keepdims=True)), sink)
        return jnp.exp2(s_ctx - m).astype(BF16), jnp.exp2(s_band - m).astype(BF16), jnp.exp2(sink - m)

    def pv_t(v, p):
        return lax.dot_general(v, p, (((0,), (0,)), ((), ())), preferred_element_type=F32)

    def finish(t, p_ctx, p_band, p_sink):
        acc = pv_t(v_ctx, p_ctx) + pv_t(v_win, p_band)
        o_t = acc[:HALF] / (acc[HALF:HALF + 1] + p_sink)
        for j in range(pairs_per_stage):
            pair_t = jnp.concatenate([o_t[:, 2 * j * tq:(2 * j + 1) * tq],
                                      o_t[:, (2 * j + 1) * tq:(2 * j + 2) * tq]], axis=0)
            o_ref[t * pairs_per_stage + j] = pair_t.T.astype(BF16)

    s_vals, p_vals = {}, {}
    for t in range(n_stage + 2):
        if t < n_stage:
            s_vals[t] = scores(t)
        if 0 <= t - 1 < n_stage:
            p_vals[t - 1] = softmax(t - 1, *s_vals.pop(t - 1))
        if 0 <= t - 2 < n_stage:
            finish(t - 2, *p_vals.pop(t - 2))


def _attn_call(geom, kern, lead, lead_specs, q, k, v, *, has_lat, tq, q_per, k_per, v_per, o_per, n_groups, name):
    b, s, c = geom.batch, geom.seq, geom.ctx
    ctx_blk0 = geom.n_lat_rows // c
    if has_lat:
        n_q = s // tq
        q_map = lambda bi, g, i: (g, bi * n_q + i, 0)
        o_rows = geom.n_lat_rows
    else:
        assert tq == c
        n_q = 1
        q_map = lambda bi, g, i: (g, ctx_blk0 + bi, 0)
        o_rows = b * c
    o_map = lambda bi, g, i: (g, bi * n_q + i, 0)
    ctx_map = lambda bi, g, i: (g, ctx_blk0 + bi, 0)
    lat_map = lambda bi, g, i: (g, bi, 0)
    in_specs = list(lead_specs) + [pl.BlockSpec((q_per, tq, LANES), q_map),
                                   pl.BlockSpec((k_per, c, LANES), ctx_map),
                                   pl.BlockSpec((v_per, c, LANES), ctx_map)]
    args = list(lead) + [q, k, v]
    if has_lat:
        in_specs += [pl.BlockSpec((k_per, s, LANES), lat_map), pl.BlockSpec((v_per, s, LANES), lat_map)]
        args += [k, v]
    return pl.pallas_call(
        kern,
        grid=(b, n_groups, n_q),
        in_specs=in_specs,
        out_specs=pl.BlockSpec((o_per, tq, LANES), o_map),
        out_shape=jax.ShapeDtypeStruct((o_per * n_groups, o_rows, LANES), BF16),
        compiler_params=_cparams(3),
        name=name,
    )(*args)


def _rope_tables(seq, rot_dim, aidx, active, scale):
    scale = scale * LOG2E
    pos = np.arange(seq)
    row = jnp.asarray(pos // GRID_W, F32)
    col = jnp.asarray(pos % GRID_W, F32)
    n_axis = rot_dim // 4
    freqs = ROPE_THETA ** (-jnp.arange(n_axis, dtype=F32) / n_axis)
    ang = jnp.concatenate([row[:, None] * freqs, col[:, None] * freqs], axis=-1)
    a = ang[:, np.asarray(aidx)]
    act = jnp.asarray(active)[None, :]
    sign = jnp.asarray(np.where(np.arange(LANES) < HALF, -1.0, 1.0), F32)[None, :]
    c = jnp.where(act, jnp.cos(a), 1.0)
    sn = jnp.where(act, jnp.sin(a) * sign, 0.0)
    c = jnp.concatenate([c, jnp.ones((ROW_TILE, LANES), F32)], axis=0)
    sn = jnp.concatenate([sn, jnp.zeros((ROW_TILE, LANES), F32)], axis=0)
    return (c * scale, sn * scale), (c, sn)


def _take_cols(w, idx):
    idx = np.asarray(idx)
    out = jnp.take(w, jnp.asarray(np.maximum(idx, 0)), axis=1)
    if (idx < 0).any():
        out = jnp.where(jnp.asarray(idx >= 0)[None, :], out, 0.0)
    return out


def _eo(d):
    return np.concatenate([np.arange(0, d, 2), np.arange(1, d, 2)])


def _pair_layout():
    ev, od = np.arange(0, 64, 2), np.arange(1, 64, 2)
    which = np.concatenate([np.zeros(32), np.ones(32), np.zeros(32), np.ones(32)]).astype(np.int64)
    dim = np.concatenate([ev, ev, od, od])
    return which, dim


def kernel(x, c, ctx, c_ctx, ada_w, ada_b, norm_g, ffn_w_in, ffn_w_out, da_w_in, da_lambda, da_subln, da_w_out,
           ga_w_in, ga_q_norm, ga_k_norm, ga_w_out, mla_w_in, mla_q_norm, mla_kv_norm, mla_w_uq, mla_w_ukv,
           mla_w_out, swa_w_in, swa_sink, swa_w_out):
    batch, seq, d = x.shape
    n_ctx = ctx.shape[1]
    depth = ada_w.shape[0]
    assert d == D_MODEL and seq % FFN_TILE == 0 and (batch * n_ctx) % FFN_TILE == 0
    assert n_ctx % 256 == 0 and batch < MOD_ROWS and depth == 4
    geom = _Geom(batch, seq, n_ctx, ROW_TILE)
    fgeom = _Geom(batch, seq, n_ctx, FFN_TILE)

    cond = jnp.zeros((MOD_ROWS, d), F32).at[:batch].set(c).at[batch].set(c_ctx)
    mods = _ada_call(cond, ada_w, ada_b).reshape(depth, MOD_ROWS, 1, N_MOD * d)

    h = (x.reshape(batch * seq, d), ctx.reshape(batch * n_ctx, d))
    g8 = jnp.concatenate([norm_g, jnp.ones((depth, 2, d), F32)], axis=1)
    w_in_b = ffn_w_in.astype(BF16)
    w_out_b = ffn_w_out.astype(BF16)

    which, pdim = _pair_layout()
    tk = 512

    for i in range(depth):
        mod = mods[i]
        last = i == depth - 1
        h = _ffn_call(fgeom, h, mod, g8[i], w_in_b, w_out_b, i, 0, k=0, gi=0, name=f"ffn_a{i}")

        if i == 0:
            w = da_w_in[0]
            hq = DA_HEADS * 2 * DA_QK
            qcols = np.concatenate([hd * 128 + which * 64 + pdim for hd in range(DA_HEADS)])
            wcat = jnp.concatenate([_take_cols(w, qcols), _take_cols(w, hq + qcols), w[:, 2 * hq:]], axis=1)
            tq_tab, tk_tab = _rope_tables(seq, DA_QK, np.arange(LANES) % 32, np.ones(LANES, bool), DA_QK ** -0.5)
            q, k, v = _proj_call(geom, functools.partial(_proj0_kernel, nq=8, nk=8, nv=8), h, mod, g8[i],
                                 [wcat.astype(BF16)], [*tq_tab, *tk_tab], 8, 8, 8, name="proj_diff")
            lam_init = 0.8 - 0.6 * math.exp(-0.3 * i)
            lead = [da_lambda[0], da_subln[0].reshape(1, LANES)]
            lead_specs = [pl.BlockSpec((4, DA_QK), lambda bi, g, qi: (0, 0)),
                          pl.BlockSpec((1, LANES), lambda bi, g, qi: (0, 0))]
            hs = 4
            mk = lambda has_lat: functools.partial(_attn_diff_kernel, has_lat=has_lat, n_heads=hs, n_ctx_keys=n_ctx,
                                                   n_lat_keys=seq, tk=tk, lam_init=lam_init)
            o_lat = _attn_call(geom, mk(True), lead, lead_specs, q, k, v, has_lat=True, tq=512, q_per=hs, k_per=hs,
                               v_per=hs, o_per=hs, n_groups=DA_HEADS // hs, name="attn_diff")
            o_ctx = _attn_call(geom, mk(False), lead, lead_specs, q, k, v, has_lat=False, tq=n_ctx, q_per=hs,
                               k_per=hs, v_per=hs, o_per=hs, n_groups=DA_HEADS // hs, name="attn_diff_ctx")
            w_mo = da_w_out[0]
        elif i == 1:
            w = ga_w_in[0]
            eo = _eo(GA_DIM)
            qcols = np.concatenate([hd * GA_DIM + eo for hd in range(GA_HEADS)])
            kcols = GA_HEADS * GA_DIM + np.concatenate([hd * GA_DIM + eo for hd in range(GA_KV)])
            wcat = jnp.concatenate([_take_cols(w, qcols), _take_cols(w, kcols), w[:, (GA_HEADS + GA_KV) * GA_DIM:]],
                                   axis=1)
            tq_tab, tk_tab = _rope_tables(seq, GA_DIM, np.arange(LANES) % 64, np.ones(LANES, bool), GA_DIM ** -0.5)
            consts = [wcat.astype(BF16), ga_q_norm[0][eo].reshape(1, LANES), ga_k_norm[0][eo].reshape(1, LANES)]
            q, k, v = _proj_call(geom, functools.partial(_proj1_kernel, nq=8, nk=2, nv=2), h, mod, g8[i],
                                 consts, [*tq_tab, *tk_tab], 8, 2, 2, name="proj_gqa")
            grp = GA_HEADS // GA_KV
            mk = lambda has_lat: functools.partial(_attn_gqa_kernel, has_lat=has_lat, grp=grp, n_ctx_keys=n_ctx,
                                                   n_lat_keys=seq, tk=tk)
            o_lat = _attn_call(geom, mk(True), [], [], q, k, v, has_lat=True, tq=256, q_per=grp, k_per=1, v_per=1,
                               o_per=grp, n_groups=GA_KV, name="attn_gqa")
            o_ctx = _attn_call(geom, mk(False), [], [], q, k, v, has_lat=False, tq=n_ctx, q_per=grp, k_per=1,
                               v_per=1, o_per=grp, n_groups=GA_KV, name="attn_gqa_ctx")
            w_mo = ga_w_out[0]
        elif i == 2:
            lanes = np.arange(LANES)
            is_rope = ((lanes % HALF) >= 32) & ((lanes % HALF) < 48)
            is_nope = (lanes % HALF) < 32
            rope_dim = 2 * ((lanes % HALF) - 32) + (lanes >= HALF)
            nope_dim = (lanes % HALF) + 32 * (lanes >= HALF)
            w = mla_w_in[0]
            kr_cols = np.where(is_rope, MLA_Q_RANK + MLA_KV_RANK + rope_dim, -1)
            w1 = jnp.concatenate([w[:, :MLA_Q_RANK + MLA_KV_RANK], _take_cols(w, kr_cols)], axis=1)
            qd = MLA_NOPE + MLA_ROPE
            q_blk = np.where(is_nope, nope_dim, np.where(is_rope, MLA_NOPE + rope_dim, -1))
            q_cols = np.concatenate([np.where(q_blk >= 0, hd * qd + q_blk, -1) for hd in range(MLA_HEADS)])
            kvd = MLA_NOPE + MLA_V
            k_blk = np.where(is_nope, nope_dim, -1)
            k_cols = np.concatenate([np.where(k_blk >= 0, hd * kvd + k_blk, -1) for hd in range(MLA_HEADS)])
            v_cols = np.concatenate([hd * kvd + MLA_NOPE + np.arange(MLA_V) for hd in range(MLA_HEADS)])
            consts = [w1.astype(BF16), mla_q_norm[0].reshape(1, -1), mla_kv_norm[0].reshape(1, -1),
                      _take_cols(mla_w_uq[0], q_cols).astype(BF16), _take_cols(mla_w_ukv[0], k_cols).astype(BF16),
                      _take_cols(mla_w_ukv[0], v_cols).astype(BF16)]
            aidx = np.where(is_rope, (lanes % HALF) - 32, 0)
            tq_tab, tk_tab = _rope_tables(seq, MLA_ROPE, aidx, is_rope, (MLA_NOPE + MLA_ROPE) ** -0.5)
            q, k, v = _proj_call(geom, _proj2_kernel, h, mod, g8[i], consts, [*tq_tab, *tk_tab],
                                 MLA_HEADS, MLA_HEADS, MLA_HEADS // 2, name="proj_mla")
            n_pairs = 4
            mk = lambda has_lat: functools.partial(_attn_mla_kernel, has_lat=has_lat, n_pairs=n_pairs,
                                                   n_ctx_keys=n_ctx, n_lat_keys=seq, tk=tk)
            o_lat = _attn_call(geom, mk(True), [], [], q, k, v, has_lat=True, tq=1024, q_per=2 * n_pairs,
                               k_per=2 * n_pairs, v_per=n_pairs, o_per=n_pairs,
                               n_groups=MLA_HEADS // (2 * n_pairs), name="attn_mla")
            o_ctx = _attn_call(geom, mk(False), [], [], q, k, v, has_lat=False, tq=n_ctx, q_per=2 * n_pairs,
                               k_per=2 * n_pairs, v_per=n_pairs, o_per=n_pairs,
                               n_groups=MLA_HEADS // (2 * n_pairs), name="attn_mla_ctx")
            w_mo = mla_w_out[0]
        else:
            w = swa_w_in[0]
            hq = SWA_HEADS * SWA_DIM
            qcols = np.concatenate([(2 * p + which) * SWA_DIM + pdim for p in range(SWA_HEADS // 2)])
            kcols = np.concatenate([hq + g * SWA_DIM + pdim for g in range(SWA_KV)])
            vcols = np.concatenate([hq + SWA_KV * SWA_DIM + g * SWA_DIM + (np.arange(LANES) % SWA_DIM)
                                    for g in range(SWA_KV)])
            wcat = jnp.concatenate([_take_cols(w, qcols), _take_cols(w, kcols), _take_cols(w, vcols)], axis=1)
            tq_tab, tk_tab = _rope_tables(seq, SWA_DIM, np.arange(LANES) % 32, np.ones(LANES, bool), SWA_DIM ** -0.5)
            q, k, v = _proj_call(geom, functools.partial(_proj0_kernel, nq=8, nk=2, nv=2), h, mod, g8[i],
                                 [wcat.astype(BF16)], [*tq_tab, *tk_tab], 8, 2, 2, name="proj_swa")
            hpg = SWA_HEADS // SWA_KV
            sink = jnp.broadcast_to((swa_sink[0].astype(F32) * LOG2E).reshape(SWA_KV, hpg, 1), (SWA_KV, hpg, LANES))
            tq = 256
            kern = functools.partial(_attn_swa_kernel, n_pairs=hpg // 2, pairs_per_stage=2, seq=seq,
                                     win=tq + 2 * WINDOW)
            o_lat = _attn_call(geom, kern, [sink], [pl.BlockSpec((None, hpg, LANES), lambda bi, g, qi: (g, 0, 0))],
                               q, k, v, has_lat=True, tq=tq, q_per=hpg // 2, k_per=1, v_per=1, o_per=hpg // 2,
                               n_groups=SWA_KV, name="attn_swa")
            o_ctx = None
            w_mo = swa_w_out[0]

        h = _ffn_call(fgeom, h, mod, g8[i], w_in_b, w_out_b, i, 1, k=6, gi=4,
                      pre=(o_lat, o_ctx, w_mo.astype(BF16)), n_tiles=fgeom.n_lat if last else None,
                      name=f"ffn_b{i}")
    return h.reshape(batch, seq, d)
```

```python
import functools
import math

import numpy as np
import jax
import jax.numpy as jnp
from jax import lax
from jax.experimental import pallas as pl
from jax.experimental.pallas import tpu as pltpu

F32 = jnp.float32
BF16 = jnp.bfloat16

D_MODEL = 1024
D_FF = 2816
N_MOD = 9
NORM_EPS = 1e-6
GRID_W = 64
ROPE_THETA = 10000.0
NEG_INF = -1e30
WINDOW = 128
LOG2E = math.log2(math.e)

LANES = 128
HALF = LANES // 2
ROW_TILE = 512
FFN_HALF = 512
FFN_TILE = 2 * FFN_HALF
FF_CHUNK = 256
MOD_ROWS = 16
VMEM_LIMIT = 52 * 1024 * 1024
FFN_VMEM_LIMIT = 58 * 1024 * 1024

DA_HEADS, DA_QK = 8, 64
GA_HEADS, GA_KV, GA_DIM = 8, 2, 128
MLA_HEADS, MLA_Q_RANK, MLA_KV_RANK, MLA_NOPE, MLA_ROPE, MLA_V = 16, 256, 128, 64, 32, 64
SWA_HEADS, SWA_KV, SWA_DIM = 16, 2, 64


def _cparams(n_axes, vmem_limit=VMEM_LIMIT):
    return pltpu.CompilerParams(dimension_semantics=("arbitrary",) * n_axes,
                                vmem_limit_bytes=vmem_limit)


def _rms(x, g):
    return x * lax.rsqrt(jnp.mean(x * x, axis=-1, keepdims=True) + NORM_EPS) * g


def _mod(mod_ref, k):
    return mod_ref[:, k * D_MODEL:(k + 1) * D_MODEL]


def _modnorm(x, g, mod_ref, k):
    return _rms(x, g) * (1.0 + _mod(mod_ref, k + 1)) + _mod(mod_ref, k)


def _rope(x, c, s):
    return x * c + pltpu.roll(x, HALF, 1) * s


def _rope_blocks(xs, c, s):
    rolled = [pltpu.roll(x, HALF, 1) for x in xs]
    return [x * c + r * s for x, r in zip(xs, rolled)]


def _lane_blocks(x, first, n):
    return [x[:, (first + i) * LANES:(first + i + 1) * LANES] for i in range(n)]


def _ada_kernel(c_ref, w_ref, b_ref, o_ref):
    c = c_ref[...]
    sc = c / (1.0 + jnp.exp(-c))
    o_ref[...] = jnp.dot(sc, w_ref[...], precision=lax.Precision.HIGHEST,
                         preferred_element_type=F32) + b_ref[...]


def _ada_call(cond, ada_w, ada_b):
    depth, d, n = ada_w.shape
    cw = 1024
    return pl.pallas_call(
        _ada_kernel,
        grid=(depth, n // cw),
        in_specs=[pl.BlockSpec((MOD_ROWS, d), lambda l, j: (0, 0)),
                  pl.BlockSpec((None, d, cw), lambda l, j: (l, 0, j)),
                  pl.BlockSpec((None, 1, cw), lambda l, j: (l, 0, j))],
        out_specs=pl.BlockSpec((None, MOD_ROWS, cw), lambda l, j: (l, 0, j)),
        out_shape=jax.ShapeDtypeStruct((depth, MOD_ROWS, n), F32),
        compiler_params=_cparams(2),
        name="adaln",
    )(cond, ada_w, ada_b.reshape(depth, 1, n))


class _Geom:
    def __init__(self, batch, seq, ctx, tile):
        self.batch, self.seq, self.ctx, self.tile = batch, seq, ctx, tile
        self.n_lat_rows = batch * seq
        self.n_rows = batch * (seq + ctx)
        self.n_lat = self.n_lat_rows // tile
        self.n_tiles = self.n_rows // tile
        self.tiles_per_batch = seq // tile

    def mod_row(self, t):
        return jnp.where(t < self.n_lat, t // self.tiles_per_batch, self.batch)

    def rope_blk(self, t):
        return jnp.where(t < self.n_lat, t % self.tiles_per_batch, self.tiles_per_batch)


def _const_spec(shape):
    nd = len(shape)
    return pl.BlockSpec(shape, lambda *_: (0,) * nd, pipeline_mode=pl.Buffered(1))


def _ffn_kernel(*refs, k, gi, n_lat, split_in, has_pre, has_ctx):
    it = iter(refs)
    x_ref = next(it)
    xc_ref = next(it) if split_in else None
    mod_ref, g_ref, win_ref, wout_ref = (next(it) for _ in range(4))
    if has_pre:
        ol_ref = next(it)
        oc_ref = next(it) if has_ctx else None
        wmo_ref = next(it)
    out_ref = next(it)
    is_lat = pl.program_id(0) < n_lat

    def prologue(r0):
        rows = slice(r0, r0 + FFN_HALF)
        x = x_ref[rows, :]
        if split_in:
            x = jnp.where(is_lat, x, xc_ref[rows, :])
        if has_pre:
            nblk = ol_ref.shape[0]
            o = jnp.concatenate([ol_ref[i, rows, :] for i in range(nblk)], axis=1)
            if has_ctx:
                o = jnp.where(is_lat, o, jnp.concatenate([oc_ref[i, rows, :] for i in range(nblk)], axis=1))
            y = jnp.dot(o, wmo_ref[...], preferred_element_type=F32)
            x = x + _mod(mod_ref, 5) * _rms(y, g_ref[3:4, :])
        return x, _modnorm(x, g_ref[gi:gi + 1, :], mod_ref, k).astype(BF16)

    def gate_up(u, c):
        lo = c * FF_CHUNK
        return (jnp.dot(u, win_ref[:, lo:lo + FF_CHUNK], preferred_element_type=F32),
                jnp.dot(u, win_ref[:, D_FF + lo:D_FF + lo + FF_CHUNK], preferred_element_type=F32))

    n_chunks = D_FF // FF_CHUNK
    steps = [(blk, c) for blk in range(FFN_TILE // FFN_HALF) for c in range(n_chunks)]
    xs, us = {}, {}
    xs[0], us[0] = prologue(0)
    nxt = gate_up(us[0], 0)
    xs[1], us[1] = prologue(FFN_HALF)
    acc = None
    for idx, (blk, c) in enumerate(steps):
        a, b = nxt
        if idx + 1 < len(steps):
            nxt = gate_up(us[steps[idx + 1][0]], steps[idx + 1][1])
        act = (a / (1.0 + jnp.exp(-a)) * b).astype(BF16)
        part = jnp.dot(act, wout_ref[c * FF_CHUNK:(c + 1) * FF_CHUNK, :], preferred_element_type=F32)
        acc = part if c == 0 else acc + part
        if c == n_chunks - 1:
            r0 = blk * FFN_HALF
            out_ref[r0:r0 + FFN_HALF, :] = (xs.pop(blk)
                                            + 0.5 * _mod(mod_ref, k + 2) * _rms(acc, g_ref[gi + 1:gi + 2, :]))


def _ffn_call(geom, h, mod, g8, w_in, w_out, layer, half, *, k, gi, pre=None, n_tiles=None, name):
    n_tiles = geom.n_tiles if n_tiles is None else n_tiles
    d = D_MODEL
    w_spec = lambda w: pl.BlockSpec((None, None) + w.shape[2:], lambda t: (layer, half, 0, 0),
                                    pipeline_mode=pl.Buffered(1))
    split_in = isinstance(h, tuple)
    if split_in:
        in_specs = [pl.BlockSpec((FFN_TILE, d), lambda t: (jnp.minimum(t, geom.n_lat - 1), 0)),
                    pl.BlockSpec((FFN_TILE, d), lambda t: (jnp.maximum(t - geom.n_lat, 0), 0))]
        args = list(h)
    else:
        in_specs = [pl.BlockSpec((FFN_TILE, d), lambda t: (t, 0))]
        args = [h]
    in_specs += [pl.BlockSpec((None, 1, N_MOD * d), lambda t: (geom.mod_row(t), 0, 0)),
                 _const_spec(g8.shape), w_spec(w_in), w_spec(w_out)]
    args += [mod, g8, w_in, w_out]
    has_pre = pre is not None
    has_ctx = False
    if has_pre:
        o_lat, o_ctx, w_mo = pre
        nblk = o_lat.shape[0]
        in_specs.append(pl.BlockSpec((nblk, FFN_TILE, LANES),
                                     lambda t: (0, jnp.minimum(t, geom.n_lat - 1), 0)))
        args.append(o_lat)
        has_ctx = o_ctx is not None
        if has_ctx:
            in_specs.append(pl.BlockSpec((nblk, FFN_TILE, LANES),
                                         lambda t: (0, jnp.maximum(t - geom.n_lat, 0), 0)))
            args.append(o_ctx)
        in_specs.append(_const_spec(w_mo.shape))
        args.append(w_mo)
    kern = functools.partial(_ffn_kernel, k=k, gi=gi, n_lat=geom.n_lat, split_in=split_in, has_pre=has_pre,
                             has_ctx=has_ctx)
    return pl.pallas_call(
        kern,
        grid=(n_tiles,),
        in_specs=in_specs,
        out_specs=pl.BlockSpec((FFN_TILE, d), lambda t: (t, 0)),
        out_shape=jax.ShapeDtypeStruct((n_tiles * FFN_TILE, d), F32),
        compiler_params=_cparams(1, FFN_VMEM_LIMIT),
        name=name,
    )(*args)


def _store_blocks(dst_ref, x, n, fn=None):
    for i in range(n):
        blk = x[:, i * LANES:(i + 1) * LANES]
        if fn is not None:
            blk = fn(blk)
        dst_ref[i] = blk.astype(BF16)


def _proj0_kernel(x_ref, mod_ref, g_ref, w_ref, cq_ref, sq_ref, ck_ref, sk_ref, q_ref, k_ref, v_ref, *, nq, nk, nv):
    u = _modnorm(x_ref[...], g_ref[2:3, :], mod_ref, 3).astype(BF16)
    cq, sq, ck, sk = cq_ref[...], sq_ref[...], ck_ref[...], sk_ref[...]
    t = jnp.dot(u, w_ref[...], preferred_element_type=F32)
    for i, blk in enumerate(_rope_blocks(_lane_blocks(t, 0, nq), cq, sq)):
        q_ref[i] = blk.astype(BF16)
    for i, blk in enumerate(_rope_blocks(_lane_blocks(t, nq, nk), ck, sk)):
        k_ref[i] = blk.astype(BF16)
    _store_blocks(v_ref, t[:, (nq + nk) * LANES:], nv)


def _proj1_kernel(x_ref, mod_ref, g_ref, w_ref, qn_ref, kn_ref, cq_ref, sq_ref, ck_ref, sk_ref,
                  q_ref, k_ref, v_ref, *, nq, nk, nv):
    u = _modnorm(x_ref[...], g_ref[2:3, :], mod_ref, 3).astype(BF16)
    cq, sq, ck, sk = cq_ref[...], sq_ref[...], ck_ref[...], sk_ref[...]
    qn, kn = qn_ref[...], kn_ref[...]
    t = jnp.dot(u, w_ref[...], preferred_element_type=F32)
    blocks = _lane_blocks(t, 0, nq + nk)
    inv = [lax.rsqrt(jnp.mean(b * b, axis=-1, keepdims=True) + NORM_EPS) for b in blocks]
    normed = [b * r * (qn if i < nq else kn) for i, (b, r) in enumerate(zip(blocks, inv))]
    for i, blk in enumerate(_rope_blocks(normed[:nq], cq, sq)):
        q_ref[i] = blk.astype(BF16)
    for i, blk in enumerate(_rope_blocks(normed[nq:], ck, sk)):
        k_ref[i] = blk.astype(BF16)
    _store_blocks(v_ref, t[:, (nq + nk) * LANES:], nv)


def _proj2_kernel(x_ref, mod_ref, g_ref, w1_ref, qn_ref, kvn_ref, wuq_ref, wuk_ref, wuv_ref,
                  cq_ref, sq_ref, ck_ref, sk_ref, q_ref, k_ref, v_ref):
    u = _modnorm(x_ref[...], g_ref[2:3, :], mod_ref, 3).astype(BF16)
    cq, sq, ck, sk = cq_ref[...], sq_ref[...], ck_ref[...], sk_ref[...]
    t = jnp.dot(u, w1_ref[...], preferred_element_type=F32)
    cqv = _rms(t[:, :MLA_Q_RANK], qn_ref[...]).astype(BF16)
    ckv = _rms(t[:, MLA_Q_RANK:MLA_Q_RANK + MLA_KV_RANK], kvn_ref[...]).astype(BF16)
    kr = _rope(t[:, MLA_Q_RANK + MLA_KV_RANK:], ck, sk)
    qa = jnp.dot(cqv, wuq_ref[...], preferred_element_type=F32)
    for i, blk in enumerate(_rope_blocks(_lane_blocks(qa, 0, MLA_HEADS), cq, sq)):
        q_ref[i] = blk.astype(BF16)
    ka = jnp.dot(ckv, wuk_ref[...], preferred_element_type=F32)
    _store_blocks(k_ref, ka, MLA_HEADS, lambda b: b + kr)
    va = jnp.dot(ckv, wuv_ref[...], preferred_element_type=F32)
    _store_blocks(v_ref, va, MLA_HEADS // 2)


def _proj_call(geom, kern, h, mod, g8, consts, tables, nq, nk, nv, name):
    d = D_MODEL
    tile = lambda t: (t, 0)
    rope_spec = pl.BlockSpec((ROW_TILE, LANES), lambda t: (geom.rope_blk(t), 0))
    in_specs = ([pl.BlockSpec((ROW_TILE, d), tile),
                 pl.BlockSpec((None, 1, N_MOD * d), lambda t: (geom.mod_row(t), 0, 0)),
                 _const_spec(g8.shape)]
                + [_const_spec(c.shape) for c in consts]
                + [rope_spec] * 4)
    blk = lambda n: pl.BlockSpec((n, ROW_TILE, LANES), lambda t: (0, t, 0))
    shp = lambda n: jax.ShapeDtypeStruct((n, geom.n_rows, LANES), BF16)
    return pl.pallas_call(
        kern,
        grid=(geom.n_tiles,),
        in_specs=in_specs,
        out_specs=[blk(nq), blk(nk), blk(nv)],
        out_shape=[shp(nq), shp(nk), shp(nv)],
        compiler_params=_cparams(1),
        name=name,
    )(h, mod, g8, *consts, *tables)


def _qk(qs, k):
    return lax.dot_general(qs, k, (((1,), (1,)), ((), ())), preferred_element_type=F32)


def _pv_with_sums(p, v):
    v_ext = jnp.concatenate([v, jnp.ones_like(v)], axis=1)
    return jnp.dot(p, v_ext, preferred_element_type=F32)


def _flash_stream(jobs):
    steps = [(j, c) for j, (_, chunks, _) in enumerate(jobs) for c in range(len(chunks))]
    qs_of = {}

    def scores(j, c):
        if j not in qs_of:
            qs_of[j] = jobs[j][0]()
        k_ref, hk, _, _, start, size = jobs[j][1][c]
        return _qk(qs_of[j], k_ref[hk, pl.ds(start, size), :])

    m = acc = None
    s_next = scores(*steps[0])
    for idx, (j, c) in enumerate(steps):
        s = s_next
        if idx + 1 < len(steps):
            s_next = scores(*steps[idx + 1])
        _, _, v_ref, hv, start, size = jobs[j][1][c]
        mc = jnp.max(s, axis=1, keepdims=True)
        m_new = mc if c == 0 else jnp.maximum(m, mc)
        p = jnp.exp2(s - m_new).astype(BF16)
        pv = _pv_with_sums(p, v_ref[hv, pl.ds(start, size), :])
        acc = pv if c == 0 else jnp.exp2(m - m_new) * acc + pv
        m = m_new
        if c == len(jobs[j][1]) - 1:
            qs_of.pop(j)
            jobs[j][2](acc[:, :LANES], acc[:, LANES:])


def _kv_chunks(kc_ref, vc_ref, kl_ref, vl_ref, hk, hv, n_ctx_keys, n_lat_keys, tk):
    chunks = [(kc_ref, hk, vc_ref, hv, 0, n_ctx_keys)]
    if kl_ref is not None:
        chunks += [(kl_ref, hk, vl_ref, hv, j * tk, tk) for j in range(n_lat_keys // tk)]
    return chunks


def _split_refs(refs, n_lead, has_lat):
    lead = refs[:n_lead]
    rest = refs[n_lead:]
    q_ref, kc_ref, vc_ref = rest[:3]
    if has_lat:
        kl_ref, vl_ref, o_ref = rest[3:6]
    else:
        kl_ref = vl_ref = None
        o_ref = rest[3]
    return lead, q_ref, kc_ref, vc_ref, kl_ref, vl_ref, o_ref


def _lane_lo(rows):
    lane = lax.broadcasted_iota(jnp.int32, (rows, LANES), 1)
    return (lane % HALF) < (HALF // 2)


def _attn_diff_kernel(*refs, has_lat, n_heads, n_ctx_keys, n_lat_keys, tk, lam_init):
    (lam_ref, subln_ref), q_ref, kc_ref, vc_ref, kl_ref, vl_ref, o_ref = _split_refs(refs, 2, has_lat)
    tq = q_ref.shape[1]
    lo = _lane_lo(tq)
    lf = lam_ref[...]
    lam = (jnp.exp(jnp.sum(lf[0:1] * lf[1:2], keepdims=True))
           - jnp.exp(jnp.sum(lf[2:3] * lf[3:4], keepdims=True)) + lam_init)
    subln = subln_ref[...]

    def job(h):
        def make_qs():
            q = q_ref[h]
            zero = jnp.zeros_like(q)
            return jnp.concatenate([jnp.where(lo, q, zero), jnp.where(lo, zero, q)], axis=0)

        def done(acc, l):
            o = acc / l
            o = o[:tq] - lam * o[tq:]
            o_ref[h] = (_rms(o, subln) * (1.0 - lam_init)).astype(BF16)

        return make_qs, _kv_chunks(kc_ref, vc_ref, kl_ref, vl_ref, h, h, n_ctx_keys, n_lat_keys, tk), done

    def head_pair(i, carry):
        _flash_stream([job(2 * i), job(2 * i + 1)])
        return carry

    lax.fori_loop(0, n_heads // 2, head_pair, 0)


def _attn_gqa_kernel(*refs, has_lat, grp, n_ctx_keys, n_lat_keys, tk):
    _, q_ref, kc_ref, vc_ref, kl_ref, vl_ref, o_ref = _split_refs(refs, 0, has_lat)
    tq = q_ref.shape[1]

    def done(acc, l):
        o = acc / l
        for j in range(grp):
            o_ref[j] = o[j * tq:(j + 1) * tq].astype(BF16)

    _flash_stream([(lambda: jnp.concatenate([q_ref[j] for j in range(grp)], axis=0),
                    _kv_chunks(kc_ref, vc_ref, kl_ref, vl_ref, 0, 0, n_ctx_keys, n_lat_keys, tk), done)])


def _attn_mla_kernel(*refs, has_lat, n_pairs, n_ctx_keys, n_lat_keys, tk):
    _, q_ref, kc_ref, vc_ref, kl_ref, vl_ref, o_ref = _split_refs(refs, 0, has_lat)
    tq = q_ref.shape[1]
    lane = lax.broadcasted_iota(jnp.int32, (tq, LANES), 1)

    def pair(p, carry):
        even_out = []

        def job(e):
            h = 2 * p + e

            def done(acc, l):
                if e == 0:
                    even_out.append(acc / l)
                else:
                    o_ref[p] = jnp.where(lane < HALF, even_out[0], acc / l).astype(BF16)

            return ((lambda: q_ref[h]),
                    _kv_chunks(kc_ref, vc_ref, kl_ref, vl_ref, h, p, n_ctx_keys, n_lat_keys, tk), done)

        _flash_stream([job(0), job(1)])
        return carry

    lax.fori_loop(0, n_pairs, pair, 0)


def _attn_swa_kernel(sink_ref, q_ref, kc_ref, vc_ref, kl_ref, vl_ref, o_ref, *, n_pairs, pairs_per_stage, seq, win):
    tq = q_ref.shape[1]
    i = pl.program_id(2)
    start = pl.multiple_of(jnp.clip(i * tq - WINDOW, 0, seq - win), WINDOW)
    lo = _lane_lo(tq)
    kpos = start + lax.broadcasted_iota(jnp.int32, (win, tq), 0)
    qpos = i * tq + lax.broadcasted_iota(jnp.int32, (win, tq), 1)
    bias = jnp.where(jnp.abs(qpos - kpos) <= WINDOW, 0.0, NEG_INF)

    zero = jnp.zeros((tq, LANES), BF16)
    n_stage = n_pairs // pairs_per_stage
    heads = 2 * pairs_per_stage
    bias_cols = jnp.concatenate([bias] * heads, axis=1)
    k_ctx, k_win = kc_ref[0], kl_ref[0, pl.ds(start, win), :]

    def values_and_ones(v):
        lane = lax.broadcasted_iota(jnp.int32, v.shape, 1)
        return jnp.where(lane < HALF, v, jnp.ones_like(v))

    v_ctx, v_win = values_and_ones(vc_ref[0]), values_and_ones(vl_ref[0, pl.ds(start, win), :])

    def scores(t):
        q_rows = []
        for j in range(t * pairs_per_stage, (t + 1) * pairs_per_stage):
            q = q_ref[j]
            q_rows += [jnp.where(lo, q, zero), jnp.where(lo, zero, q)]
        qs = jnp.concatenate(q_rows, axis=0)
        return _qk(k_ctx, qs), _qk(k_win, qs) + bias_cols

    def softmax(t, s_ctx, s_band):
        sink = jnp.concatenate([sink_ref[h:h + 1, :] for h in range(t * heads, (t + 1) * heads)
                                for _ in range(tq // LANES)], axis=1)
        m = jnp.maximum(jnp.maximum(jnp.max(s_ctx, axis=0, keepdims=True),
                                    jnp.max(s_band, axis=0, keepdims=True)), sink)
        return jnp.exp2(s_ctx - m).astype(BF16), jnp.exp2(s_band - m).astype(BF16), jnp.exp2(sink - m)

    def pv_t(v, p):
        return lax.dot_general(v, p, (((0,), (0,)), ((), ())), preferred_element_type=F32)

    def finish(t, p_ctx, p_band, p_sink):
        acc = pv_t(v_ctx, p_ctx) + pv_t(v_win, p_band)
        o_t = acc[:HALF] / (acc[HALF:HALF + 1] + p_sink)
        for j in range(pairs_per_stage):
            pair_t = jnp.concatenate([o_t[:, 2 * j * tq:(2 * j + 1) * tq],
                                      o_t[:, (2 * j + 1) * tq:(2 * j + 2) * tq]], axis=0)
            o_ref[t * pairs_per_stage + j] = pair_t.T.astype(BF16)

    s_vals, p_vals = {}, {}
    for t in range(n_stage + 2):
        if t < n_stage:
            s_vals[t] = scores(t)
        if 0 <= t - 1 < n_stage:
            p_vals[t - 1] = softmax(t - 1, *s_vals.pop(t - 1))
        if 0 <= t - 2 < n_stage:
            finish(t - 2, *p_vals.pop(t - 2))


def _attn_call(geom, kern, lead, lead_specs, q, k, v, *, has_lat, tq, q_per, k_per, v_per, o_per, n_groups, name):
    b, s, c = geom.batch, geom.seq, geom.ctx
    ctx_blk0 = geom.n_lat_rows // c
    if has_lat:
        n_q = s // tq
        q_map = lambda bi, g, i: (g, bi * n_q + i, 0)
        o_rows = geom.n_lat_rows
    else:
        assert tq == c
        n_q = 1
        q_map = lambda bi, g, i: (g, ctx_blk0 + bi, 0)
        o_rows = b * c
    o_map = lambda bi, g, i: (g, bi * n_q + i, 0)
    ctx_map = lambda bi, g, i: (g, ctx_blk0 + bi, 0)
    lat_map = lambda bi, g, i: (g, bi, 0)
    in_specs = list(lead_specs) + [pl.BlockSpec((q_per, tq, LANES), q_map),
                                   pl.BlockSpec((k_per, c, LANES), ctx_map),
                                   pl.BlockSpec((v_per, c, LANES), ctx_map)]
    args = list(lead) + [q, k, v]
    if has_lat:
        in_specs += [pl.BlockSpec((k_per, s, LANES), lat_map), pl.BlockSpec((v_per, s, LANES), lat_map)]
        args += [k, v]
    return pl.pallas_call(
        kern,
        grid=(b, n_groups, n_q),
        in_specs=in_specs,
        out_specs=pl.BlockSpec((o_per, tq, LANES), o_map),
        out_shape=jax.ShapeDtypeStruct((o_per * n_groups, o_rows, LANES), BF16),
        compiler_params=_cparams(3),
        name=name,
    )(*args)


def _rope_tables(seq, rot_dim, aidx, active, scale):
    scale = scale * LOG2E
    pos = np.arange(seq)
    row = jnp.asarray(pos // GRID_W, F32)
    col = jnp.asarray(pos % GRID_W, F32)
    n_axis = rot_dim // 4
    freqs = ROPE_THETA ** (-jnp.arange(n_axis, dtype=F32) / n_axis)
    ang = jnp.concatenate([row[:, None] * freqs, col[:, None] * freqs], axis=-1)
    a = ang[:, np.asarray(aidx)]
    act = jnp.asarray(active)[None, :]
    sign = jnp.asarray(np.where(np.arange(LANES) < HALF, -1.0, 1.0), F32)[None, :]
    c = jnp.where(act, jnp.cos(a), 1.0)
    sn = jnp.where(act, jnp.sin(a) * sign, 0.0)
    c = jnp.concatenate([c, jnp.ones((ROW_TILE, LANES), F32)], axis=0)
    sn = jnp.concatenate([sn, jnp.zeros((ROW_TILE, LANES), F32)], axis=0)
    return (c * scale, sn * scale), (c, sn)


def _take_cols(w, idx):
    idx = np.asarray(idx)
    out = jnp.take(w, jnp.asarray(np.maximum(idx, 0)), axis=1)
    if (idx < 0).any():
        out = jnp.where(jnp.asarray(idx >= 0)[None, :], out, 0.0)
    return out


def _eo(d):
    return np.concatenate([np.arange(0, d, 2), np.arange(1, d, 2)])


def _pair_layout():
    ev, od = np.arange(0, 64, 2), np.arange(1, 64, 2)
    which = np.concatenate([np.zeros(32), np.ones(32), np.zeros(32), np.ones(32)]).astype(np.int64)
    dim = np.concatenate([ev, ev, od, od])
    return which, dim


def kernel(x, c, ctx, c_ctx, ada_w, ada_b, norm_g, ffn_w_in, ffn_w_out, da_w_in, da_lambda, da_subln, da_w_out,
           ga_w_in, ga_q_norm, ga_k_norm, ga_w_out, mla_w_in, mla_q_norm, mla_kv_norm, mla_w_uq, mla_w_ukv,
           mla_w_out, swa_w_in, swa_sink, swa_w_out):
    batch, seq, d = x.shape
    n_ctx = ctx.shape[1]
    depth = ada_w.shape[0]
    assert d == D_MODEL and seq % FFN_TILE == 0 and (batch * n_ctx) % FFN_TILE == 0
    assert n_ctx % 256 == 0 and batch < MOD_ROWS and depth == 4
    geom = _Geom(batch, seq, n_ctx, ROW_TILE)
    fgeom = _Geom(batch, seq, n_ctx, FFN_TILE)

    cond = jnp.zeros((MOD_ROWS, d), F32).at[:batch].set(c).at[batch].set(c_ctx)
    mods = _ada_call(cond, ada_w, ada_b).reshape(depth, MOD_ROWS, 1, N_MOD * d)

    h = (x.reshape(batch * seq, d), ctx.reshape(batch * n_ctx, d))
    g8 = jnp.concatenate([norm_g, jnp.ones((depth, 2, d), F32)], axis=1)
    w_in_b = ffn_w_in.astype(BF16)
    w_out_b = ffn_w_out.astype(BF16)

    which, pdim = _pair_layout()
    tk = 512

    for i in range(depth):
        mod = mods[i]
        last = i == depth - 1
        h = _ffn_call(fgeom, h, mod, g8[i], w_in_b, w_out_b, i, 0, k=0, gi=0, name=f"ffn_a{i}")

        if i == 0:
            w = da_w_in[0]
            hq = DA_HEADS * 2 * DA_QK
            qcols = np.concatenate([hd * 128 + which * 64 + pdim for hd in range(DA_HEADS)])
            wcat = jnp.concatenate([_take_cols(w, qcols), _take_cols(w, hq + qcols), w[:, 2 * hq:]], axis=1)
            tq_tab, tk_tab = _rope_tables(seq, DA_QK, np.arange(LANES) % 32, np.ones(LANES, bool), DA_QK ** -0.5)
            q, k, v = _proj_call(geom, functools.partial(_proj0_kernel, nq=8, nk=8, nv=8), h, mod, g8[i],
                                 [wcat.astype(BF16)], [*tq_tab, *tk_tab], 8, 8, 8, name="proj_diff")
            lam_init = 0.8 - 0.6 * math.exp(-0.3 * i)
            lead = [da_lambda[0], da_subln[0].reshape(1, LANES)]
            lead_specs = [pl.BlockSpec((4, DA_QK), lambda bi, g, qi: (0, 0)),
                          pl.BlockSpec((1, LANES), lambda bi, g, qi: (0, 0))]
            hs = 4
            mk = lambda has_lat: functools.partial(_attn_diff_kernel, has_lat=has_lat, n_heads=hs, n_ctx_keys=n_ctx,
                                                   n_lat_keys=seq, tk=tk, lam_init=lam_init)
            o_lat = _attn_call(geom, mk(True), lead, lead_specs, q, k, v, has_lat=True, tq=512, q_per=hs, k_per=hs,
                               v_per=hs, o_per=hs, n_groups=DA_HEADS // hs, name="attn_diff")
            o_ctx = _attn_call(geom, mk(False), lead, lead_specs, q, k, v, has_lat=False, tq=n_ctx, q_per=hs,
                               k_per=hs, v_per=hs, o_per=hs, n_groups=DA_HEADS // hs, name="attn_diff_ctx")
            w_mo = da_w_out[0]
        elif i == 1:
            w = ga_w_in[0]
            eo = _eo(GA_DIM)
            qcols = np.concatenate([hd * GA_DIM + eo for hd in range(GA_HEADS)])
            kcols = GA_HEADS * GA_DIM + np.concatenate([hd * GA_DIM + eo for hd in range(GA_KV)])
            wcat = jnp.concatenate([_take_cols(w, qcols), _take_cols(w, kcols), w[:, (GA_HEADS + GA_KV) * GA_DIM:]],
                                   axis=1)
            tq_tab, tk_tab = _rope_tables(seq, GA_DIM, np.arange(LANES) % 64, np.ones(LANES, bool), GA_DIM ** -0.5)
            consts = [wcat.astype(BF16), ga_q_norm[0][eo].reshape(1, LANES), ga_k_norm[0][eo].reshape(1, LANES)]
            q, k, v = _proj_call(geom, functools.partial(_proj1_kernel, nq=8, nk=2, nv=2), h, mod, g8[i],
                                 consts, [*tq_tab, *tk_tab], 8, 2, 2, name="proj_gqa")
            grp = GA_HEADS // GA_KV
            mk = lambda has_lat: functools.partial(_attn_gqa_kernel, has_lat=has_lat, grp=grp, n_ctx_keys=n_ctx,
                                                   n_lat_keys=seq, tk=tk)
            o_lat = _attn_call(geom, mk(True), [], [], q, k, v, has_lat=True, tq=256, q_per=grp, k_per=1, v_per=1,
                               o_per=grp, n_groups=GA_KV, name="attn_gqa")
            o_ctx = _attn_call(geom, mk(False), [], [], q, k, v, has_lat=False, tq=n_ctx, q_per=grp, k_per=1,
                               v_per=1, o_per=grp, n_groups=GA_KV, name="attn_gqa_ctx")
            w_mo = ga_w_out[0]
        elif i == 2:
            lanes = np.arange(LANES)
            is_rope = ((lanes % HALF) >= 32) & ((lanes % HALF) < 48)
            is_nope = (lanes % HALF) < 32
            rope_dim = 2 * ((lanes % HALF) - 32) + (lanes >= HALF)
            nope_dim = (lanes % HALF) + 32 * (lanes >= HALF)
            w = mla_w_in[0]
            kr_cols = np.where(is_rope, MLA_Q_RANK + MLA_KV_RANK + rope_dim, -1)
            w1 = jnp.concatenate([w[:, :MLA_Q_RANK + MLA_KV_RANK], _take_cols(w, kr_cols)], axis=1)
            qd = MLA_NOPE + MLA_ROPE
            q_blk = np.where(is_nope, nope_dim, np.where(is_rope, MLA_NOPE + rope_dim, -1))
            q_cols = np.concatenate([np.where(q_blk >= 0, hd * qd + q_blk, -1) for hd in range(MLA_HEADS)])
            kvd = MLA_NOPE + MLA_V
            k_blk = np.where(is_nope, nope_dim, -1)
            k_cols = np.concatenate([np.where(k_blk >= 0, hd * kvd + k_blk, -1) for hd in range(MLA_HEADS)])
            v_cols = np.concatenate([hd * kvd + MLA_NOPE + np.arange(MLA_V) for hd in range(MLA_HEADS)])
            consts = [w1.astype(BF16), mla_q_norm[0].reshape(1, -1), mla_kv_norm[0].reshape(1, -1),
                      _take_cols(mla_w_uq[0], q_cols).astype(BF16), _take_cols(mla_w_ukv[0], k_cols).astype(BF16),
                      _take_cols(mla_w_ukv[0], v_cols).astype(BF16)]
            aidx = np.where(is_rope, (lanes % HALF) - 32, 0)
            tq_tab, tk_tab = _rope_tables(seq, MLA_ROPE, aidx, is_rope, (MLA_NOPE + MLA_ROPE) ** -0.5)
            q, k, v = _proj_call(geom, _proj2_kernel, h, mod, g8[i], consts, [*tq_tab, *tk_tab],
                                 MLA_HEADS, MLA_HEADS, MLA_HEADS // 2, name="proj_mla")
            n_pairs = 4
            mk = lambda has_lat: functools.partial(_attn_mla_kernel, has_lat=has_lat, n_pairs=n_pairs,
                                                   n_ctx_keys=n_ctx, n_lat_keys=seq, tk=tk)
            o_lat = _attn_call(geom, mk(True), [], [], q, k, v, has_lat=True, tq=1024, q_per=2 * n_pairs,
                               k_per=2 * n_pairs, v_per=n_pairs, o_per=n_pairs,
                               n_groups=MLA_HEADS // (2 * n_pairs), name="attn_mla")
            o_ctx = _attn_call(geom, mk(False), [], [], q, k, v, has_lat=False, tq=n_ctx, q_per=2 * n_pairs,
                               k_per=2 * n_pairs, v_per=n_pairs, o_per=n_pairs,
                               n_groups=MLA_HEADS // (2 * n_pairs), name="attn_mla_ctx")
            w_mo = mla_w_out[0]
        else:
            w = swa_w_in[0]
            hq = SWA_HEADS * SWA_DIM
            qcols = np.concatenate([(2 * p + which) * SWA_DIM + pdim for p in range(SWA_HEADS // 2)])
            kcols = np.concatenate([hq + g * SWA_DIM + pdim for g in range(SWA_KV)])
            vcols = np.concatenate([hq + SWA_KV * SWA_DIM + g * SWA_DIM + (np.arange(LANES) % SWA_DIM)
                                    for g in range(SWA_KV)])
            wcat = jnp.concatenate([_take_cols(w, qcols), _take_cols(w, kcols), _take_cols(w, vcols)], axis=1)
            tq_tab, tk_tab = _rope_tables(seq, SWA_DIM, np.arange(LANES) % 32, np.ones(LANES, bool), SWA_DIM ** -0.5)
            q, k, v = _proj_call(geom, functools.partial(_proj0_kernel, nq=8, nk=2, nv=2), h, mod, g8[i],
                                 [wcat.astype(BF16)], [*tq_tab, *tk_tab], 8, 2, 2, name="proj_swa")
            hpg = SWA_HEADS // SWA_KV
            sink = jnp.broadcast_to((swa_sink[0].astype(F32) * LOG2E).reshape(SWA_KV, hpg, 1), (SWA_KV, hpg, LANES))
            tq = 256
            kern = functools.partial(_attn_swa_kernel, n_pairs=hpg // 2, pairs_per_stage=2, seq=seq,
                                     win=tq + 2 * WINDOW)
            o_lat = _attn_call(geom, kern, [sink], [pl.BlockSpec((None, hpg, LANES), lambda bi, g, qi: (g, 0, 0))],
                               q, k, v, has_lat=True, tq=tq, q_per=hpg // 2, k_per=1, v_per=1, o_per=hpg // 2,
                               n_groups=SWA_KV, name="attn_swa")
            o_ctx = None
            w_mo = swa_w_out[0]

        h = _ffn_call(fgeom, h, mod, g8[i], w_in_b, w_out_b, i, 1, k=6, gi=4,
                      pre=(o_lat, o_ctx, w_mo.astype(BF16)), n_tiles=fgeom.n_lat if last else None,
                      name=f"ffn_b{i}")
    return h.reshape(batch, seq, d)
```

```python
import functools
import math

import numpy as np
import jax
import jax.numpy as jnp
from jax import lax
from jax.experimental import pallas as pl
from jax.experimental.pallas import tpu as pltpu

F32 = jnp.float32
BF16 = jnp.bfloat16

D_MODEL = 1024
D_FF = 2816
N_MOD = 9
NORM_EPS = 1e-6
GRID_W = 64
ROPE_THETA = 10000.0
NEG_INF = -1e30
WINDOW = 128
LOG2E = math.log2(math.e)

LANES = 128
HALF = LANES // 2
ROW_TILE = 512
FFN_HALF = 512
FFN_TILE = 2 * FFN_HALF
FF_CHUNK = 256
GQA_JOB_ROWS = 256
MOD_ROWS = 16
VMEM_LIMIT = 52 * 1024 * 1024
FFN_VMEM_LIMIT = 58 * 1024 * 1024

DA_HEADS, DA_QK = 8, 64
GA_HEADS, GA_KV, GA_DIM = 8, 2, 128
MLA_HEADS, MLA_Q_RANK, MLA_KV_RANK, MLA_NOPE, MLA_ROPE, MLA_V = 16, 256, 128, 64, 32, 64
SWA_HEADS, SWA_KV, SWA_DIM = 16, 2, 64


def _cparams(n_axes, vmem_limit=VMEM_LIMIT):
    return pltpu.CompilerParams(dimension_semantics=("arbitrary",) * n_axes,
                                vmem_limit_bytes=vmem_limit)


def _rms(x, g):
    return x * lax.rsqrt(jnp.mean(x * x, axis=-1, keepdims=True) + NORM_EPS) * g


def _mod(mod_ref, k):
    return mod_ref[:, k * D_MODEL:(k + 1) * D_MODEL]


def _modnorm(x, g, mod_ref, k):
    return _rms(x, g) * (1.0 + _mod(mod_ref, k + 1)) + _mod(mod_ref, k)


def _rope(x, c, s):
    return x * c + pltpu.roll(x, HALF, 1) * s


def _rope_blocks(xs, c, s):
    rolled = [pltpu.roll(x, HALF, 1) for x in xs]
    return [x * c + r * s for x, r in zip(xs, rolled)]


def _lane_blocks(x, first, n):
    return [x[:, (first + i) * LANES:(first + i + 1) * LANES] for i in range(n)]


def _ada_kernel(c_ref, w_ref, b_ref, o_ref):
    c = c_ref[...]
    sc = c / (1.0 + jnp.exp(-c))
    o_ref[...] = jnp.dot(sc, w_ref[...], precision=lax.Precision.HIGHEST,
                         preferred_element_type=F32) + b_ref[...]


def _ada_call(cond, ada_w, ada_b):
    depth, d, n = ada_w.shape
    cw = 1024
    return pl.pallas_call(
        _ada_kernel,
        grid=(depth, n // cw),
        in_specs=[pl.BlockSpec((MOD_ROWS, d), lambda l, j: (0, 0)),
                  pl.BlockSpec((None, d, cw), lambda l, j: (l, 0, j)),
                  pl.BlockSpec((None, 1, cw), lambda l, j: (l, 0, j))],
        out_specs=pl.BlockSpec((None, MOD_ROWS, cw), lambda l, j: (l, 0, j)),
        out_shape=jax.ShapeDtypeStruct((depth, MOD_ROWS, n), F32),
        compiler_params=_cparams(2),
        name="adaln",
    )(cond, ada_w, ada_b.reshape(depth, 1, n))


class _Geom:
    def __init__(self, batch, seq, ctx, tile):
        self.batch, self.seq, self.ctx, self.tile = batch, seq, ctx, tile
        self.n_lat_rows = batch * seq
        self.n_rows = batch * (seq + ctx)
        self.n_lat = self.n_lat_rows // tile
        self.n_tiles = self.n_rows // tile
        self.tiles_per_batch = seq // tile

    def mod_row(self, t):
        return jnp.where(t < self.n_lat, t // self.tiles_per_batch, self.batch)

    def rope_blk(self, t):
        return jnp.where(t < self.n_lat, t % self.tiles_per_batch, self.tiles_per_batch)


def _const_spec(shape):
    nd = len(shape)
    return pl.BlockSpec(shape, lambda *_: (0,) * nd, pipeline_mode=pl.Buffered(1))


def _ffn_kernel(*refs, k, gi, n_lat, split_in, has_pre, has_ctx):
    it = iter(refs)
    x_ref = next(it)
    xc_ref = next(it) if split_in else None
    mod_ref, g_ref, win_ref, wout_ref = (next(it) for _ in range(4))
    if has_pre:
        ol_ref = next(it)
        oc_ref = next(it) if has_ctx else None
        wmo_ref = next(it)
    out_ref = next(it)
    is_lat = pl.program_id(0) < n_lat

    def prologue(r0):
        rows = slice(r0, r0 + FFN_HALF)
        x = x_ref[rows, :]
        if split_in:
            x = jnp.where(is_lat, x, xc_ref[rows, :])
        if has_pre:
            nblk = ol_ref.shape[0]
            o = jnp.concatenate([ol_ref[i, rows, :] for i in range(nblk)], axis=1)
            if has_ctx:
                o = jnp.where(is_lat, o, jnp.concatenate([oc_ref[i, rows, :] for i in range(nblk)], axis=1))
            y = jnp.dot(o, wmo_ref[...], preferred_element_type=F32)
            x = x + _mod(mod_ref, 5) * _rms(y, g_ref[3:4, :])
        return x, _modnorm(x, g_ref[gi:gi + 1, :], mod_ref, k).astype(BF16)

    def gate_up(u, c):
        lo = c * FF_CHUNK
        return (jnp.dot(u, win_ref[:, lo:lo + FF_CHUNK], preferred_element_type=F32),
                jnp.dot(u, win_ref[:, D_FF + lo:D_FF + lo + FF_CHUNK], preferred_element_type=F32))

    n_chunks = D_FF // FF_CHUNK
    steps = [(blk, c) for blk in range(FFN_TILE // FFN_HALF) for c in range(n_chunks)]
    xs, us = {}, {}
    xs[0], us[0] = prologue(0)
    nxt = gate_up(us[0], 0)
    xs[1], us[1] = prologue(FFN_HALF)
    acc = None
    for idx, (blk, c) in enumerate(steps):
        a, b = nxt
        if idx + 1 < len(steps):
            nxt = gate_up(us[steps[idx + 1][0]], steps[idx + 1][1])
        act = (a / (1.0 + jnp.exp(-a)) * b).astype(BF16)
        part = jnp.dot(act, wout_ref[c * FF_CHUNK:(c + 1) * FF_CHUNK, :], preferred_element_type=F32)
        acc = part if c == 0 else acc + part
        if c == n_chunks - 1:
            r0 = blk * FFN_HALF
            out_ref[r0:r0 + FFN_HALF, :] = (xs.pop(blk)
                                            + 0.5 * _mod(mod_ref, k + 2) * _rms(acc, g_ref[gi + 1:gi + 2, :]))


def _ffn_call(geom, h, mod, g8, w_in, w_out, layer, half, *, k, gi, pre=None, n_tiles=None, name):
    n_tiles = geom.n_tiles if n_tiles is None else n_tiles
    d = D_MODEL
    w_spec = lambda w: pl.BlockSpec((None, None) + w.shape[2:], lambda t: (layer, half, 0, 0),
                                    pipeline_mode=pl.Buffered(1))
    split_in = isinstance(h, tuple)
    if split_in:
        in_specs = [pl.BlockSpec((FFN_TILE, d), lambda t: (jnp.minimum(t, geom.n_lat - 1), 0)),
                    pl.BlockSpec((FFN_TILE, d), lambda t: (jnp.maximum(t - geom.n_lat, 0), 0))]
        args = list(h)
    else:
        in_specs = [pl.BlockSpec((FFN_TILE, d), lambda t: (t, 0))]
        args = [h]
    in_specs += [pl.BlockSpec((None, 1, N_MOD * d), lambda t: (geom.mod_row(t), 0, 0)),
                 _const_spec(g8.shape), w_spec(w_in), w_spec(w_out)]
    args += [mod, g8, w_in, w_out]
    has_pre = pre is not None
    has_ctx = False
    if has_pre:
        o_lat, o_ctx, w_mo = pre
        nblk = o_lat.shape[0]
        in_specs.append(pl.BlockSpec((nblk, FFN_TILE, LANES),
                                     lambda t: (0, jnp.minimum(t, geom.n_lat - 1), 0)))
        args.append(o_lat)
        has_ctx = o_ctx is not None
        if has_ctx:
            in_specs.append(pl.BlockSpec((nblk, FFN_TILE, LANES),
                                         lambda t: (0, jnp.maximum(t - geom.n_lat, 0), 0)))
            args.append(o_ctx)
        in_specs.append(_const_spec(w_mo.shape))
        args.append(w_mo)
    kern = functools.partial(_ffn_kernel, k=k, gi=gi, n_lat=geom.n_lat, split_in=split_in, has_pre=has_pre,
                             has_ctx=has_ctx)
    return pl.pallas_call(
        kern,
        grid=(n_tiles,),
        in_specs=in_specs,
        out_specs=pl.BlockSpec((FFN_TILE, d), lambda t: (t, 0)),
        out_shape=jax.ShapeDtypeStruct((n_tiles * FFN_TILE, d), F32),
        compiler_params=_cparams(1, FFN_VMEM_LIMIT),
        name=name,
    )(*args)


def _store_blocks(dst_ref, x, n, fn=None):
    for i in range(n):
        blk = x[:, i * LANES:(i + 1) * LANES]
        if fn is not None:
            blk = fn(blk)
        dst_ref[i] = blk.astype(BF16)


def _proj0_kernel(x_ref, mod_ref, g_ref, w_ref, cq_ref, sq_ref, ck_ref, sk_ref, q_ref, k_ref, v_ref, *, nq, nk, nv):
    u = _modnorm(x_ref[...], g_ref[2:3, :], mod_ref, 3).astype(BF16)
    cq, sq, ck, sk = cq_ref[...], sq_ref[...], ck_ref[...], sk_ref[...]
    t = jnp.dot(u, w_ref[...], preferred_element_type=F32)
    for i, blk in enumerate(_rope_blocks(_lane_blocks(t, 0, nq), cq, sq)):
        q_ref[i] = blk.astype(BF16)
    for i, blk in enumerate(_rope_blocks(_lane_blocks(t, nq, nk), ck, sk)):
        k_ref[i] = blk.astype(BF16)
    _store_blocks(v_ref, t[:, (nq + nk) * LANES:], nv)


def _proj1_kernel(x_ref, mod_ref, g_ref, w_ref, qn_ref, kn_ref, cq_ref, sq_ref, ck_ref, sk_ref,
                  q_ref, k_ref, v_ref, *, nq, nk, nv):
    u = _modnorm(x_ref[...], g_ref[2:3, :], mod_ref, 3).astype(BF16)
    cq, sq, ck, sk = cq_ref[...], sq_ref[...], ck_ref[...], sk_ref[...]
    qn, kn = qn_ref[...], kn_ref[...]
    t = jnp.dot(u, w_ref[...], preferred_element_type=F32)
    blocks = _lane_blocks(t, 0, nq + nk)
    inv = [lax.rsqrt(jnp.mean(b * b, axis=-1, keepdims=True) + NORM_EPS) for b in blocks]
    normed = [b * r * (qn if i < nq else kn) for i, (b, r) in enumerate(zip(blocks, inv))]
    for i, blk in enumerate(_rope_blocks(normed[:nq], cq, sq)):
        q_ref[i] = blk.astype(BF16)
    for i, blk in enumerate(_rope_blocks(normed[nq:], ck, sk)):
        k_ref[i] = blk.astype(BF16)
    _store_blocks(v_ref, t[:, (nq + nk) * LANES:], nv)


def _proj2_kernel(x_ref, mod_ref, g_ref, w1_ref, qn_ref, kvn_ref, wuq_ref, wuk_ref, wuv_ref,
                  cq_ref, sq_ref, ck_ref, sk_ref, q_ref, k_ref, v_ref):
    u = _modnorm(x_ref[...], g_ref[2:3, :], mod_ref, 3).astype(BF16)
    cq, sq, ck, sk = cq_ref[...], sq_ref[...], ck_ref[...], sk_ref[...]
    t = jnp.dot(u, w1_ref[...], preferred_element_type=F32)
    cqv = _rms(t[:, :MLA_Q_RANK], qn_ref[...]).astype(BF16)
    ckv = _rms(t[:, MLA_Q_RANK:MLA_Q_RANK + MLA_KV_RANK], kvn_ref[...]).astype(BF16)
    kr = _rope(t[:, MLA_Q_RANK + MLA_KV_RANK:], ck, sk)
    qa = jnp.dot(cqv, wuq_ref[...], preferred_element_type=F32)
    for i, blk in enumerate(_rope_blocks(_lane_blocks(qa, 0, MLA_HEADS), cq, sq)):
        q_ref[i] = blk.astype(BF16)
    ka = jnp.dot(ckv, wuk_ref[...], preferred_element_type=F32)
    _store_blocks(k_ref, ka, MLA_HEADS, lambda b: b + kr)
    va = jnp.dot(ckv, wuv_ref[...], preferred_element_type=F32)
    _store_blocks(v_ref, va, MLA_HEADS // 2)


def _proj_call(geom, kern, h, mod, g8, consts, tables, nq, nk, nv, name):
    d = D_MODEL
    tile = lambda t: (t, 0)
    rope_spec = pl.BlockSpec((ROW_TILE, LANES), lambda t: (geom.rope_blk(t), 0))
    in_specs = ([pl.BlockSpec((ROW_TILE, d), tile),
                 pl.BlockSpec((None, 1, N_MOD * d), lambda t: (geom.mod_row(t), 0, 0)),
                 _const_spec(g8.shape)]
                + [_const_spec(c.shape) for c in consts]
                + [rope_spec] * 4)
    blk = lambda n: pl.BlockSpec((n, ROW_TILE, LANES), lambda t: (0, t, 0))
    shp = lambda n: jax.ShapeDtypeStruct((n, geom.n_rows, LANES), BF16)
    return pl.pallas_call(
        kern,
        grid=(geom.n_tiles,),
        in_specs=in_specs,
        out_specs=[blk(nq), blk(nk), blk(nv)],
        out_shape=[shp(nq), shp(nk), shp(nv)],
        compiler_params=_cparams(1),
        name=name,
    )(h, mod, g8, *consts, *tables)


def _qk(qs, k):
    return lax.dot_general(qs, k, (((1,), (1,)), ((), ())), preferred_element_type=F32)


def _pv_with_sums(p, v):
    v_ext = jnp.concatenate([v, jnp.ones_like(v)], axis=1)
    return jnp.dot(p, v_ext, preferred_element_type=F32)


def _flash_stream(jobs):
    steps = [(j, c) for j, (_, chunks, _) in enumerate(jobs) for c in range(len(chunks))]
    qs_of = {}

    def scores(j, c):
        if j not in qs_of:
            qs_of[j] = jobs[j][0]()
        k_ref, hk, _, _, start, size = jobs[j][1][c]
        return _qk(qs_of[j], k_ref[hk, pl.ds(start, size), :])

    m = acc = None
    s_next = scores(*steps[0])
    for idx, (j, c) in enumerate(steps):
        s = s_next
        if idx + 1 < len(steps):
            s_next = scores(*steps[idx + 1])
        _, _, v_ref, hv, start, size = jobs[j][1][c]
        mc = jnp.max(s, axis=1, keepdims=True)
        m_new = mc if c == 0 else jnp.maximum(m, mc)
        p = jnp.exp2(s - m_new).astype(BF16)
        pv = _pv_with_sums(p, v_ref[hv, pl.ds(start, size), :])
        acc = pv if c == 0 else jnp.exp2(m - m_new) * acc + pv
        m = m_new
        if c == len(jobs[j][1]) - 1:
            qs_of.pop(j)
            jobs[j][2](acc[:, :LANES], acc[:, LANES:])


def _kv_chunks(kc_ref, vc_ref, kl_ref, vl_ref, hk, hv, n_ctx_keys, n_lat_keys, tk):
    chunks = [(kc_ref, hk, vc_ref, hv, 0, n_ctx_keys)]
    if kl_ref is not None:
        chunks += [(kl_ref, hk, vl_ref, hv, j * tk, tk) for j in range(n_lat_keys // tk)]
    return chunks


def _split_refs(refs, n_lead, has_lat):
    lead = refs[:n_lead]
    rest = refs[n_lead:]
    q_ref, kc_ref, vc_ref = rest[:3]
    if has_lat:
        kl_ref, vl_ref, o_ref = rest[3:6]
    else:
        kl_ref = vl_ref = None
        o_ref = rest[3]
    return lead, q_ref, kc_ref, vc_ref, kl_ref, vl_ref, o_ref


def _lane_lo(rows):
    lane = lax.broadcasted_iota(jnp.int32, (rows, LANES), 1)
    return (lane % HALF) < (HALF // 2)


def _attn_diff_kernel(*refs, has_lat, n_heads, n_ctx_keys, n_lat_keys, tk, lam_init):
    (lam_ref, subln_ref), q_ref, kc_ref, vc_ref, kl_ref, vl_ref, o_ref = _split_refs(refs, 2, has_lat)
    tq = q_ref.shape[1]
    lo = _lane_lo(tq)
    lf = lam_ref[...]
    lam = (jnp.exp(jnp.sum(lf[0:1] * lf[1:2], keepdims=True))
           - jnp.exp(jnp.sum(lf[2:3] * lf[3:4], keepdims=True)) + lam_init)
    subln = subln_ref[...]

    def job(h):
        def make_qs():
            q = q_ref[h]
            zero = jnp.zeros_like(q)
            return jnp.concatenate([jnp.where(lo, q, zero), jnp.where(lo, zero, q)], axis=0)

        def done(acc, l):
            o = acc / l
            o = o[:tq] - lam * o[tq:]
            o_ref[h] = (_rms(o, subln) * (1.0 - lam_init)).astype(BF16)

        return make_qs, _kv_chunks(kc_ref, vc_ref, kl_ref, vl_ref, h, h, n_ctx_keys, n_lat_keys, tk), done

    def head_pair(i, carry):
        _flash_stream([job(2 * i), job(2 * i + 1)])
        return carry

    lax.fori_loop(0, n_heads // 2, head_pair, 0)


def _attn_gqa_kernel(*refs, has_lat, grp, n_ctx_keys, n_lat_keys, tk):
    _, q_ref, kc_ref, vc_ref, kl_ref, vl_ref, o_ref = _split_refs(refs, 0, has_lat)
    tq = q_ref.shape[1]
    rows = min(tq, GQA_JOB_ROWS)

    def job(r0):
        def done(acc, l):
            o = acc / l
            for j in range(grp):
                o_ref[j, r0:r0 + rows, :] = o[j * rows:(j + 1) * rows].astype(BF16)

        return ((lambda: jnp.concatenate([q_ref[j, r0:r0 + rows, :] for j in range(grp)], axis=0)),
                _kv_chunks(kc_ref, vc_ref, kl_ref, vl_ref, 0, 0, n_ctx_keys, n_lat_keys, tk), done)

    _flash_stream([job(r0) for r0 in range(0, tq, rows)])


def _attn_mla_kernel(*refs, has_lat, n_pairs, n_ctx_keys, n_lat_keys, tk):
    _, q_ref, kc_ref, vc_ref, kl_ref, vl_ref, o_ref = _split_refs(refs, 0, has_lat)
    tq = q_ref.shape[1]
    lane = lax.broadcasted_iota(jnp.int32, (tq, LANES), 1)

    def pair(p, carry):
        even_out = []

        def job(e):
            h = 2 * p + e

            def done(acc, l):
                if e == 0:
                    even_out.append(acc / l)
                else:
                    o_ref[p] = jnp.where(lane < HALF, even_out[0], acc / l).astype(BF16)

            return ((lambda: q_ref[h]),
                    _kv_chunks(kc_ref, vc_ref, kl_ref, vl_ref, h, p, n_ctx_keys, n_lat_keys, tk), done)

        _flash_stream([job(0), job(1)])
        return carry

    lax.fori_loop(0, n_pairs, pair, 0)


def _attn_swa_kernel(sink_ref, q_ref, kc_ref, vc_ref, kl_ref, vl_ref, o_ref, *, n_pairs, pairs_per_stage, seq, win):
    tq = q_ref.shape[1]
    i = pl.program_id(2)
    start = pl.multiple_of(jnp.clip(i * tq - WINDOW, 0, seq - win), WINDOW)
    lo = _lane_lo(tq)
    kpos = start + lax.broadcasted_iota(jnp.int32, (win, tq), 0)
    qpos = i * tq + lax.broadcasted_iota(jnp.int32, (win, tq), 1)
    bias = jnp.where(jnp.abs(qpos - kpos) <= WINDOW, 0.0, NEG_INF)

    zero = jnp.zeros((tq, LANES), BF16)
    n_stage = n_pairs // pairs_per_stage
    heads = 2 * pairs_per_stage
    bias_cols = jnp.concatenate([bias] * heads, axis=1)
    k_ctx, k_win = kc_ref[0], kl_ref[0, pl.ds(start, win), :]

    def values_and_ones(v):
        lane = lax.broadcasted_iota(jnp.int32, v.shape, 1)
        return jnp.where(lane < HALF, v, jnp.ones_like(v))

    v_ctx, v_win = values_and_ones(vc_ref[0]), values_and_ones(vl_ref[0, pl.ds(start, win), :])

    def scores(t):
        q_rows = []
        for j in range(t * pairs_per_stage, (t + 1) * pairs_per_stage):
            q = q_ref[j]
            q_rows += [jnp.where(lo, q, zero), jnp.where(lo, zero, q)]
        qs = jnp.concatenate(q_rows, axis=0)
        return _qk(k_ctx, qs), _qk(k_win, qs) + bias_cols

    def softmax(t, s_ctx, s_band):
        sink = jnp.concatenate([sink_ref[h:h + 1, :] for h in range(t * heads, (t + 1) * heads)
                                for _ in range(tq // LANES)], axis=1)
        m = jnp.maximum(jnp.maximum(jnp.max(s_ctx, axis=0, keepdims=True),
                                    jnp.max(s_band, axis=0, keepdims=True)), sink)
        return jnp.exp2(s_ctx - m).astype(BF16), jnp.exp2(s_band - m).astype(BF16), jnp.exp2(sink - m)

    def pv_t(v, p):
        return lax.dot_general(v, p, (((0,), (0,)), ((), ())), preferred_element_type=F32)

    def finish(t, p_ctx, p_band, p_sink):
        acc = pv_t(v_ctx, p_ctx) + pv_t(v_win, p_band)
        o_t = acc[:HALF] / (acc[HALF:HALF + 1] + p_sink)
        for j in range(pairs_per_stage):
            pair_t = jnp.concatenate([o_t[:, 2 * j * tq:(2 * j + 1) * tq],
                                      o_t[:, (2 * j + 1) * tq:(2 * j + 2) * tq]], axis=0)
            o_ref[t * pairs_per_stage + j] = pair_t.T.astype(BF16)

    s_vals, p_vals = {}, {}
    for t in range(n_stage + 2):
        if t < n_stage:
            s_vals[t] = scores(t)
        if 0 <= t - 1 < n_stage:
            p_vals[t - 1] = softmax(t - 1, *s_vals.pop(t - 1))
        if 0 <= t - 2 < n_stage:
            finish(t - 2, *p_vals.pop(t - 2))


def _attn_call(geom, kern, lead, lead_specs, q, k, v, *, has_lat, tq, q_per, k_per, v_per, o_per, n_groups, name):
    b, s, c = geom.batch, geom.seq, geom.ctx
    ctx_blk0 = geom.n_lat_rows // c
    if has_lat:
        n_q = s // tq
        q_map = lambda bi, g, i: (g, bi * n_q + i, 0)
        o_rows = geom.n_lat_rows
    else:
        assert tq == c
        n_q = 1
        q_map = lambda bi, g, i: (g, ctx_blk0 + bi, 0)
        o_rows = b * c
    o_map = lambda bi, g, i: (g, bi * n_q + i, 0)
    ctx_map = lambda bi, g, i: (g, ctx_blk0 + bi, 0)
    lat_map = lambda bi, g, i: (g, bi, 0)
    in_specs = list(lead_specs) + [pl.BlockSpec((q_per, tq, LANES), q_map),
                                   pl.BlockSpec((k_per, c, LANES), ctx_map),
                                   pl.BlockSpec((v_per, c, LANES), ctx_map)]
    args = list(lead) + [q, k, v]
    if has_lat:
        in_specs += [pl.BlockSpec((k_per, s, LANES), lat_map), pl.BlockSpec((v_per, s, LANES), lat_map)]
        args += [k, v]
    return pl.pallas_call(
        kern,
        grid=(b, n_groups, n_q),
        in_specs=in_specs,
        out_specs=pl.BlockSpec((o_per, tq, LANES), o_map),
        out_shape=jax.ShapeDtypeStruct((o_per * n_groups, o_rows, LANES), BF16),
        compiler_params=_cparams(3),
        name=name,
    )(*args)


def _rope_tables(seq, rot_dim, aidx, active, scale):
    scale = scale * LOG2E
    pos = np.arange(seq)
    row = jnp.asarray(pos // GRID_W, F32)
    col = jnp.asarray(pos % GRID_W, F32)
    n_axis = rot_dim // 4
    freqs = ROPE_THETA ** (-jnp.arange(n_axis, dtype=F32) / n_axis)
    ang = jnp.concatenate([row[:, None] * freqs, col[:, None] * freqs], axis=-1)
    a = ang[:, np.asarray(aidx)]
    act = jnp.asarray(active)[None, :]
    sign = jnp.asarray(np.where(np.arange(LANES) < HALF, -1.0, 1.0), F32)[None, :]
    c = jnp.where(act, jnp.cos(a), 1.0)
    sn = jnp.where(act, jnp.sin(a) * sign, 0.0)
    c = jnp.concatenate([c, jnp.ones((ROW_TILE, LANES), F32)], axis=0)
    sn = jnp.concatenate([sn, jnp.zeros((ROW_TILE, LANES), F32)], axis=0)
    return (c * scale, sn * scale), (c, sn)


def _take_cols(w, idx):
    idx = np.asarray(idx)
    out = jnp.take(w, jnp.asarray(np.maximum(idx, 0)), axis=1)
    if (idx < 0).any():
        out = jnp.where(jnp.asarray(idx >= 0)[None, :], out, 0.0)
    return out


def _eo(d):
    return np.concatenate([np.arange(0, d, 2), np.arange(1, d, 2)])


def _pair_layout():
    ev, od = np.arange(0, 64, 2), np.arange(1, 64, 2)
    which = np.concatenate([np.zeros(32), np.ones(32), np.zeros(32), np.ones(32)]).astype(np.int64)
    dim = np.concatenate([ev, ev, od, od])
    return which, dim


def kernel(x, c, ctx, c_ctx, ada_w, ada_b, norm_g, ffn_w_in, ffn_w_out, da_w_in, da_lambda, da_subln, da_w_out,
           ga_w_in, ga_q_norm, ga_k_norm, ga_w_out, mla_w_in, mla_q_norm, mla_kv_norm, mla_w_uq, mla_w_ukv,
           mla_w_out, swa_w_in, swa_sink, swa_w_out):
    batch, seq, d = x.shape
    n_ctx = ctx.shape[1]
    depth = ada_w.shape[0]
    assert d == D_MODEL and seq % FFN_TILE == 0 and (batch * n_ctx) % FFN_TILE == 0
    assert n_ctx % 256 == 0 and batch < MOD_ROWS and depth == 4
    geom = _Geom(batch, seq, n_ctx, ROW_TILE)
    fgeom = _Geom(batch, seq, n_ctx, FFN_TILE)

    cond = jnp.zeros((MOD_ROWS, d), F32).at[:batch].set(c).at[batch].set(c_ctx)
    mods = _ada_call(cond, ada_w, ada_b).reshape(depth, MOD_ROWS, 1, N_MOD * d)

    h = (x.reshape(batch * seq, d), ctx.reshape(batch * n_ctx, d))
    g8 = jnp.concatenate([norm_g, jnp.ones((depth, 2, d), F32)], axis=1)
    w_in_b = ffn_w_in.astype(BF16)
    w_out_b = ffn_w_out.astype(BF16)

    which, pdim = _pair_layout()
    tk = 512

    for i in range(depth):
        mod = mods[i]
        last = i == depth - 1
        h = _ffn_call(fgeom, h, mod, g8[i], w_in_b, w_out_b, i, 0, k=0, gi=0, name=f"ffn_a{i}")

        if i == 0:
            w = da_w_in[0]
            hq = DA_HEADS * 2 * DA_QK
            qcols = np.concatenate([hd * 128 + which * 64 + pdim for hd in range(DA_HEADS)])
            wcat = jnp.concatenate([_take_cols(w, qcols), _take_cols(w, hq + qcols), w[:, 2 * hq:]], axis=1)
            tq_tab, tk_tab = _rope_tables(seq, DA_QK, np.arange(LANES) % 32, np.ones(LANES, bool), DA_QK ** -0.5)
            q, k, v = _proj_call(geom, functools.partial(_proj0_kernel, nq=8, nk=8, nv=8), h, mod, g8[i],
                                 [wcat.astype(BF16)], [*tq_tab, *tk_tab], 8, 8, 8, name="proj_diff")
            lam_init = 0.8 - 0.6 * math.exp(-0.3 * i)
            lead = [da_lambda[0], da_subln[0].reshape(1, LANES)]
            lead_specs = [pl.BlockSpec((4, DA_QK), lambda bi, g, qi: (0, 0)),
                          pl.BlockSpec((1, LANES), lambda bi, g, qi: (0, 0))]
            hs = 4
            mk = lambda has_lat: functools.partial(_attn_diff_kernel, has_lat=has_lat, n_heads=hs, n_ctx_keys=n_ctx,
                                                   n_lat_keys=seq, tk=tk, lam_init=lam_init)
            o_lat = _attn_call(geom, mk(True), lead, lead_specs, q, k, v, has_lat=True, tq=512, q_per=hs, k_per=hs,
                               v_per=hs, o_per=hs, n_groups=DA_HEADS // hs, name="attn_diff")
            o_ctx = _attn_call(geom, mk(False), lead, lead_specs, q, k, v, has_lat=False, tq=n_ctx, q_per=hs,
                               k_per=hs, v_per=hs, o_per=hs, n_groups=DA_HEADS // hs, name="attn_diff_ctx")
            w_mo = da_w_out[0]
        elif i == 1:
            w = ga_w_in[0]
            eo = _eo(GA_DIM)
            qcols = np.concatenate([hd * GA_DIM + eo for hd in range(GA_HEADS)])
            kcols = GA_HEADS * GA_DIM + np.concatenate([hd * GA_DIM + eo for hd in range(GA_KV)])
            wcat = jnp.concatenate([_take_cols(w, qcols), _take_cols(w, kcols), w[:, (GA_HEADS + GA_KV) * GA_DIM:]],
                                   axis=1)
            tq_tab, tk_tab = _rope_tables(seq, GA_DIM, np.arange(LANES) % 64, np.ones(LANES, bool), GA_DIM ** -0.5)
            consts = [wcat.astype(BF16), ga_q_norm[0][eo].reshape(1, LANES), ga_k_norm[0][eo].reshape(1, LANES)]
            q, k, v = _proj_call(geom, functools.partial(_proj1_kernel, nq=8, nk=2, nv=2), h, mod, g8[i],
                                 consts, [*tq_tab, *tk_tab], 8, 2, 2, name="proj_gqa")
            grp = GA_HEADS // GA_KV
            mk = lambda has_lat: functools.partial(_attn_gqa_kernel, has_lat=has_lat, grp=grp, n_ctx_keys=n_ctx,
                                                   n_lat_keys=seq, tk=tk)
            o_lat = _attn_call(geom, mk(True), [], [], q, k, v, has_lat=True, tq=2 * GQA_JOB_ROWS, q_per=grp,
                               k_per=1, v_per=1, o_per=grp, n_groups=GA_KV, name="attn_gqa")
            o_ctx = _attn_call(geom, mk(False), [], [], q, k, v, has_lat=False, tq=n_ctx, q_per=grp, k_per=1,
                               v_per=1, o_per=grp, n_groups=GA_KV, name="attn_gqa_ctx")
            w_mo = ga_w_out[0]
        elif i == 2:
            lanes = np.arange(LANES)
            is_rope = ((lanes % HALF) >= 32) & ((lanes % HALF) < 48)
            is_nope = (lanes % HALF) < 32
            rope_dim = 2 * ((lanes % HALF) - 32) + (lanes >= HALF)
            nope_dim = (lanes % HALF) + 32 * (lanes >= HALF)
            w = mla_w_in[0]
            kr_cols = np.where(is_rope, MLA_Q_RANK + MLA_KV_RANK + rope_dim, -1)
            w1 = jnp.concatenate([w[:, :MLA_Q_RANK + MLA_KV_RANK], _take_cols(w, kr_cols)], axis=1)
            qd = MLA_NOPE + MLA_ROPE
            q_blk = np.where(is_nope, nope_dim, np.where(is_rope, MLA_NOPE + rope_dim, -1))
            q_cols = np.concatenate([np.where(q_blk >= 0, hd * qd + q_blk, -1) for hd in range(MLA_HEADS)])
            kvd = MLA_NOPE + MLA_V
            k_blk = np.where(is_nope, nope_dim, -1)
            k_cols = np.concatenate([np.where(k_blk >= 0, hd * kvd + k_blk, -1) for hd in range(MLA_HEADS)])
            v_cols = np.concatenate([hd * kvd + MLA_NOPE + np.arange(MLA_V) for hd in range(MLA_HEADS)])
            consts = [w1.astype(BF16), mla_q_norm[0].reshape(1, -1), mla_kv_norm[0].reshape(1, -1),
                      _take_cols(mla_w_uq[0], q_cols).astype(BF16), _take_cols(mla_w_ukv[0], k_cols).astype(BF16),
                      _take_cols(mla_w_ukv[0], v_cols).astype(BF16)]
            aidx = np.where(is_rope, (lanes % HALF) - 32, 0)
            tq_tab, tk_tab = _rope_tables(seq, MLA_ROPE, aidx, is_rope, (MLA_NOPE + MLA_ROPE) ** -0.5)
            q, k, v = _proj_call(geom, _proj2_kernel, h, mod, g8[i], consts, [*tq_tab, *tk_tab],
                                 MLA_HEADS, MLA_HEADS, MLA_HEADS // 2, name="proj_mla")
            n_pairs = 4
            mk = lambda has_lat: functools.partial(_attn_mla_kernel, has_lat=has_lat, n_pairs=n_pairs,
                                                   n_ctx_keys=n_ctx, n_lat_keys=seq, tk=tk)
            o_lat = _attn_call(geom, mk(True), [], [], q, k, v, has_lat=True, tq=1024, q_per=2 * n_pairs,
                               k_per=2 * n_pairs, v_per=n_pairs, o_per=n_pairs,
                               n_groups=MLA_HEADS // (2 * n_pairs), name="attn_mla")
            o_ctx = _attn_call(geom, mk(False), [], [], q, k, v, has_lat=False, tq=n_ctx, q_per=2 * n_pairs,
                               k_per=2 * n_pairs, v_per=n_pairs, o_per=n_pairs,
                               n_groups=MLA_HEADS // (2 * n_pairs), name="attn_mla_ctx")
            w_mo = mla_w_out[0]
        else:
            w = swa_w_in[0]
            hq = SWA_HEADS * SWA_DIM
            qcols = np.concatenate([(2 * p + which) * SWA_DIM + pdim for p in range(SWA_HEADS // 2)])
            kcols = np.concatenate([hq + g * SWA_DIM + pdim for g in range(SWA_KV)])
            vcols = np.concatenate([hq + SWA_KV * SWA_DIM + g * SWA_DIM + (np.arange(LANES) % SWA_DIM)
                                    for g in range(SWA_KV)])
            wcat = jnp.concatenate([_take_cols(w, qcols), _take_cols(w, kcols), _take_cols(w, vcols)], axis=1)
            tq_tab, tk_tab = _rope_tables(seq, SWA_DIM, np.arange(LANES) % 32, np.ones(LANES, bool), SWA_DIM ** -0.5)
            q, k, v = _proj_call(geom, functools.partial(_proj0_kernel, nq=8, nk=2, nv=2), h, mod, g8[i],
                                 [wcat.astype(BF16)], [*tq_tab, *tk_tab], 8, 2, 2, name="proj_swa")
            hpg = SWA_HEADS // SWA_KV
            sink = jnp.broadcast_to((swa_sink[0].astype(F32) * LOG2E).reshape(SWA_KV, hpg, 1), (SWA_KV, hpg, LANES))
            tq = 256
            kern = functools.partial(_attn_swa_kernel, n_pairs=hpg // 2, pairs_per_stage=2, seq=seq,
                                     win=tq + 2 * WINDOW)
            o_lat = _attn_call(geom, kern, [sink], [pl.BlockSpec((None, hpg, LANES), lambda bi, g, qi: (g, 0, 0))],
                               q, k, v, has_lat=True, tq=tq, q_per=hpg // 2, k_per=1, v_per=1, o_per=hpg // 2,
                               n_groups=SWA_KV, name="attn_swa")
            o_ctx = None
            w_mo = swa_w_out[0]

        h = _ffn_call(fgeom, h, mod, g8[i], w_in_b, w_out_b, i, 1, k=6, gi=4,
                      pre=(o_lat, o_ctx, w_mo.astype(BF16)), n_tiles=fgeom.n_lat if last else None,
                      name=f"ffn_b{i}")
    return h.reshape(batch, seq, d)
```

```python
import functools
import math

import numpy as np
import jax
import jax.numpy as jnp
from jax import lax
from jax.experimental import pallas as pl
from jax.experimental.pallas import tpu as pltpu

F32 = jnp.float32
BF16 = jnp.bfloat16

D_MODEL = 1024
D_FF = 2816
N_MOD = 9
NORM_EPS = 1e-6
GRID_W = 64
ROPE_THETA = 10000.0
NEG_INF = -1e30
WINDOW = 128
LOG2E = math.log2(math.e)

LANES = 128
HALF = LANES // 2
V7X_MXU_WIDTH = 256
V7X_VMEM_BYTES = 64 * 1024 * 1024
ROW_TILE = 512
FFN_HALF = 512
FFN_TILE = 2 * FFN_HALF
FF_CHUNK = V7X_MXU_WIDTH
ADA_COL_TILE = 1024
MOD_ROWS = 16
ATTN_KEY_CHUNK = 512
GQA_JOB_ROWS = 256
DIFF_Q_TILE, DIFF_HEADS_PER_STEP = 512, 4
MLA_Q_TILE, MLA_PAIRS_PER_STEP = 1024, 4
SWA_Q_TILE, SWA_PAIRS_PER_STAGE = 256, 2
VMEM_LIMIT = V7X_VMEM_BYTES * 13 // 16
FFN_VMEM_LIMIT = V7X_VMEM_BYTES * 29 // 32

DA_HEADS, DA_QK = 8, 64
GA_HEADS, GA_KV, GA_DIM = 8, 2, 128
MLA_HEADS, MLA_Q_RANK, MLA_KV_RANK, MLA_NOPE, MLA_ROPE, MLA_V = 16, 256, 128, 64, 32, 64
SWA_HEADS, SWA_KV, SWA_DIM = 16, 2, 64


def _cparams(n_axes, vmem_limit=VMEM_LIMIT):
    return pltpu.CompilerParams(dimension_semantics=("arbitrary",) * n_axes,
                                vmem_limit_bytes=vmem_limit)


def _rms(x, g):
    return x * lax.rsqrt(jnp.mean(x * x, axis=-1, keepdims=True) + NORM_EPS) * g


def _mod(mod_ref, k):
    return mod_ref[:, k * D_MODEL:(k + 1) * D_MODEL]


def _modnorm(x, g, mod_ref, k):
    return _rms(x, g) * (1.0 + _mod(mod_ref, k + 1)) + _mod(mod_ref, k)


def _rope(x, c, s):
    return x * c + pltpu.roll(x, HALF, 1) * s


def _rope_blocks(xs, c, s):
    rolled = [pltpu.roll(x, HALF, 1) for x in xs]
    return [x * c + r * s for x, r in zip(xs, rolled)]


def _lane_blocks(x, first, n):
    return [x[:, (first + i) * LANES:(first + i + 1) * LANES] for i in range(n)]


def _ada_kernel(c_ref, w_ref, b_ref, o_ref):
    c = c_ref[...]
    sc = c / (1.0 + jnp.exp(-c))
    o_ref[...] = jnp.dot(sc, w_ref[...], precision=lax.Precision.HIGHEST,
                         preferred_element_type=F32) + b_ref[...]


def _ada_call(cond, ada_w, ada_b):
    depth, d, n = ada_w.shape
    cw = ADA_COL_TILE
    return pl.pallas_call(
        _ada_kernel,
        grid=(depth, n // cw),
        in_specs=[pl.BlockSpec((MOD_ROWS, d), lambda l, j: (0, 0)),
                  pl.BlockSpec((None, d, cw), lambda l, j: (l, 0, j)),
                  pl.BlockSpec((None, 1, cw), lambda l, j: (l, 0, j))],
        out_specs=pl.BlockSpec((None, MOD_ROWS, cw), lambda l, j: (l, 0, j)),
        out_shape=jax.ShapeDtypeStruct((depth, MOD_ROWS, n), F32),
        compiler_params=_cparams(2),
        name="adaln",
    )(cond, ada_w, ada_b.reshape(depth, 1, n))


class _Geom:
    def __init__(self, batch, seq, ctx, tile):
        self.batch, self.seq, self.ctx, self.tile = batch, seq, ctx, tile
        self.n_lat_rows = batch * seq
        self.n_rows = batch * (seq + ctx)
        self.n_lat = self.n_lat_rows // tile
        self.n_tiles = self.n_rows // tile
        self.tiles_per_batch = seq // tile

    def mod_row(self, t):
        return jnp.where(t < self.n_lat, t // self.tiles_per_batch, self.batch)

    def rope_blk(self, t):
        return jnp.where(t < self.n_lat, t % self.tiles_per_batch, self.tiles_per_batch)


def _const_spec(shape):
    nd = len(shape)
    return pl.BlockSpec(shape, lambda *_: (0,) * nd, pipeline_mode=pl.Buffered(1))


def _ffn_kernel(*refs, k, gi, n_lat, split_in, has_pre, has_ctx):
    it = iter(refs)
    x_ref = next(it)
    xc_ref = next(it) if split_in else None
    mod_ref, g_ref, win_ref, wout_ref = (next(it) for _ in range(4))
    if has_pre:
        ol_ref = next(it)
        oc_ref = next(it) if has_ctx else None
        wmo_ref = next(it)
    out_ref = next(it)
    is_lat = pl.program_id(0) < n_lat

    def prologue(r0):
        rows = slice(r0, r0 + FFN_HALF)
        x = x_ref[rows, :]
        if split_in:
            x = jnp.where(is_lat, x, xc_ref[rows, :])
        if has_pre:
            nblk = ol_ref.shape[0]
            o = jnp.concatenate([ol_ref[i, rows, :] for i in range(nblk)], axis=1)
            if has_ctx:
                o = jnp.where(is_lat, o, jnp.concatenate([oc_ref[i, rows, :] for i in range(nblk)], axis=1))
            y = jnp.dot(o, wmo_ref[...], preferred_element_type=F32)
            x = x + _mod(mod_ref, 5) * _rms(y, g_ref[3:4, :])
        return x, _modnorm(x, g_ref[gi:gi + 1, :], mod_ref, k).astype(BF16)

    def gate_up(u, c):
        lo = c * FF_CHUNK
        return (jnp.dot(u, win_ref[:, lo:lo + FF_CHUNK], preferred_element_type=F32),
                jnp.dot(u, win_ref[:, D_FF + lo:D_FF + lo + FF_CHUNK], preferred_element_type=F32))

    n_chunks = D_FF // FF_CHUNK
    steps = [(blk, c) for blk in range(FFN_TILE // FFN_HALF) for c in range(n_chunks)]
    xs, us = {}, {}
    xs[0], us[0] = prologue(0)
    nxt = gate_up(us[0], 0)
    xs[1], us[1] = prologue(FFN_HALF)
    acc = None
    for idx, (blk, c) in enumerate(steps):
        a, b = nxt
        if idx + 1 < len(steps):
            nxt = gate_up(us[steps[idx + 1][0]], steps[idx + 1][1])
        act = (a / (1.0 + jnp.exp(-a)) * b).astype(BF16)
        part = jnp.dot(act, wout_ref[c * FF_CHUNK:(c + 1) * FF_CHUNK, :], preferred_element_type=F32)
        acc = part if c == 0 else acc + part
        if c == n_chunks - 1:
            r0 = blk * FFN_HALF
            out_ref[r0:r0 + FFN_HALF, :] = (xs.pop(blk)
                                            + 0.5 * _mod(mod_ref, k + 2) * _rms(acc, g_ref[gi + 1:gi + 2, :]))


def _ffn_call(geom, h, mod, g8, w_in, w_out, layer, half, *, k, gi, pre=None, n_tiles=None, name):
    n_tiles = geom.n_tiles if n_tiles is None else n_tiles
    d = D_MODEL
    w_spec = lambda w: pl.BlockSpec((None, None) + w.shape[2:], lambda t: (layer, half, 0, 0),
                                    pipeline_mode=pl.Buffered(1))
    split_in = isinstance(h, tuple)
    if split_in:
        in_specs = [pl.BlockSpec((FFN_TILE, d), lambda t: (jnp.minimum(t, geom.n_lat - 1), 0)),
                    pl.BlockSpec((FFN_TILE, d), lambda t: (jnp.maximum(t - geom.n_lat, 0), 0))]
        args = list(h)
    else:
        in_specs = [pl.BlockSpec((FFN_TILE, d), lambda t: (t, 0))]
        args = [h]
    in_specs += [pl.BlockSpec((None, 1, N_MOD * d), lambda t: (geom.mod_row(t), 0, 0)),
                 _const_spec(g8.shape), w_spec(w_in), w_spec(w_out)]
    args += [mod, g8, w_in, w_out]
    has_pre = pre is not None
    has_ctx = False
    if has_pre:
        o_lat, o_ctx, w_mo = pre
        nblk = o_lat.shape[0]
        in_specs.append(pl.BlockSpec((nblk, FFN_TILE, LANES),
                                     lambda t: (0, jnp.minimum(t, geom.n_lat - 1), 0)))
        args.append(o_lat)
        has_ctx = o_ctx is not None
        if has_ctx:
            in_specs.append(pl.BlockSpec((nblk, FFN_TILE, LANES),
                                         lambda t: (0, jnp.maximum(t - geom.n_lat, 0), 0)))
            args.append(o_ctx)
        in_specs.append(_const_spec(w_mo.shape))
        args.append(w_mo)
    kern = functools.partial(_ffn_kernel, k=k, gi=gi, n_lat=geom.n_lat, split_in=split_in, has_pre=has_pre,
                             has_ctx=has_ctx)
    return pl.pallas_call(
        kern,
        grid=(n_tiles,),
        in_specs=in_specs,
        out_specs=pl.BlockSpec((FFN_TILE, d), lambda t: (t, 0)),
        out_shape=jax.ShapeDtypeStruct((n_tiles * FFN_TILE, d), F32),
        compiler_params=_cparams(1, FFN_VMEM_LIMIT),
        name=name,
    )(*args)


def _two_halves(*stages):
    halves = [slice(0, ROW_TILE // 2), slice(ROW_TILE // 2, ROW_TILE)]
    states = [stages[0](rows) for rows in halves]
    for stage in stages[1:]:
        states = [stage(rows, st) for rows, st in zip(halves, states)]


def _proj0_kernel(x_ref, mod_ref, g_ref, w_ref, cq_ref, sq_ref, ck_ref, sk_ref, q_ref, k_ref, v_ref, *, nq, nk, nv):
    def front(rows):
        u = _modnorm(x_ref[rows, :], g_ref[2:3, :], mod_ref, 3).astype(BF16)
        return jnp.dot(u, w_ref[...], preferred_element_type=F32)

    def back(rows, t):
        for i, blk in enumerate(_rope_blocks(_lane_blocks(t, 0, nq), cq_ref[rows, :], sq_ref[rows, :])):
            q_ref[i, rows, :] = blk.astype(BF16)
        for i, blk in enumerate(_rope_blocks(_lane_blocks(t, nq, nk), ck_ref[rows, :], sk_ref[rows, :])):
            k_ref[i, rows, :] = blk.astype(BF16)
        for i, blk in enumerate(_lane_blocks(t, nq + nk, nv)):
            v_ref[i, rows, :] = blk.astype(BF16)

    _two_halves(front, back)


def _proj1_kernel(x_ref, mod_ref, g_ref, w_ref, qn_ref, kn_ref, cq_ref, sq_ref, ck_ref, sk_ref,
                  q_ref, k_ref, v_ref, *, nq, nk, nv):
    qn, kn = qn_ref[...], kn_ref[...]

    def front(rows):
        u = _modnorm(x_ref[rows, :], g_ref[2:3, :], mod_ref, 3).astype(BF16)
        return jnp.dot(u, w_ref[...], preferred_element_type=F32)

    def back(rows, t):
        cq, sq, ck, sk = cq_ref[rows, :], sq_ref[rows, :], ck_ref[rows, :], sk_ref[rows, :]
        blocks = _lane_blocks(t, 0, nq + nk)
        inv = [lax.rsqrt(jnp.mean(b * b, axis=-1, keepdims=True) + NORM_EPS) for b in blocks]
        normed = [b * r * (qn if i < nq else kn) for i, (b, r) in enumerate(zip(blocks, inv))]
        for i, blk in enumerate(_rope_blocks(normed[:nq], cq, sq)):
            q_ref[i, rows, :] = blk.astype(BF16)
        for i, blk in enumerate(_rope_blocks(normed[nq:], ck, sk)):
            k_ref[i, rows, :] = blk.astype(BF16)
        for i, blk in enumerate(_lane_blocks(t, nq + nk, nv)):
            v_ref[i, rows, :] = blk.astype(BF16)

    _two_halves(front, back)


def _proj2_kernel(x_ref, mod_ref, g_ref, w1_ref, qn_ref, kvn_ref, wuq_ref, wuk_ref, wuv_ref,
                  cq_ref, sq_ref, ck_ref, sk_ref, q_ref, k_ref, v_ref):
    u = _modnorm(x_ref[...], g_ref[2:3, :], mod_ref, 3).astype(BF16)
    t = jnp.dot(u, w1_ref[...], preferred_element_type=F32)
    cqv = _rms(t[:, :MLA_Q_RANK], qn_ref[...]).astype(BF16)
    ckv = _rms(t[:, MLA_Q_RANK:MLA_Q_RANK + MLA_KV_RANK], kvn_ref[...]).astype(BF16)
    kr = _rope(t[:, MLA_Q_RANK + MLA_KV_RANK:], ck_ref[...], sk_ref[...])
    qa = jnp.dot(cqv, wuq_ref[...], preferred_element_type=F32)
    for i, blk in enumerate(_rope_blocks(_lane_blocks(qa, 0, MLA_HEADS), cq_ref[...], sq_ref[...])):
        q_ref[i] = blk.astype(BF16)
    ka = jnp.dot(ckv, wuk_ref[...], preferred_element_type=F32)
    for i, blk in enumerate(_lane_blocks(ka, 0, MLA_HEADS)):
        k_ref[i] = (blk + kr).astype(BF16)
    va = jnp.dot(ckv, wuv_ref[...], preferred_element_type=F32)
    for i, blk in enumerate(_lane_blocks(va, 0, MLA_HEADS // 2)):
        v_ref[i] = blk.astype(BF16)


def _proj_call(geom, kern, h, mod, g8, consts, tables, nq, nk, nv, name):
    d = D_MODEL
    tile = lambda t: (t, 0)
    rope_spec = pl.BlockSpec((ROW_TILE, LANES), lambda t: (geom.rope_blk(t), 0))
    in_specs = ([pl.BlockSpec((ROW_TILE, d), tile),
                 pl.BlockSpec((None, 1, N_MOD * d), lambda t: (geom.mod_row(t), 0, 0)),
                 _const_spec(g8.shape)]
                + [_const_spec(c.shape) for c in consts]
                + [rope_spec] * 4)
    blk = lambda n: pl.BlockSpec((n, ROW_TILE, LANES), lambda t: (0, t, 0))
    shp = lambda n: jax.ShapeDtypeStruct((n, geom.n_rows, LANES), BF16)
    return pl.pallas_call(
        kern,
        grid=(geom.n_tiles,),
        in_specs=in_specs,
        out_specs=[blk(nq), blk(nk), blk(nv)],
        out_shape=[shp(nq), shp(nk), shp(nv)],
        compiler_params=_cparams(1),
        name=name,
    )(h, mod, g8, *consts, *tables)


def _qk(qs, k):
    return lax.dot_general(qs, k, (((1,), (1,)), ((), ())), preferred_element_type=F32)


def _pv_with_sums(p, v):
    v_ext = jnp.concatenate([v, jnp.ones_like(v)], axis=1)
    return jnp.dot(p, v_ext, preferred_element_type=F32)


def _flash_stream(jobs):
    steps = [(j, c) for j, (_, chunks, _) in enumerate(jobs) for c in range(len(chunks))]
    qs_of = {}

    def scores(j, c):
        if j not in qs_of:
            qs_of[j] = jobs[j][0]()
        k_ref, hk, _, _, start, size = jobs[j][1][c]
        return _qk(qs_of[j], k_ref[hk, pl.ds(start, size), :])

    m = acc = None
    s_next = scores(*steps[0])
    for idx, (j, c) in enumerate(steps):
        s = s_next
        if idx + 1 < len(steps):
            s_next = scores(*steps[idx + 1])
        _, _, v_ref, hv, start, size = jobs[j][1][c]
        mc = jnp.max(s, axis=1, keepdims=True)
        m_new = mc if c == 0 else jnp.maximum(m, mc)
        p = jnp.exp2(s - m_new).astype(BF16)
        pv = _pv_with_sums(p, v_ref[hv, pl.ds(start, size), :])
        acc = pv if c == 0 else jnp.exp2(m - m_new) * acc + pv
        m = m_new
        if c == len(jobs[j][1]) - 1:
            qs_of.pop(j)
            jobs[j][2](acc[:, :LANES], acc[:, LANES:])


def _kv_chunks(kc_ref, vc_ref, kl_ref, vl_ref, hk, hv, n_ctx_keys, n_lat_keys, tk):
    chunks = [(kc_ref, hk, vc_ref, hv, 0, n_ctx_keys)]
    if kl_ref is not None:
        chunks += [(kl_ref, hk, vl_ref, hv, j * tk, tk) for j in range(n_lat_keys // tk)]
    return chunks


def _split_refs(refs, n_lead, has_lat):
    lead = refs[:n_lead]
    rest = refs[n_lead:]
    q_ref, kc_ref, vc_ref = rest[:3]
    if has_lat:
        kl_ref, vl_ref, o_ref = rest[3:6]
    else:
        kl_ref = vl_ref = None
        o_ref = rest[3]
    return lead, q_ref, kc_ref, vc_ref, kl_ref, vl_ref, o_ref


def _lane_lo(rows):
    lane = lax.broadcasted_iota(jnp.int32, (rows, LANES), 1)
    return (lane % HALF) < (HALF // 2)


def _attn_diff_kernel(*refs, has_lat, n_heads, n_ctx_keys, n_lat_keys, tk, lam_init):
    (lam_ref, subln_ref), q_ref, kc_ref, vc_ref, kl_ref, vl_ref, o_ref = _split_refs(refs, 2, has_lat)
    tq = q_ref.shape[1]
    lo = _lane_lo(tq)
    lf = lam_ref[...]
    lam = (jnp.exp(jnp.sum(lf[0:1] * lf[1:2], keepdims=True))
           - jnp.exp(jnp.sum(lf[2:3] * lf[3:4], keepdims=True)) + lam_init)
    subln = subln_ref[...]

    def job(h):
        def make_qs():
            q = q_ref[h]
            zero = jnp.zeros_like(q)
            return jnp.concatenate([jnp.where(lo, q, zero), jnp.where(lo, zero, q)], axis=0)

        def done(acc, l):
            o = acc / l
            o = o[:tq] - lam * o[tq:]
            o_ref[h] = (_rms(o, subln) * (1.0 - lam_init)).astype(BF16)

        return make_qs, _kv_chunks(kc_ref, vc_ref, kl_ref, vl_ref, h, h, n_ctx_keys, n_lat_keys, tk), done

    def head_pair(i, carry):
        _flash_stream([job(2 * i), job(2 * i + 1)])
        return carry

    lax.fori_loop(0, n_heads // 2, head_pair, 0)


def _attn_gqa_kernel(*refs, has_lat, grp, n_ctx_keys, n_lat_keys, tk):
    _, q_ref, kc_ref, vc_ref, kl_ref, vl_ref, o_ref = _split_refs(refs, 0, has_lat)
    tq = q_ref.shape[1]
    rows = min(tq, GQA_JOB_ROWS)

    def job(r0):
        def done(acc, l):
            o = acc / l
            for j in range(grp):
                o_ref[j, r0:r0 + rows, :] = o[j * rows:(j + 1) * rows].astype(BF16)

        return ((lambda: jnp.concatenate([q_ref[j, r0:r0 + rows, :] for j in range(grp)], axis=0)),
                _kv_chunks(kc_ref, vc_ref, kl_ref, vl_ref, 0, 0, n_ctx_keys, n_lat_keys, tk), done)

    _flash_stream([job(r0) for r0 in range(0, tq, rows)])


def _attn_mla_kernel(*refs, has_lat, n_pairs, n_ctx_keys, n_lat_keys, tk):
    _, q_ref, kc_ref, vc_ref, kl_ref, vl_ref, o_ref = _split_refs(refs, 0, has_lat)
    tq = q_ref.shape[1]
    lane = lax.broadcasted_iota(jnp.int32, (tq, LANES), 1)

    def pair(p, carry):
        even_out = []

        def job(e):
            h = 2 * p + e

            def done(acc, l):
                if e == 0:
                    even_out.append(acc / l)
                else:
                    o_ref[p] = jnp.where(lane < HALF, even_out[0], acc / l).astype(BF16)

            return ((lambda: q_ref[h]),
                    _kv_chunks(kc_ref, vc_ref, kl_ref, vl_ref, h, p, n_ctx_keys, n_lat_keys, tk), done)

        _flash_stream([job(0), job(1)])
        return carry

    lax.fori_loop(0, n_pairs, pair, 0)


def _attn_swa_kernel(sink_ref, q_ref, kc_ref, vc_ref, kl_ref, vl_ref, o_ref, *, n_pairs, pairs_per_stage, seq, win):
    tq = q_ref.shape[1]
    i = pl.program_id(2)
    start = pl.multiple_of(jnp.clip(i * tq - WINDOW, 0, seq - win), WINDOW)
    lo = _lane_lo(tq)
    kpos = start + lax.broadcasted_iota(jnp.int32, (win, tq), 0)
    qpos = i * tq + lax.broadcasted_iota(jnp.int32, (win, tq), 1)
    bias = jnp.where(jnp.abs(qpos - kpos) <= WINDOW, 0.0, NEG_INF)

    zero = jnp.zeros((tq, LANES), BF16)
    n_stage = n_pairs // pairs_per_stage
    heads = 2 * pairs_per_stage
    bias_cols = jnp.concatenate([bias] * heads, axis=1)
    k_ctx, k_win = kc_ref[0], kl_ref[0, pl.ds(start, win), :]

    def values_and_ones(v):
        lane = lax.broadcasted_iota(jnp.int32, v.shape, 1)
        return jnp.where(lane < HALF, v, jnp.ones_like(v))

    v_ctx, v_win = values_and_ones(vc_ref[0]), values_and_ones(vl_ref[0, pl.ds(start, win), :])

    def scores(t):
        q_rows = []
        for j in range(t * pairs_per_stage, (t + 1) * pairs_per_stage):
            q = q_ref[j]
            q_rows += [jnp.where(lo, q, zero), jnp.where(lo, zero, q)]
        qs = jnp.concatenate(q_rows, axis=0)
        return _qk(k_ctx, qs), _qk(k_win, qs) + bias_cols

    def softmax(t, s_ctx, s_band):
        sink = jnp.concatenate([sink_ref[h:h + 1, :] for h in range(t * heads, (t + 1) * heads)
                                for _ in range(tq // LANES)], axis=1)
        m = jnp.maximum(jnp.maximum(jnp.max(s_ctx, axis=0, keepdims=True),
                                    jnp.max(s_band, axis=0, keepdims=True)), sink)
        return jnp.exp2(s_ctx - m).astype(BF16), jnp.exp2(s_band - m).astype(BF16), jnp.exp2(sink - m)

    def pv_t(v, p):
        return lax.dot_general(v, p, (((0,), (0,)), ((), ())), preferred_element_type=F32)

    def finish(t, p_ctx, p_band, p_sink):
        acc = pv_t(v_ctx, p_ctx) + pv_t(v_win, p_band)
        o_t = acc[:HALF] / (acc[HALF:HALF + 1] + p_sink)
        for j in range(pairs_per_stage):
            pair_t = jnp.concatenate([o_t[:, 2 * j * tq:(2 * j + 1) * tq],
                                      o_t[:, (2 * j + 1) * tq:(2 * j + 2) * tq]], axis=0)
            o_ref[t * pairs_per_stage + j] = pair_t.T.astype(BF16)

    s_vals, p_vals = {}, {}
    for t in range(n_stage + 2):
        if t < n_stage:
            s_vals[t] = scores(t)
        if 0 <= t - 1 < n_stage:
            p_vals[t - 1] = softmax(t - 1, *s_vals.pop(t - 1))
        if 0 <= t - 2 < n_stage:
            finish(t - 2, *p_vals.pop(t - 2))


def _attn_call(geom, kern, lead, lead_specs, q, k, v, *, has_lat, tq, q_per, k_per, v_per, o_per, n_groups, name):
    b, s, c = geom.batch, geom.seq, geom.ctx
    ctx_blk0 = geom.n_lat_rows // c
    if has_lat:
        n_q = s // tq
        q_map = lambda bi, g, i: (g, bi * n_q + i, 0)
        o_rows = geom.n_lat_rows
    else:
        assert tq == c
        n_q = 1
        q_map = lambda bi, g, i: (g, ctx_blk0 + bi, 0)
        o_rows = b * c
    o_map = lambda bi, g, i: (g, bi * n_q + i, 0)
    ctx_map = lambda bi, g, i: (g, ctx_blk0 + bi, 0)
    lat_map = lambda bi, g, i: (g, bi, 0)
    in_specs = list(lead_specs) + [pl.BlockSpec((q_per, tq, LANES), q_map),
                                   pl.BlockSpec((k_per, c, LANES), ctx_map),
                                   pl.BlockSpec((v_per, c, LANES), ctx_map)]
    args = list(lead) + [q, k, v]
    if has_lat:
        in_specs += [pl.BlockSpec((k_per, s, LANES), lat_map), pl.BlockSpec((v_per, s, LANES), lat_map)]
        args += [k, v]
    return pl.pallas_call(
        kern,
        grid=(b, n_groups, n_q),
        in_specs=in_specs,
        out_specs=pl.BlockSpec((o_per, tq, LANES), o_map),
        out_shape=jax.ShapeDtypeStruct((o_per * n_groups, o_rows, LANES), BF16),
        compiler_params=_cparams(3),
        name=name,
    )(*args)


def _rope_tables(seq, rot_dim, aidx, active, scale):
    scale = scale * LOG2E
    pos = np.arange(seq)
    row = jnp.asarray(pos // GRID_W, F32)
    col = jnp.asarray(pos % GRID_W, F32)
    n_axis = rot_dim // 4
    freqs = ROPE_THETA ** (-jnp.arange(n_axis, dtype=F32) / n_axis)
    ang = jnp.concatenate([row[:, None] * freqs, col[:, None] * freqs], axis=-1)
    a = ang[:, np.asarray(aidx)]
    act = jnp.asarray(active)[None, :]
    sign = jnp.asarray(np.where(np.arange(LANES) < HALF, -1.0, 1.0), F32)[None, :]
    c = jnp.where(act, jnp.cos(a), 1.0)
    sn = jnp.where(act, jnp.sin(a) * sign, 0.0)
    c = jnp.concatenate([c, jnp.ones((ROW_TILE, LANES), F32)], axis=0)
    sn = jnp.concatenate([sn, jnp.zeros((ROW_TILE, LANES), F32)], axis=0)
    return (c * scale, sn * scale), (c, sn)


def _take_cols(w, idx):
    idx = np.asarray(idx)
    out = jnp.take(w, jnp.asarray(np.maximum(idx, 0)), axis=1)
    if (idx < 0).any():
        out = jnp.where(jnp.asarray(idx >= 0)[None, :], out, 0.0)
    return out


def _eo(d):
    return np.concatenate([np.arange(0, d, 2), np.arange(1, d, 2)])


def _pair_layout():
    ev, od = np.arange(0, 64, 2), np.arange(1, 64, 2)
    which = np.concatenate([np.zeros(32), np.ones(32), np.zeros(32), np.ones(32)]).astype(np.int64)
    dim = np.concatenate([ev, ev, od, od])
    return which, dim


def kernel(x, c, ctx, c_ctx, ada_w, ada_b, norm_g, ffn_w_in, ffn_w_out, da_w_in, da_lambda, da_subln, da_w_out,
           ga_w_in, ga_q_norm, ga_k_norm, ga_w_out, mla_w_in, mla_q_norm, mla_kv_norm, mla_w_uq, mla_w_ukv,
           mla_w_out, swa_w_in, swa_sink, swa_w_out):
    batch, seq, d = x.shape
    n_ctx = ctx.shape[1]
    depth = ada_w.shape[0]
    assert d == D_MODEL and seq % FFN_TILE == 0 and (batch * n_ctx) % FFN_TILE == 0
    assert n_ctx % V7X_MXU_WIDTH == 0 and batch < MOD_ROWS and depth == 4
    assert all(seq % t == 0 for t in (ATTN_KEY_CHUNK, DIFF_Q_TILE, MLA_Q_TILE, 2 * GQA_JOB_ROWS, SWA_Q_TILE))
    geom = _Geom(batch, seq, n_ctx, ROW_TILE)
    fgeom = _Geom(batch, seq, n_ctx, FFN_TILE)

    cond = jnp.zeros((MOD_ROWS, d), F32).at[:batch].set(c).at[batch].set(c_ctx)
    mods = _ada_call(cond, ada_w, ada_b).reshape(depth, MOD_ROWS, 1, N_MOD * d)

    h = (x.reshape(batch * seq, d), ctx.reshape(batch * n_ctx, d))
    g8 = jnp.concatenate([norm_g, jnp.ones((depth, 2, d), F32)], axis=1)
    w_in_b = ffn_w_in.astype(BF16)
    w_out_b = ffn_w_out.astype(BF16)

    which, pdim = _pair_layout()
    tk = ATTN_KEY_CHUNK

    for i in range(depth):
        mod = mods[i]
        last = i == depth - 1
        h = _ffn_call(fgeom, h, mod, g8[i], w_in_b, w_out_b, i, 0, k=0, gi=0, name=f"ffn_a{i}")

        if i == 0:
            w = da_w_in[0]
            hq = DA_HEADS * 2 * DA_QK
            qcols = np.concatenate([hd * 128 + which * 64 + pdim for hd in range(DA_HEADS)])
            wcat = jnp.concatenate([_take_cols(w, qcols), _take_cols(w, hq + qcols), w[:, 2 * hq:]], axis=1)
            tq_tab, tk_tab = _rope_tables(seq, DA_QK, np.arange(LANES) % 32, np.ones(LANES, bool), DA_QK ** -0.5)
            q, k, v = _proj_call(geom, functools.partial(_proj0_kernel, nq=8, nk=8, nv=8), h, mod, g8[i],
                                 [wcat.astype(BF16)], [*tq_tab, *tk_tab], 8, 8, 8, name="proj_diff")
            lam_init = 0.8 - 0.6 * math.exp(-0.3 * i)
            lead = [da_lambda[0], da_subln[0].reshape(1, LANES)]
            lead_specs = [pl.BlockSpec((4, DA_QK), lambda bi, g, qi: (0, 0)),
                          pl.BlockSpec((1, LANES), lambda bi, g, qi: (0, 0))]
            hs = DIFF_HEADS_PER_STEP
            mk = lambda has_lat: functools.partial(_attn_diff_kernel, has_lat=has_lat, n_heads=hs, n_ctx_keys=n_ctx,
                                                   n_lat_keys=seq, tk=tk, lam_init=lam_init)
            o_lat = _attn_call(geom, mk(True), lead, lead_specs, q, k, v, has_lat=True, tq=DIFF_Q_TILE, q_per=hs, k_per=hs,
                               v_per=hs, o_per=hs, n_groups=DA_HEADS // hs, name="attn_diff")
            o_ctx = _attn_call(geom, mk(False), lead, lead_specs, q, k, v, has_lat=False, tq=n_ctx, q_per=hs,
                               k_per=hs, v_per=hs, o_per=hs, n_groups=DA_HEADS // hs, name="attn_diff_ctx")
            w_mo = da_w_out[0]
        elif i == 1:
            w = ga_w_in[0]
            eo = _eo(GA_DIM)
            qcols = np.concatenate([hd * GA_DIM + eo for hd in range(GA_HEADS)])
            kcols = GA_HEADS * GA_DIM + np.concatenate([hd * GA_DIM + eo for hd in range(GA_KV)])
            wcat = jnp.concatenate([_take_cols(w, qcols), _take_cols(w, kcols), w[:, (GA_HEADS + GA_KV) * GA_DIM:]],
                                   axis=1)
            tq_tab, tk_tab = _rope_tables(seq, GA_DIM, np.arange(LANES) % 64, np.ones(LANES, bool), GA_DIM ** -0.5)
            consts = [wcat.astype(BF16), ga_q_norm[0][eo].reshape(1, LANES), ga_k_norm[0][eo].reshape(1, LANES)]
            q, k, v = _proj_call(geom, functools.partial(_proj1_kernel, nq=8, nk=2, nv=2), h, mod, g8[i],
                                 consts, [*tq_tab, *tk_tab], 8, 2, 2, name="proj_gqa")
            grp = GA_HEADS // GA_KV
            mk = lambda has_lat: functools.partial(_attn_gqa_kernel, has_lat=has_lat, grp=grp, n_ctx_keys=n_ctx,
                                                   n_lat_keys=seq, tk=tk)
            o_lat = _attn_call(geom, mk(True), [], [], q, k, v, has_lat=True, tq=2 * GQA_JOB_ROWS, q_per=grp,
                               k_per=1, v_per=1, o_per=grp, n_groups=GA_KV, name="attn_gqa")
            o_ctx = _attn_call(geom, mk(False), [], [], q, k, v, has_lat=False, tq=n_ctx, q_per=grp, k_per=1,
                               v_per=1, o_per=grp, n_groups=GA_KV, name="attn_gqa_ctx")
            w_mo = ga_w_out[0]
        elif i == 2:
            lanes = np.arange(LANES)
            is_rope = ((lanes % HALF) >= 32) & ((lanes % HALF) < 48)
            is_nope = (lanes % HALF) < 32
            rope_dim = 2 * ((lanes % HALF) - 32) + (lanes >= HALF)
            nope_dim = (lanes % HALF) + 32 * (lanes >= HALF)
            w = mla_w_in[0]
            kr_cols = np.where(is_rope, MLA_Q_RANK + MLA_KV_RANK + rope_dim, -1)
            w1 = jnp.concatenate([w[:, :MLA_Q_RANK + MLA_KV_RANK], _take_cols(w, kr_cols)], axis=1)
            qd = MLA_NOPE + MLA_ROPE
            q_blk = np.where(is_nope, nope_dim, np.where(is_rope, MLA_NOPE + rope_dim, -1))
            q_cols = np.concatenate([np.where(q_blk >= 0, hd * qd + q_blk, -1) for hd in range(MLA_HEADS)])
            kvd = MLA_NOPE + MLA_V
            k_blk = np.where(is_nope, nope_dim, -1)
            k_cols = np.concatenate([np.where(k_blk >= 0, hd * kvd + k_blk, -1) for hd in range(MLA_HEADS)])
            v_cols = np.concatenate([hd * kvd + MLA_NOPE + np.arange(MLA_V) for hd in range(MLA_HEADS)])
            consts = [w1.astype(BF16), mla_q_norm[0].reshape(1, -1), mla_kv_norm[0].reshape(1, -1),
                      _take_cols(mla_w_uq[0], q_cols).astype(BF16), _take_cols(mla_w_ukv[0], k_cols).astype(BF16),
                      _take_cols(mla_w_ukv[0], v_cols).astype(BF16)]
            aidx = np.where(is_rope, (lanes % HALF) - 32, 0)
            tq_tab, tk_tab = _rope_tables(seq, MLA_ROPE, aidx, is_rope, (MLA_NOPE + MLA_ROPE) ** -0.5)
            q, k, v = _proj_call(geom, _proj2_kernel, h, mod, g8[i], consts, [*tq_tab, *tk_tab],
                                 MLA_HEADS, MLA_HEADS, MLA_HEADS // 2, name="proj_mla")
            n_pairs = MLA_PAIRS_PER_STEP
            mk = lambda has_lat: functools.partial(_attn_mla_kernel, has_lat=has_lat, n_pairs=n_pairs,
                                                   n_ctx_keys=n_ctx, n_lat_keys=seq, tk=tk)
            o_lat = _attn_call(geom, mk(True), [], [], q, k, v, has_lat=True, tq=MLA_Q_TILE, q_per=2 * n_pairs,
                               k_per=2 * n_pairs, v_per=n_pairs, o_per=n_pairs,
                               n_groups=MLA_HEADS // (2 * n_pairs), name="attn_mla")
            o_ctx = _attn_call(geom, mk(False), [], [], q, k, v, has_lat=False, tq=n_ctx, q_per=2 * n_pairs,
                               k_per=2 * n_pairs, v_per=n_pairs, o_per=n_pairs,
                               n_groups=MLA_HEADS // (2 * n_pairs), name="attn_mla_ctx")
            w_mo = mla_w_out[0]
        else:
            w = swa_w_in[0]
            hq = SWA_HEADS * SWA_DIM
            qcols = np.concatenate([(2 * p + which) * SWA_DIM + pdim for p in range(SWA_HEADS // 2)])
            kcols = np.concatenate([hq + g * SWA_DIM + pdim for g in range(SWA_KV)])
            vcols = np.concatenate([hq + SWA_KV * SWA_DIM + g * SWA_DIM + (np.arange(LANES) % SWA_DIM)
                                    for g in range(SWA_KV)])
            wcat = jnp.concatenate([_take_cols(w, qcols), _take_cols(w, kcols), _take_cols(w, vcols)], axis=1)
            tq_tab, tk_tab = _rope_tables(seq, SWA_DIM, np.arange(LANES) % 32, np.ones(LANES, bool), SWA_DIM ** -0.5)
            q, k, v = _proj_call(geom, functools.partial(_proj0_kernel, nq=8, nk=2, nv=2), h, mod, g8[i],
                                 [wcat.astype(BF16)], [*tq_tab, *tk_tab], 8, 2, 2, name="proj_swa")
            hpg = SWA_HEADS // SWA_KV
            sink = jnp.broadcast_to((swa_sink[0].astype(F32) * LOG2E).reshape(SWA_KV, hpg, 1), (SWA_KV, hpg, LANES))
            tq = SWA_Q_TILE
            kern = functools.partial(_attn_swa_kernel, n_pairs=hpg // 2, pairs_per_stage=SWA_PAIRS_PER_STAGE, seq=seq,
                                     win=tq + 2 * WINDOW)
            o_lat = _attn_call(geom, kern, [sink], [pl.BlockSpec((None, hpg, LANES), lambda bi, g, qi: (g, 0, 0))],
                               q, k, v, has_lat=True, tq=tq, q_per=hpg // 2, k_per=1, v_per=1, o_per=hpg // 2,
                               n_groups=SWA_KV, name="attn_swa")
            o_ctx = None
            w_mo = swa_w_out[0]

        h = _ffn_call(fgeom, h, mod, g8[i], w_in_b, w_out_b, i, 1, k=6, gi=4,
                      pre=(o_lat, o_ctx, w_mo.astype(BF16)), n_tiles=fgeom.n_lat if last else None,
                      name=f"ffn_b{i}")
    return h.reshape(batch, seq, d)
```

```python
import functools
import math

import numpy as np
import jax
import jax.numpy as jnp
from jax import lax
from jax.experimental import pallas as pl
from jax.experimental.pallas import tpu as pltpu

F32 = jnp.float32
BF16 = jnp.bfloat16

D_MODEL = 1024
D_FF = 2816
N_MOD = 9
NORM_EPS = 1e-6
GRID_W = 64
ROPE_THETA = 10000.0
NEG_INF = -1e30
WINDOW = 128
LOG2E = math.log2(math.e)

LANES = 128
HALF = LANES // 2
V7X_MXU_WIDTH = 256
V7X_VMEM_BYTES = 64 * 1024 * 1024
ROW_TILE = 512
FFN_HALF = 512
FFN_TILE = 2 * FFN_HALF
FF_CHUNK = V7X_MXU_WIDTH
ADA_COL_TILE = 1024
MOD_ROWS = 16
ATTN_KEY_CHUNK = 512
GQA_JOB_ROWS = 256
DIFF_Q_TILE, DIFF_HEADS_PER_STEP = 512, 4
MLA_Q_TILE, MLA_PAIRS_PER_STEP = 1024, 4
SWA_Q_TILE, SWA_PAIRS_PER_STAGE, SWA_SUBTILES = 256, 2, 4
VMEM_LIMIT = V7X_VMEM_BYTES * 13 // 16
FFN_VMEM_LIMIT = V7X_VMEM_BYTES * 29 // 32

DA_HEADS, DA_QK = 8, 64
GA_HEADS, GA_KV, GA_DIM = 8, 2, 128
MLA_HEADS, MLA_Q_RANK, MLA_KV_RANK, MLA_NOPE, MLA_ROPE, MLA_V = 16, 256, 128, 64, 32, 64
SWA_HEADS, SWA_KV, SWA_DIM = 16, 2, 64


def _cparams(n_axes, vmem_limit=VMEM_LIMIT):
    return pltpu.CompilerParams(dimension_semantics=("arbitrary",) * n_axes,
                                vmem_limit_bytes=vmem_limit)


def _rms(x, g):
    return x * lax.rsqrt(jnp.mean(x * x, axis=-1, keepdims=True) + NORM_EPS) * g


def _mod(mod_ref, k):
    return mod_ref[:, k * D_MODEL:(k + 1) * D_MODEL]


def _modnorm(x, g, mod_ref, k):
    return _rms(x, g) * (1.0 + _mod(mod_ref, k + 1)) + _mod(mod_ref, k)


def _rope(x, c, s):
    return x * c + pltpu.roll(x, HALF, 1) * s


def _rope_blocks(xs, c, s):
    rolled = [pltpu.roll(x, HALF, 1) for x in xs]
    return [x * c + r * s for x, r in zip(xs, rolled)]


def _lane_blocks(x, first, n):
    return [x[:, (first + i) * LANES:(first + i + 1) * LANES] for i in range(n)]


def _ada_kernel(c_ref, w_ref, b_ref, o_ref):
    c = c_ref[...]
    sc = c / (1.0 + jnp.exp(-c))
    o_ref[...] = jnp.dot(sc, w_ref[...], precision=lax.Precision.HIGHEST,
                         preferred_element_type=F32) + b_ref[...]


def _ada_call(cond, ada_w, ada_b):
    depth, d, n = ada_w.shape
    cw = ADA_COL_TILE
    return pl.pallas_call(
        _ada_kernel,
        grid=(depth, n // cw),
        in_specs=[pl.BlockSpec((MOD_ROWS, d), lambda l, j: (0, 0)),
                  pl.BlockSpec((None, d, cw), lambda l, j: (l, 0, j)),
                  pl.BlockSpec((None, 1, cw), lambda l, j: (l, 0, j))],
        out_specs=pl.BlockSpec((None, MOD_ROWS, cw), lambda l, j: (l, 0, j)),
        out_shape=jax.ShapeDtypeStruct((depth, MOD_ROWS, n), F32),
        compiler_params=_cparams(2),
        name="adaln",
    )(cond, ada_w, ada_b.reshape(depth, 1, n))


class _Geom:
    def __init__(self, batch, seq, ctx, tile):
        self.batch, self.seq, self.ctx, self.tile = batch, seq, ctx, tile
        self.n_lat_rows = batch * seq
        self.n_rows = batch * (seq + ctx)
        self.n_lat = self.n_lat_rows // tile
        self.n_tiles = self.n_rows // tile
        self.tiles_per_batch = seq // tile

    def mod_row(self, t):
        return jnp.where(t < self.n_lat, t // self.tiles_per_batch, self.batch)

    def rope_blk(self, t):
        return jnp.where(t < self.n_lat, t % self.tiles_per_batch, self.tiles_per_batch)


def _const_spec(shape):
    nd = len(shape)
    return pl.BlockSpec(shape, lambda *_: (0,) * nd, pipeline_mode=pl.Buffered(1))


def _ffn_kernel(*refs, k, gi, n_lat, split_in, has_pre, has_ctx):
    it = iter(refs)
    x_ref = next(it)
    xc_ref = next(it) if split_in else None
    mod_ref, g_ref, win_ref, wout_ref = (next(it) for _ in range(4))
    if has_pre:
        ol_ref = next(it)
        oc_ref = next(it) if has_ctx else None
        wmo_ref = next(it)
    out_ref = next(it)
    is_lat = pl.program_id(0) < n_lat

    def prologue(r0):
        rows = slice(r0, r0 + FFN_HALF)
        x = x_ref[rows, :]
        if split_in:
            x = jnp.where(is_lat, x, xc_ref[rows, :])
        if has_pre:
            nblk = ol_ref.shape[0]
            o = jnp.concatenate([ol_ref[i, rows, :] for i in range(nblk)], axis=1)
            if has_ctx:
                o = jnp.where(is_lat, o, jnp.concatenate([oc_ref[i, rows, :] for i in range(nblk)], axis=1))
            y = jnp.dot(o, wmo_ref[...], preferred_element_type=F32)
            x = x + _mod(mod_ref, 5) * _rms(y, g_ref[3:4, :])
        return x, _modnorm(x, g_ref[gi:gi + 1, :], mod_ref, k).astype(BF16)

    def gate_up(u, c):
        lo = c * FF_CHUNK
        return (jnp.dot(u, win_ref[:, lo:lo + FF_CHUNK], preferred_element_type=F32),
                jnp.dot(u, win_ref[:, D_FF + lo:D_FF + lo + FF_CHUNK], preferred_element_type=F32))

    n_chunks = D_FF // FF_CHUNK
    steps = [(blk, c) for blk in range(FFN_TILE // FFN_HALF) for c in range(n_chunks)]
    xs, us = {}, {}
    xs[0], us[0] = prologue(0)
    nxt = gate_up(us[0], 0)
    xs[1], us[1] = prologue(FFN_HALF)
    acc = None
    for idx, (blk, c) in enumerate(steps):
        a, b = nxt
        if idx + 1 < len(steps):
            nxt = gate_up(us[steps[idx + 1][0]], steps[idx + 1][1])
        act = (a / (1.0 + jnp.exp(-a)) * b).astype(BF16)
        part = jnp.dot(act, wout_ref[c * FF_CHUNK:(c + 1) * FF_CHUNK, :], preferred_element_type=F32)
        acc = part if c == 0 else acc + part
        if c == n_chunks - 1:
            r0 = blk * FFN_HALF
            out_ref[r0:r0 + FFN_HALF, :] = (xs.pop(blk)
                                            + 0.5 * _mod(mod_ref, k + 2) * _rms(acc, g_ref[gi + 1:gi + 2, :]))


def _ffn_call(geom, h, mod, g8, w_in, w_out, layer, half, *, k, gi, pre=None, n_tiles=None, name):
    n_tiles = geom.n_tiles if n_tiles is None else n_tiles
    d = D_MODEL
    w_spec = lambda w: pl.BlockSpec((None, None) + w.shape[2:], lambda t: (layer, half, 0, 0),
                                    pipeline_mode=pl.Buffered(1))
    split_in = isinstance(h, tuple)
    if split_in:
        in_specs = [pl.BlockSpec((FFN_TILE, d), lambda t: (jnp.minimum(t, geom.n_lat - 1), 0)),
                    pl.BlockSpec((FFN_TILE, d), lambda t: (jnp.maximum(t - geom.n_lat, 0), 0))]
        args = list(h)
    else:
        in_specs = [pl.BlockSpec((FFN_TILE, d), lambda t: (t, 0))]
        args = [h]
    in_specs += [pl.BlockSpec((None, 1, N_MOD * d), lambda t: (geom.mod_row(t), 0, 0)),
                 _const_spec(g8.shape), w_spec(w_in), w_spec(w_out)]
    args += [mod, g8, w_in, w_out]
    has_pre = pre is not None
    has_ctx = False
    if has_pre:
        o_lat, o_ctx, w_mo = pre
        nblk = o_lat.shape[0]
        in_specs.append(pl.BlockSpec((nblk, FFN_TILE, LANES),
                                     lambda t: (0, jnp.minimum(t, geom.n_lat - 1), 0)))
        args.append(o_lat)
        has_ctx = o_ctx is not None
        if has_ctx:
            in_specs.append(pl.BlockSpec((nblk, FFN_TILE, LANES),
                                         lambda t: (0, jnp.maximum(t - geom.n_lat, 0), 0)))
            args.append(o_ctx)
        in_specs.append(_const_spec(w_mo.shape))
        args.append(w_mo)
    kern = functools.partial(_ffn_kernel, k=k, gi=gi, n_lat=geom.n_lat, split_in=split_in, has_pre=has_pre,
                             has_ctx=has_ctx)
    return pl.pallas_call(
        kern,
        grid=(n_tiles,),
        in_specs=in_specs,
        out_specs=pl.BlockSpec((FFN_TILE, d), lambda t: (t, 0)),
        out_shape=jax.ShapeDtypeStruct((n_tiles * FFN_TILE, d), F32),
        compiler_params=_cparams(1, FFN_VMEM_LIMIT),
        name=name,
    )(*args)


def _two_halves(*stages):
    halves = [slice(0, ROW_TILE // 2), slice(ROW_TILE // 2, ROW_TILE)]
    states = [stages[0](rows) for rows in halves]
    for stage in stages[1:]:
        states = [stage(rows, st) for rows, st in zip(halves, states)]


def _proj0_kernel(x_ref, mod_ref, g_ref, w_ref, cq_ref, sq_ref, ck_ref, sk_ref, q_ref, k_ref, v_ref, *, nq, nk, nv):
    def front(rows):
        u = _modnorm(x_ref[rows, :], g_ref[2:3, :], mod_ref, 3).astype(BF16)
        return jnp.dot(u, w_ref[...], preferred_element_type=F32)

    def back(rows, t):
        for i, blk in enumerate(_rope_blocks(_lane_blocks(t, 0, nq), cq_ref[rows, :], sq_ref[rows, :])):
            q_ref[i, rows, :] = blk.astype(BF16)
        for i, blk in enumerate(_rope_blocks(_lane_blocks(t, nq, nk), ck_ref[rows, :], sk_ref[rows, :])):
            k_ref[i, rows, :] = blk.astype(BF16)
        for i, blk in enumerate(_lane_blocks(t, nq + nk, nv)):
            v_ref[i, rows, :] = blk.astype(BF16)

    _two_halves(front, back)


def _proj1_kernel(x_ref, mod_ref, g_ref, w_ref, qn_ref, kn_ref, cq_ref, sq_ref, ck_ref, sk_ref,
                  q_ref, k_ref, v_ref, *, nq, nk, nv):
    qn, kn = qn_ref[...], kn_ref[...]

    def front(rows):
        u = _modnorm(x_ref[rows, :], g_ref[2:3, :], mod_ref, 3).astype(BF16)
        return jnp.dot(u, w_ref[...], preferred_element_type=F32)

    def back(rows, t):
        cq, sq, ck, sk = cq_ref[rows, :], sq_ref[rows, :], ck_ref[rows, :], sk_ref[rows, :]
        blocks = _lane_blocks(t, 0, nq + nk)
        inv = [lax.rsqrt(jnp.mean(b * b, axis=-1, keepdims=True) + NORM_EPS) for b in blocks]
        normed = [b * r * (qn if i < nq else kn) for i, (b, r) in enumerate(zip(blocks, inv))]
        for i, blk in enumerate(_rope_blocks(normed[:nq], cq, sq)):
            q_ref[i, rows, :] = blk.astype(BF16)
        for i, blk in enumerate(_rope_blocks(normed[nq:], ck, sk)):
            k_ref[i, rows, :] = blk.astype(BF16)
        for i, blk in enumerate(_lane_blocks(t, nq + nk, nv)):
            v_ref[i, rows, :] = blk.astype(BF16)

    _two_halves(front, back)


def _proj2_kernel(x_ref, mod_ref, g_ref, w1_ref, qn_ref, kvn_ref, wuq_ref, wuk_ref, wuv_ref,
                  cq_ref, sq_ref, ck_ref, sk_ref, q_ref, k_ref, v_ref):
    u = _modnorm(x_ref[...], g_ref[2:3, :], mod_ref, 3).astype(BF16)
    t = jnp.dot(u, w1_ref[...], preferred_element_type=F32)
    cqv = _rms(t[:, :MLA_Q_RANK], qn_ref[...]).astype(BF16)
    ckv = _rms(t[:, MLA_Q_RANK:MLA_Q_RANK + MLA_KV_RANK], kvn_ref[...]).astype(BF16)
    kr = _rope(t[:, MLA_Q_RANK + MLA_KV_RANK:], ck_ref[...], sk_ref[...])
    qa = jnp.dot(cqv, wuq_ref[...], preferred_element_type=F32)
    for i, blk in enumerate(_rope_blocks(_lane_blocks(qa, 0, MLA_HEADS), cq_ref[...], sq_ref[...])):
        q_ref[i] = blk.astype(BF16)
    ka = jnp.dot(ckv, wuk_ref[...], preferred_element_type=F32)
    for i, blk in enumerate(_lane_blocks(ka, 0, MLA_HEADS)):
        k_ref[i] = (blk + kr).astype(BF16)
    va = jnp.dot(ckv, wuv_ref[...], preferred_element_type=F32)
    for i, blk in enumerate(_lane_blocks(va, 0, MLA_HEADS // 2)):
        v_ref[i] = blk.astype(BF16)


def _proj_call(geom, kern, h, mod, g8, consts, tables, nq, nk, nv, name):
    d = D_MODEL
    tile = lambda t: (t, 0)
    rope_spec = pl.BlockSpec((ROW_TILE, LANES), lambda t: (geom.rope_blk(t), 0))
    in_specs = ([pl.BlockSpec((ROW_TILE, d), tile),
                 pl.BlockSpec((None, 1, N_MOD * d), lambda t: (geom.mod_row(t), 0, 0)),
                 _const_spec(g8.shape)]
                + [_const_spec(c.shape) for c in consts]
                + [rope_spec] * 4)
    blk = lambda n: pl.BlockSpec((n, ROW_TILE, LANES), lambda t: (0, t, 0))
    shp = lambda n: jax.ShapeDtypeStruct((n, geom.n_rows, LANES), BF16)
    return pl.pallas_call(
        kern,
        grid=(geom.n_tiles,),
        in_specs=in_specs,
        out_specs=[blk(nq), blk(nk), blk(nv)],
        out_shape=[shp(nq), shp(nk), shp(nv)],
        compiler_params=_cparams(1),
        name=name,
    )(h, mod, g8, *consts, *tables)


def _qk(qs, k):
    return lax.dot_general(qs, k, (((1,), (1,)), ((), ())), preferred_element_type=F32)


def _pv_with_sums(p, v):
    v_ext = jnp.concatenate([v, jnp.ones_like(v)], axis=1)
    return jnp.dot(p, v_ext, preferred_element_type=F32)


def _flash_stream(jobs):
    steps = [(j, c) for j, (_, chunks, _) in enumerate(jobs) for c in range(len(chunks))]
    qs_of = {}

    def scores(j, c):
        if j not in qs_of:
            qs_of[j] = jobs[j][0]()
        k_ref, hk, _, _, start, size = jobs[j][1][c]
        return _qk(qs_of[j], k_ref[hk, pl.ds(start, size), :])

    m = acc = None
    s_next = scores(*steps[0])
    for idx, (j, c) in enumerate(steps):
        s = s_next
        if idx + 1 < len(steps):
            s_next = scores(*steps[idx + 1])
        _, _, v_ref, hv, start, size = jobs[j][1][c]
        mc = jnp.max(s, axis=1, keepdims=True)
        m_new = mc if c == 0 else jnp.maximum(m, mc)
        p = jnp.exp2(s - m_new).astype(BF16)
        pv = _pv_with_sums(p, v_ref[hv, pl.ds(start, size), :])
        acc = pv if c == 0 else jnp.exp2(m - m_new) * acc + pv
        m = m_new
        if c == len(jobs[j][1]) - 1:
            qs_of.pop(j)
            jobs[j][2](acc[:, :LANES], acc[:, LANES:])


def _kv_chunks(kc_ref, vc_ref, kl_ref, vl_ref, hk, hv, n_ctx_keys, n_lat_keys, tk):
    chunks = [(kc_ref, hk, vc_ref, hv, 0, n_ctx_keys)]
    if kl_ref is not None:
        chunks += [(kl_ref, hk, vl_ref, hv, j * tk, tk) for j in range(n_lat_keys // tk)]
    return chunks


def _split_refs(refs, n_lead, has_lat):
    lead = refs[:n_lead]
    rest = refs[n_lead:]
    q_ref, kc_ref, vc_ref = rest[:3]
    if has_lat:
        kl_ref, vl_ref, o_ref = rest[3:6]
    else:
        kl_ref = vl_ref = None
        o_ref = rest[3]
    return lead, q_ref, kc_ref, vc_ref, kl_ref, vl_ref, o_ref


def _lane_lo(rows):
    lane = lax.broadcasted_iota(jnp.int32, (rows, LANES), 1)
    return (lane % HALF) < (HALF // 2)


def _attn_diff_kernel(*refs, has_lat, n_heads, n_ctx_keys, n_lat_keys, tk, lam_init):
    (lam_ref, subln_ref), q_ref, kc_ref, vc_ref, kl_ref, vl_ref, o_ref = _split_refs(refs, 2, has_lat)
    tq = q_ref.shape[1]
    lo = _lane_lo(tq)
    lf = lam_ref[...]
    lam = (jnp.exp(jnp.sum(lf[0:1] * lf[1:2], keepdims=True))
           - jnp.exp(jnp.sum(lf[2:3] * lf[3:4], keepdims=True)) + lam_init)
    subln = subln_ref[...]

    def job(h):
        def make_qs():
            q = q_ref[h]
            zero = jnp.zeros_like(q)
            return jnp.concatenate([jnp.where(lo, q, zero), jnp.where(lo, zero, q)], axis=0)

        def done(acc, l):
            o = acc / l
            o = o[:tq] - lam * o[tq:]
            o_ref[h] = (_rms(o, subln) * (1.0 - lam_init)).astype(BF16)

        return make_qs, _kv_chunks(kc_ref, vc_ref, kl_ref, vl_ref, h, h, n_ctx_keys, n_lat_keys, tk), done

    def head_pair(i, carry):
        _flash_stream([job(2 * i), job(2 * i + 1)])
        return carry

    lax.fori_loop(0, n_heads // 2, head_pair, 0)


def _attn_gqa_kernel(*refs, has_lat, grp, n_ctx_keys, n_lat_keys, tk):
    _, q_ref, kc_ref, vc_ref, kl_ref, vl_ref, o_ref = _split_refs(refs, 0, has_lat)
    tq = q_ref.shape[1]
    rows = min(tq, GQA_JOB_ROWS)

    def job(r0):
        def done(acc, l):
            o = acc / l
            for j in range(grp):
                o_ref[j, r0:r0 + rows, :] = o[j * rows:(j + 1) * rows].astype(BF16)

        return ((lambda: jnp.concatenate([q_ref[j, r0:r0 + rows, :] for j in range(grp)], axis=0)),
                _kv_chunks(kc_ref, vc_ref, kl_ref, vl_ref, 0, 0, n_ctx_keys, n_lat_keys, tk), done)

    _flash_stream([job(r0) for r0 in range(0, tq, rows)])


def _attn_mla_kernel(*refs, has_lat, n_pairs, n_ctx_keys, n_lat_keys, tk):
    _, q_ref, kc_ref, vc_ref, kl_ref, vl_ref, o_ref = _split_refs(refs, 0, has_lat)
    tq = q_ref.shape[1]
    lane = lax.broadcasted_iota(jnp.int32, (tq, LANES), 1)

    def pair(p, carry):
        even_out = []

        def job(e):
            h = 2 * p + e

            def done(acc, l):
                if e == 0:
                    even_out.append(acc / l)
                else:
                    o_ref[p] = jnp.where(lane < HALF, even_out[0], acc / l).astype(BF16)

            return ((lambda: q_ref[h]),
                    _kv_chunks(kc_ref, vc_ref, kl_ref, vl_ref, h, p, n_ctx_keys, n_lat_keys, tk), done)

        _flash_stream([job(0), job(1)])
        return carry

    lax.fori_loop(0, n_pairs, pair, 0)


def _attn_swa_kernel(sink_ref, q_ref, kc_ref, vc_ref, kl_ref, vl_ref, o_ref, *, n_pairs, pairs_per_stage, sub_rows, seq):
    tq = sub_rows
    win = tq + 2 * WINDOW
    n_sub = q_ref.shape[1] // tq
    i = pl.program_id(2)
    lo = _lane_lo(tq)
    zero = jnp.zeros((tq, LANES), BF16)
    n_stage = n_pairs // pairs_per_stage
    heads = 2 * pairs_per_stage

    def values_and_ones(v):
        lane = lax.broadcasted_iota(jnp.int32, v.shape, 1)
        return jnp.where(lane < HALF, v, jnp.ones_like(v))

    k_ctx, v_ctx = kc_ref[0], values_and_ones(vc_ref[0])

    def band(sub):
        q0 = (i * n_sub + sub) * tq
        start = pl.multiple_of(jnp.clip(q0 - WINDOW, 0, seq - win), WINDOW)
        kpos = start + lax.broadcasted_iota(jnp.int32, (win, tq), 0)
        qpos = q0 + lax.broadcasted_iota(jnp.int32, (win, tq), 1)
        bias = jnp.where(jnp.abs(qpos - kpos) <= WINDOW, 0.0, NEG_INF)
        return (jnp.concatenate([bias] * heads, axis=1), kl_ref[0, pl.ds(start, win), :],
                values_and_ones(vl_ref[0, pl.ds(start, win), :]))

    bands = [band(sub) for sub in range(n_sub)]

    stages = [(sub, t) for sub in range(n_sub) for t in range(n_stage)]

    def scores(sub, t):
        bias_cols, k_win, _ = bands[sub]
        q_rows = []
        for j in range(t * pairs_per_stage, (t + 1) * pairs_per_stage):
            q = q_ref[j, sub * tq:(sub + 1) * tq, :]
            q_rows += [jnp.where(lo, q, zero), jnp.where(lo, zero, q)]
        qs = jnp.concatenate(q_rows, axis=0)
        return _qk(k_ctx, qs), _qk(k_win, qs) + bias_cols

    def softmax(sub, t, s_ctx, s_band):
        sink = jnp.concatenate([sink_ref[h:h + 1, :] for h in range(t * heads, (t + 1) * heads)
                                for _ in range(tq // LANES)], axis=1)
        m = jnp.maximum(jnp.maximum(jnp.max(s_ctx, axis=0, keepdims=True),
                                    jnp.max(s_band, axis=0, keepdims=True)), sink)
        return jnp.exp2(s_ctx - m).astype(BF16), jnp.exp2(s_band - m).astype(BF16), jnp.exp2(sink - m)

    def pv_t(v, p):
        return lax.dot_general(v, p, (((0,), (0,)), ((), ())), preferred_element_type=F32)

    def finish(sub, t, p_ctx, p_band, p_sink):
        acc = pv_t(v_ctx, p_ctx) + pv_t(bands[sub][2], p_band)
        o_t = acc[:HALF] / (acc[HALF:HALF + 1] + p_sink)
        for j in range(pairs_per_stage):
            pair_t = jnp.concatenate([o_t[:, 2 * j * tq:(2 * j + 1) * tq],
                                      o_t[:, (2 * j + 1) * tq:(2 * j + 2) * tq]], axis=0)
            o_ref[t * pairs_per_stage + j, sub * tq:(sub + 1) * tq, :] = pair_t.T.astype(BF16)

    s_vals, p_vals = {}, {}
    for n in range(len(stages) + 2):
        if n < len(stages):
            s_vals[n] = scores(*stages[n])
        if 0 <= n - 1 < len(stages):
            p_vals[n - 1] = softmax(*stages[n - 1], *s_vals.pop(n - 1))
        if 0 <= n - 2 < len(stages):
            finish(*stages[n - 2], *p_vals.pop(n - 2))


def _attn_call(geom, kern, lead, lead_specs, q, k, v, *, has_lat, tq, q_per, k_per, v_per, o_per, n_groups, name):
    b, s, c = geom.batch, geom.seq, geom.ctx
    ctx_blk0 = geom.n_lat_rows // c
    if has_lat:
        n_q = s // tq
        q_map = lambda bi, g, i: (g, bi * n_q + i, 0)
        o_rows = geom.n_lat_rows
    else:
        assert tq == c
        n_q = 1
        q_map = lambda bi, g, i: (g, ctx_blk0 + bi, 0)
        o_rows = b * c
    o_map = lambda bi, g, i: (g, bi * n_q + i, 0)
    ctx_map = lambda bi, g, i: (g, ctx_blk0 + bi, 0)
    lat_map = lambda bi, g, i: (g, bi, 0)
    in_specs = list(lead_specs) + [pl.BlockSpec((q_per, tq, LANES), q_map),
                                   pl.BlockSpec((k_per, c, LANES), ctx_map),
                                   pl.BlockSpec((v_per, c, LANES), ctx_map)]
    args = list(lead) + [q, k, v]
    if has_lat:
        in_specs += [pl.BlockSpec((k_per, s, LANES), lat_map), pl.BlockSpec((v_per, s, LANES), lat_map)]
        args += [k, v]
    return pl.pallas_call(
        kern,
        grid=(b, n_groups, n_q),
        in_specs=in_specs,
        out_specs=pl.BlockSpec((o_per, tq, LANES), o_map),
        out_shape=jax.ShapeDtypeStruct((o_per * n_groups, o_rows, LANES), BF16),
        compiler_params=_cparams(3),
        name=name,
    )(*args)


def _rope_tables(seq, rot_dim, aidx, active, scale):
    scale = scale * LOG2E
    pos = np.arange(seq)
    row = jnp.asarray(pos // GRID_W, F32)
    col = jnp.asarray(pos % GRID_W, F32)
    n_axis = rot_dim // 4
    freqs = ROPE_THETA ** (-jnp.arange(n_axis, dtype=F32) / n_axis)
    ang = jnp.concatenate([row[:, None] * freqs, col[:, None] * freqs], axis=-1)
    a = ang[:, np.asarray(aidx)]
    act = jnp.asarray(active)[None, :]
    sign = jnp.asarray(np.where(np.arange(LANES) < HALF, -1.0, 1.0), F32)[None, :]
    c = jnp.where(act, jnp.cos(a), 1.0)
    sn = jnp.where(act, jnp.sin(a) * sign, 0.0)
    c = jnp.concatenate([c, jnp.ones((ROW_TILE, LANES), F32)], axis=0)
    sn = jnp.concatenate([sn, jnp.zeros((ROW_TILE, LANES), F32)], axis=0)
    return (c * scale, sn * scale), (c, sn)


def _take_cols(w, idx):
    idx = np.asarray(idx)
    out = jnp.take(w, jnp.asarray(np.maximum(idx, 0)), axis=1)
    if (idx < 0).any():
        out = jnp.where(jnp.asarray(idx >= 0)[None, :], out, 0.0)
    return out


def _eo(d):
    return np.concatenate([np.arange(0, d, 2), np.arange(1, d, 2)])


def _pair_layout():
    ev, od = np.arange(0, 64, 2), np.arange(1, 64, 2)
    which = np.concatenate([np.zeros(32), np.ones(32), np.zeros(32), np.ones(32)]).astype(np.int64)
    dim = np.concatenate([ev, ev, od, od])
    return which, dim


def kernel(x, c, ctx, c_ctx, ada_w, ada_b, norm_g, ffn_w_in, ffn_w_out, da_w_in, da_lambda, da_subln, da_w_out,
           ga_w_in, ga_q_norm, ga_k_norm, ga_w_out, mla_w_in, mla_q_norm, mla_kv_norm, mla_w_uq, mla_w_ukv,
           mla_w_out, swa_w_in, swa_sink, swa_w_out):
    batch, seq, d = x.shape
    n_ctx = ctx.shape[1]
    depth = ada_w.shape[0]
    assert d == D_MODEL and seq % FFN_TILE == 0 and (batch * n_ctx) % FFN_TILE == 0
    assert n_ctx % V7X_MXU_WIDTH == 0 and batch < MOD_ROWS and depth == 4
    assert all(seq % t == 0 for t in (ATTN_KEY_CHUNK, DIFF_Q_TILE, MLA_Q_TILE, 2 * GQA_JOB_ROWS,
                                    SWA_SUBTILES * SWA_Q_TILE))
    geom = _Geom(batch, seq, n_ctx, ROW_TILE)
    fgeom = _Geom(batch, seq, n_ctx, FFN_TILE)

    cond = jnp.zeros((MOD_ROWS, d), F32).at[:batch].set(c).at[batch].set(c_ctx)
    mods = _ada_call(cond, ada_w, ada_b).reshape(depth, MOD_ROWS, 1, N_MOD * d)

    h = (x.reshape(batch * seq, d), ctx.reshape(batch * n_ctx, d))
    g8 = jnp.concatenate([norm_g, jnp.ones((depth, 2, d), F32)], axis=1)
    w_in_b = ffn_w_in.astype(BF16)
    w_out_b = ffn_w_out.astype(BF16)

    which, pdim = _pair_layout()
    tk = ATTN_KEY_CHUNK

    for i in range(depth):
        mod = mods[i]
        last = i == depth - 1
        h = _ffn_call(fgeom, h, mod, g8[i], w_in_b, w_out_b, i, 0, k=0, gi=0, name=f"ffn_a{i}")

        if i == 0:
            w = da_w_in[0]
            hq = DA_HEADS * 2 * DA_QK
            qcols = np.concatenate([hd * 128 + which * 64 + pdim for hd in range(DA_HEADS)])
            wcat = jnp.concatenate([_take_cols(w, qcols), _take_cols(w, hq + qcols), w[:, 2 * hq:]], axis=1)
            tq_tab, tk_tab = _rope_tables(seq, DA_QK, np.arange(LANES) % 32, np.ones(LANES, bool), DA_QK ** -0.5)
            q, k, v = _proj_call(geom, functools.partial(_proj0_kernel, nq=8, nk=8, nv=8), h, mod, g8[i],
                                 [wcat.astype(BF16)], [*tq_tab, *tk_tab], 8, 8, 8, name="proj_diff")
            lam_init = 0.8 - 0.6 * math.exp(-0.3 * i)
            lead = [da_lambda[0], da_subln[0].reshape(1, LANES)]
            lead_specs = [pl.BlockSpec((4, DA_QK), lambda bi, g, qi: (0, 0)),
                          pl.BlockSpec((1, LANES), lambda bi, g, qi: (0, 0))]
            hs = DIFF_HEADS_PER_STEP
            mk = lambda has_lat: functools.partial(_attn_diff_kernel, has_lat=has_lat, n_heads=hs, n_ctx_keys=n_ctx,
                                                   n_lat_keys=seq, tk=tk, lam_init=lam_init)
            o_lat = _attn_call(geom, mk(True), lead, lead_specs, q, k, v, has_lat=True, tq=DIFF_Q_TILE, q_per=hs, k_per=hs,
                               v_per=hs, o_per=hs, n_groups=DA_HEADS // hs, name="attn_diff")
            o_ctx = _attn_call(geom, mk(False), lead, lead_specs, q, k, v, has_lat=False, tq=n_ctx, q_per=hs,
                               k_per=hs, v_per=hs, o_per=hs, n_groups=DA_HEADS // hs, name="attn_diff_ctx")
            w_mo = da_w_out[0]
        elif i == 1:
            w = ga_w_in[0]
            eo = _eo(GA_DIM)
            qcols = np.concatenate([hd * GA_DIM + eo for hd in range(GA_HEADS)])
            kcols = GA_HEADS * GA_DIM + np.concatenate([hd * GA_DIM + eo for hd in range(GA_KV)])
            wcat = jnp.concatenate([_take_cols(w, qcols), _take_cols(w, kcols), w[:, (GA_HEADS + GA_KV) * GA_DIM:]],
                                   axis=1)
            tq_tab, tk_tab = _rope_tables(seq, GA_DIM, np.arange(LANES) % 64, np.ones(LANES, bool), GA_DIM ** -0.5)
            consts = [wcat.astype(BF16), ga_q_norm[0][eo].reshape(1, LANES), ga_k_norm[0][eo].reshape(1, LANES)]
            q, k, v = _proj_call(geom, functools.partial(_proj1_kernel, nq=8, nk=2, nv=2), h, mod, g8[i],
                                 consts, [*tq_tab, *tk_tab], 8, 2, 2, name="proj_gqa")
            grp = GA_HEADS // GA_KV
            mk = lambda has_lat: functools.partial(_attn_gqa_kernel, has_lat=has_lat, grp=grp, n_ctx_keys=n_ctx,
                                                   n_lat_keys=seq, tk=tk)
            o_lat = _attn_call(geom, mk(True), [], [], q, k, v, has_lat=True, tq=2 * GQA_JOB_ROWS, q_per=grp,
                               k_per=1, v_per=1, o_per=grp, n_groups=GA_KV, name="attn_gqa")
            o_ctx = _attn_call(geom, mk(False), [], [], q, k, v, has_lat=False, tq=n_ctx, q_per=grp, k_per=1,
                               v_per=1, o_per=grp, n_groups=GA_KV, name="attn_gqa_ctx")
            w_mo = ga_w_out[0]
        elif i == 2:
            lanes = np.arange(LANES)
            is_rope = ((lanes % HALF) >= 32) & ((lanes % HALF) < 48)
            is_nope = (lanes % HALF) < 32
            rope_dim = 2 * ((lanes % HALF) - 32) + (lanes >= HALF)
            nope_dim = (lanes % HALF) + 32 * (lanes >= HALF)
            w = mla_w_in[0]
            kr_cols = np.where(is_rope, MLA_Q_RANK + MLA_KV_RANK + rope_dim, -1)
            w1 = jnp.concatenate([w[:, :MLA_Q_RANK + MLA_KV_RANK], _take_cols(w, kr_cols)], axis=1)
            qd = MLA_NOPE + MLA_ROPE
            q_blk = np.where(is_nope, nope_dim, np.where(is_rope, MLA_NOPE + rope_dim, -1))
            q_cols = np.concatenate([np.where(q_blk >= 0, hd * qd + q_blk, -1) for hd in range(MLA_HEADS)])
            kvd = MLA_NOPE + MLA_V
            k_blk = np.where(is_nope, nope_dim, -1)
            k_cols = np.concatenate([np.where(k_blk >= 0, hd * kvd + k_blk, -1) for hd in range(MLA_HEADS)])
            v_cols = np.concatenate([hd * kvd + MLA_NOPE + np.arange(MLA_V) for hd in range(MLA_HEADS)])
            consts = [w1.astype(BF16), mla_q_norm[0].reshape(1, -1), mla_kv_norm[0].reshape(1, -1),
                      _take_cols(mla_w_uq[0], q_cols).astype(BF16), _take_cols(mla_w_ukv[0], k_cols).astype(BF16),
                      _take_cols(mla_w_ukv[0], v_cols).astype(BF16)]
            aidx = np.where(is_rope, (lanes % HALF) - 32, 0)
            tq_tab, tk_tab = _rope_tables(seq, MLA_ROPE, aidx, is_rope, (MLA_NOPE + MLA_ROPE) ** -0.5)
            q, k, v = _proj_call(geom, _proj2_kernel, h, mod, g8[i], consts, [*tq_tab, *tk_tab],
                                 MLA_HEADS, MLA_HEADS, MLA_HEADS // 2, name="proj_mla")
            n_pairs = MLA_PAIRS_PER_STEP
            mk = lambda has_lat: functools.partial(_attn_mla_kernel, has_lat=has_lat, n_pairs=n_pairs,
                                                   n_ctx_keys=n_ctx, n_lat_keys=seq, tk=tk)
            o_lat = _attn_call(geom, mk(True), [], [], q, k, v, has_lat=True, tq=MLA_Q_TILE, q_per=2 * n_pairs,
                               k_per=2 * n_pairs, v_per=n_pairs, o_per=n_pairs,
                               n_groups=MLA_HEADS // (2 * n_pairs), name="attn_mla")
            o_ctx = _attn_call(geom, mk(False), [], [], q, k, v, has_lat=False, tq=n_ctx, q_per=2 * n_pairs,
                               k_per=2 * n_pairs, v_per=n_pairs, o_per=n_pairs,
                               n_groups=MLA_HEADS // (2 * n_pairs), name="attn_mla_ctx")
            w_mo = mla_w_out[0]
        else:
            w = swa_w_in[0]
            hq = SWA_HEADS * SWA_DIM
            qcols = np.concatenate([(2 * p + which) * SWA_DIM + pdim for p in range(SWA_HEADS // 2)])
            kcols = np.concatenate([hq + g * SWA_DIM + pdim for g in range(SWA_KV)])
            vcols = np.concatenate([hq + SWA_KV * SWA_DIM + g * SWA_DIM + (np.arange(LANES) % SWA_DIM)
                                    for g in range(SWA_KV)])
            wcat = jnp.concatenate([_take_cols(w, qcols), _take_cols(w, kcols), _take_cols(w, vcols)], axis=1)
            tq_tab, tk_tab = _rope_tables(seq, SWA_DIM, np.arange(LANES) % 32, np.ones(LANES, bool), SWA_DIM ** -0.5)
            q, k, v = _proj_call(geom, functools.partial(_proj0_kernel, nq=8, nk=2, nv=2), h, mod, g8[i],
                                 [wcat.astype(BF16)], [*tq_tab, *tk_tab], 8, 2, 2, name="proj_swa")
            hpg = SWA_HEADS // SWA_KV
            sink = jnp.broadcast_to((swa_sink[0].astype(F32) * LOG2E).reshape(SWA_KV, hpg, 1), (SWA_KV, hpg, LANES))
            tq = SWA_SUBTILES * SWA_Q_TILE
            kern = functools.partial(_attn_swa_kernel, n_pairs=hpg // 2, pairs_per_stage=SWA_PAIRS_PER_STAGE,
                                     sub_rows=SWA_Q_TILE, seq=seq)
            o_lat = _attn_call(geom, kern, [sink], [pl.BlockSpec((None, hpg, LANES), lambda bi, g, qi: (g, 0, 0))],
                               q, k, v, has_lat=True, tq=tq, q_per=hpg // 2, k_per=1, v_per=1, o_per=hpg // 2,
                               n_groups=SWA_KV, name="attn_swa")
            o_ctx = None
            w_mo = swa_w_out[0]

        h = _ffn_call(fgeom, h, mod, g8[i], w_in_b, w_out_b, i, 1, k=6, gi=4,
                      pre=(o_lat, o_ctx, w_mo.astype(BF16)), n_tiles=fgeom.n_lat if last else None,
                      name=f"ffn_b{i}")
    return h.reshape(batch, seq, d)
```

```python
import functools
import math

import numpy as np
import jax
import jax.numpy as jnp
from jax import lax
from jax.experimental import pallas as pl
from jax.experimental.pallas import tpu as pltpu

F32 = jnp.float32
BF16 = jnp.bfloat16

D_MODEL = 1024
D_FF = 2816
N_MOD = 9
NORM_EPS = 1e-6
GRID_W = 64
ROPE_THETA = 10000.0
NEG_INF = -1e30
WINDOW = 128
LOG2E = math.log2(math.e)

LANES = 128
HALF = LANES // 2
V7X_MXU_WIDTH = 256
V7X_VMEM_BYTES = 64 * 1024 * 1024
ROW_TILE = 1024
FFN_HALF = 512
FFN_TILE = 2 * FFN_HALF
FF_CHUNK = V7X_MXU_WIDTH
ADA_COL_TILE = 1024
MOD_ROWS = 16
ATTN_KEY_CHUNK = 512
GQA_JOB_ROWS = 256
DIFF_Q_TILE, DIFF_HEADS_PER_STEP = 512, 4
MLA_Q_TILE, MLA_PAIRS_PER_STEP = 1024, 4
SWA_Q_TILE, SWA_PAIRS_PER_STAGE, SWA_SUBTILES = 256, 2, 4
VMEM_LIMIT = V7X_VMEM_BYTES * 13 // 16
FFN_VMEM_LIMIT = V7X_VMEM_BYTES * 29 // 32

DA_HEADS, DA_QK = 8, 64
GA_HEADS, GA_KV, GA_DIM = 8, 2, 128
MLA_HEADS, MLA_Q_RANK, MLA_KV_RANK, MLA_NOPE, MLA_ROPE, MLA_V = 16, 256, 128, 64, 32, 64
SWA_HEADS, SWA_KV, SWA_DIM = 16, 2, 64


def _cparams(n_axes, vmem_limit=VMEM_LIMIT):
    return pltpu.CompilerParams(dimension_semantics=("arbitrary",) * n_axes,
                                vmem_limit_bytes=vmem_limit)


def _rms(x, g):
    return x * lax.rsqrt(jnp.mean(x * x, axis=-1, keepdims=True) + NORM_EPS) * g


def _mod(mod_ref, k):
    return mod_ref[:, k * D_MODEL:(k + 1) * D_MODEL]


def _modnorm(x, g, mod_ref, k):
    return _rms(x, g) * (1.0 + _mod(mod_ref, k + 1)) + _mod(mod_ref, k)


def _rope(x, c, s):
    return x * c + pltpu.roll(x, HALF, 1) * s


def _rope_blocks(xs, c, s):
    rolled = [pltpu.roll(x, HALF, 1) for x in xs]
    return [x * c + r * s for x, r in zip(xs, rolled)]


def _lane_blocks(x, first, n):
    return [x[:, (first + i) * LANES:(first + i + 1) * LANES] for i in range(n)]


def _ada_kernel(c_ref, w_ref, b_ref, o_ref):
    c = c_ref[...]
    sc = c / (1.0 + jnp.exp(-c))
    o_ref[...] = jnp.dot(sc, w_ref[...], precision=lax.Precision.HIGHEST,
                         preferred_element_type=F32) + b_ref[...]


def _ada_call(cond, ada_w, ada_b):
    depth, d, n = ada_w.shape
    cw = ADA_COL_TILE
    return pl.pallas_call(
        _ada_kernel,
        grid=(depth, n // cw),
        in_specs=[pl.BlockSpec((MOD_ROWS, d), lambda l, j: (0, 0)),
                  pl.BlockSpec((None, d, cw), lambda l, j: (l, 0, j)),
                  pl.BlockSpec((None, 1, cw), lambda l, j: (l, 0, j))],
        out_specs=pl.BlockSpec((None, MOD_ROWS, cw), lambda l, j: (l, 0, j)),
        out_shape=jax.ShapeDtypeStruct((depth, MOD_ROWS, n), F32),
        compiler_params=_cparams(2),
        name="adaln",
    )(cond, ada_w, ada_b.reshape(depth, 1, n))


class _Geom:
    def __init__(self, batch, seq, ctx, tile):
        self.batch, self.seq, self.ctx, self.tile = batch, seq, ctx, tile
        self.n_lat_rows = batch * seq
        self.n_rows = batch * (seq + ctx)
        self.n_lat = self.n_lat_rows // tile
        self.n_tiles = self.n_rows // tile
        self.tiles_per_batch = seq // tile

    def mod_row(self, t):
        return jnp.where(t < self.n_lat, t // self.tiles_per_batch, self.batch)

    def rope_blk(self, t):
        return jnp.where(t < self.n_lat, t % self.tiles_per_batch, self.tiles_per_batch)


def _const_spec(shape):
    nd = len(shape)
    return pl.BlockSpec(shape, lambda *_: (0,) * nd, pipeline_mode=pl.Buffered(1))


def _ffn_kernel(*refs, k, gi, n_lat, split_in, has_pre, has_ctx):
    it = iter(refs)
    x_ref = next(it)
    xc_ref = next(it) if split_in else None
    mod_ref, g_ref, win_ref, wout_ref = (next(it) for _ in range(4))
    if has_pre:
        ol_ref = next(it)
        oc_ref = next(it) if has_ctx else None
        wmo_ref = next(it)
    out_ref = next(it)
    is_lat = pl.program_id(0) < n_lat

    def prologue(r0):
        rows = slice(r0, r0 + FFN_HALF)
        x = x_ref[rows, :]
        if split_in:
            x = jnp.where(is_lat, x, xc_ref[rows, :])
        if has_pre:
            nblk = ol_ref.shape[0]
            o = jnp.concatenate([ol_ref[i, rows, :] for i in range(nblk)], axis=1)
            if has_ctx:
                o = jnp.where(is_lat, o, jnp.concatenate([oc_ref[i, rows, :] for i in range(nblk)], axis=1))
            y = jnp.dot(o, wmo_ref[...], preferred_element_type=F32)
            x = x + _mod(mod_ref, 5) * _rms(y, g_ref[3:4, :])
        return x, _modnorm(x, g_ref[gi:gi + 1, :], mod_ref, k).astype(BF16)

    def gate_up(u, c):
        lo = c * FF_CHUNK
        return (jnp.dot(u, win_ref[:, lo:lo + FF_CHUNK], preferred_element_type=F32),
                jnp.dot(u, win_ref[:, D_FF + lo:D_FF + lo + FF_CHUNK], preferred_element_type=F32))

    n_chunks = D_FF // FF_CHUNK
    steps = [(blk, c) for blk in range(FFN_TILE // FFN_HALF) for c in range(n_chunks)]
    xs, us = {}, {}
    xs[0], us[0] = prologue(0)
    nxt = gate_up(us[0], 0)
    xs[1], us[1] = prologue(FFN_HALF)
    acc = None
    for idx, (blk, c) in enumerate(steps):
        a, b = nxt
        if idx + 1 < len(steps):
            nxt = gate_up(us[steps[idx + 1][0]], steps[idx + 1][1])
        act = (a / (1.0 + jnp.exp(-a)) * b).astype(BF16)
        part = jnp.dot(act, wout_ref[c * FF_CHUNK:(c + 1) * FF_CHUNK, :], preferred_element_type=F32)
        acc = part if c == 0 else acc + part
        if c == n_chunks - 1:
            r0 = blk * FFN_HALF
            out_ref[r0:r0 + FFN_HALF, :] = (xs.pop(blk)
                                            + 0.5 * _mod(mod_ref, k + 2) * _rms(acc, g_ref[gi + 1:gi + 2, :]))


def _ffn_call(geom, h, mod, g8, w_in, w_out, layer, half, *, k, gi, pre=None, n_tiles=None, name):
    n_tiles = geom.n_tiles if n_tiles is None else n_tiles
    d = D_MODEL
    w_spec = lambda w: pl.BlockSpec((None, None) + w.shape[2:], lambda t: (layer, half, 0, 0),
                                    pipeline_mode=pl.Buffered(1))
    split_in = isinstance(h, tuple)
    if split_in:
        in_specs = [pl.BlockSpec((FFN_TILE, d), lambda t: (jnp.minimum(t, geom.n_lat - 1), 0)),
                    pl.BlockSpec((FFN_TILE, d), lambda t: (jnp.maximum(t - geom.n_lat, 0), 0))]
        args = list(h)
    else:
        in_specs = [pl.BlockSpec((FFN_TILE, d), lambda t: (t, 0))]
        args = [h]
    in_specs += [pl.BlockSpec((None, 1, N_MOD * d), lambda t: (geom.mod_row(t), 0, 0)),
                 _const_spec(g8.shape), w_spec(w_in), w_spec(w_out)]
    args += [mod, g8, w_in, w_out]
    has_pre = pre is not None
    has_ctx = False
    if has_pre:
        o_lat, o_ctx, w_mo = pre
        nblk = o_lat.shape[0]
        in_specs.append(pl.BlockSpec((nblk, FFN_TILE, LANES),
                                     lambda t: (0, jnp.minimum(t, geom.n_lat - 1), 0)))
        args.append(o_lat)
        has_ctx = o_ctx is not None
        if has_ctx:
            in_specs.append(pl.BlockSpec((nblk, FFN_TILE, LANES),
                                         lambda t: (0, jnp.maximum(t - geom.n_lat, 0), 0)))
            args.append(o_ctx)
        in_specs.append(_const_spec(w_mo.shape))
        args.append(w_mo)
    kern = functools.partial(_ffn_kernel, k=k, gi=gi, n_lat=geom.n_lat, split_in=split_in, has_pre=has_pre,
                             has_ctx=has_ctx)
    return pl.pallas_call(
        kern,
        grid=(n_tiles,),
        in_specs=in_specs,
        out_specs=pl.BlockSpec((FFN_TILE, d), lambda t: (t, 0)),
        out_shape=jax.ShapeDtypeStruct((n_tiles * FFN_TILE, d), F32),
        compiler_params=_cparams(1, FFN_VMEM_LIMIT),
        name=name,
    )(*args)


def _two_halves(*stages):
    halves = [slice(r, r + 256) for r in range(0, ROW_TILE, 256)]
    states = [stages[0](rows) for rows in halves]
    for stage in stages[1:]:
        states = [stage(rows, st) for rows, st in zip(halves, states)]


def _proj0_kernel(x_ref, mod_ref, g_ref, w_ref, cq_ref, sq_ref, ck_ref, sk_ref, q_ref, k_ref, v_ref, *, nq, nk, nv):
    def front(rows):
        u = _modnorm(x_ref[rows, :], g_ref[2:3, :], mod_ref, 3).astype(BF16)
        return jnp.dot(u, w_ref[...], preferred_element_type=F32)

    def back(rows, t):
        for i, blk in enumerate(_rope_blocks(_lane_blocks(t, 0, nq), cq_ref[rows, :], sq_ref[rows, :])):
            q_ref[i, rows, :] = blk.astype(BF16)
        for i, blk in enumerate(_rope_blocks(_lane_blocks(t, nq, nk), ck_ref[rows, :], sk_ref[rows, :])):
            k_ref[i, rows, :] = blk.astype(BF16)
        for i, blk in enumerate(_lane_blocks(t, nq + nk, nv)):
            v_ref[i, rows, :] = blk.astype(BF16)

    _two_halves(front, back)


def _proj1_kernel(x_ref, mod_ref, g_ref, w_ref, qn_ref, kn_ref, cq_ref, sq_ref, ck_ref, sk_ref,
                  q_ref, k_ref, v_ref, *, nq, nk, nv):
    qn, kn = qn_ref[...], kn_ref[...]

    def front(rows):
        u = _modnorm(x_ref[rows, :], g_ref[2:3, :], mod_ref, 3).astype(BF16)
        return jnp.dot(u, w_ref[...], preferred_element_type=F32)

    def back(rows, t):
        cq, sq, ck, sk = cq_ref[rows, :], sq_ref[rows, :], ck_ref[rows, :], sk_ref[rows, :]
        blocks = _lane_blocks(t, 0, nq + nk)
        inv = [lax.rsqrt(jnp.mean(b * b, axis=-1, keepdims=True) + NORM_EPS) for b in blocks]
        normed = [b * r * (qn if i < nq else kn) for i, (b, r) in enumerate(zip(blocks, inv))]
        for i, blk in enumerate(_rope_blocks(normed[:nq], cq, sq)):
            q_ref[i, rows, :] = blk.astype(BF16)
        for i, blk in enumerate(_rope_blocks(normed[nq:], ck, sk)):
            k_ref[i, rows, :] = blk.astype(BF16)
        for i, blk in enumerate(_lane_blocks(t, nq + nk, nv)):
            v_ref[i, rows, :] = blk.astype(BF16)

    _two_halves(front, back)


def _proj2_kernel(x_ref, mod_ref, g_ref, w1_ref, qn_ref, kvn_ref, wuq_ref, wuk_ref, wuv_ref,
                  cq_ref, sq_ref, ck_ref, sk_ref, q_ref, k_ref, v_ref):
    u = _modnorm(x_ref[...], g_ref[2:3, :], mod_ref, 3).astype(BF16)
    t = jnp.dot(u, w1_ref[...], preferred_element_type=F32)
    cqv = _rms(t[:, :MLA_Q_RANK], qn_ref[...]).astype(BF16)
    ckv = _rms(t[:, MLA_Q_RANK:MLA_Q_RANK + MLA_KV_RANK], kvn_ref[...]).astype(BF16)
    kr = _rope(t[:, MLA_Q_RANK + MLA_KV_RANK:], ck_ref[...], sk_ref[...])
    qa = jnp.dot(cqv, wuq_ref[...], preferred_element_type=F32)
    for i, blk in enumerate(_rope_blocks(_lane_blocks(qa, 0, MLA_HEADS), cq_ref[...], sq_ref[...])):
        q_ref[i] = blk.astype(BF16)
    ka = jnp.dot(ckv, wuk_ref[...], preferred_element_type=F32)
    for i, blk in enumerate(_lane_blocks(ka, 0, MLA_HEADS)):
        k_ref[i] = (blk + kr).astype(BF16)
    va = jnp.dot(ckv, wuv_ref[...], preferred_element_type=F32)
    for i, blk in enumerate(_lane_blocks(va, 0, MLA_HEADS // 2)):
        v_ref[i] = blk.astype(BF16)


def _proj_call(geom, kern, h, mod, g8, consts, tables, nq, nk, nv, name):
    d = D_MODEL
    tile = lambda t: (t, 0)
    rope_spec = pl.BlockSpec((ROW_TILE, LANES), lambda t: (geom.rope_blk(t), 0))
    in_specs = ([pl.BlockSpec((ROW_TILE, d), tile),
                 pl.BlockSpec((None, 1, N_MOD * d), lambda t: (geom.mod_row(t), 0, 0)),
                 _const_spec(g8.shape)]
                + [_const_spec(c.shape) for c in consts]
                + [rope_spec] * 4)
    blk = lambda n: pl.BlockSpec((n, ROW_TILE, LANES), lambda t: (0, t, 0))
    shp = lambda n: jax.ShapeDtypeStruct((n, geom.n_rows, LANES), BF16)
    return pl.pallas_call(
        kern,
        grid=(geom.n_tiles,),
        in_specs=in_specs,
        out_specs=[blk(nq), blk(nk), blk(nv)],
        out_shape=[shp(nq), shp(nk), shp(nv)],
        compiler_params=_cparams(1),
        name=name,
    )(h, mod, g8, *consts, *tables)


def _qk(qs, k):
    return lax.dot_general(qs, k, (((1,), (1,)), ((), ())), preferred_element_type=F32)


def _pv_with_sums(p, v):
    v_ext = jnp.concatenate([v, jnp.ones_like(v)], axis=1)
    return jnp.dot(p, v_ext, preferred_element_type=F32)


def _flash_stream(jobs):
    steps = [(j, c) for j, (_, chunks, _) in enumerate(jobs) for c in range(len(chunks))]
    qs_of = {}

    def scores(j, c):
        if j not in qs_of:
            qs_of[j] = jobs[j][0]()
        k_ref, hk, _, _, start, size = jobs[j][1][c]
        return _qk(qs_of[j], k_ref[hk, pl.ds(start, size), :])

    m = acc = None
    s_next = scores(*steps[0])
    for idx, (j, c) in enumerate(steps):
        s = s_next
        if idx + 1 < len(steps):
            s_next = scores(*steps[idx + 1])
        _, _, v_ref, hv, start, size = jobs[j][1][c]
        mc = jnp.max(s, axis=1, keepdims=True)
        m_new = mc if c == 0 else jnp.maximum(m, mc)
        p = jnp.exp2(s - m_new).astype(BF16)
        pv = _pv_with_sums(p, v_ref[hv, pl.ds(start, size), :])
        acc = pv if c == 0 else jnp.exp2(m - m_new) * acc + pv
        m = m_new
        if c == len(jobs[j][1]) - 1:
            qs_of.pop(j)
            jobs[j][2](acc[:, :LANES], acc[:, LANES:])


def _kv_chunks(kc_ref, vc_ref, kl_ref, vl_ref, hk, hv, n_ctx_keys, n_lat_keys, tk):
    chunks = [(kc_ref, hk, vc_ref, hv, 0, n_ctx_keys)]
    if kl_ref is not None:
        chunks += [(kl_ref, hk, vl_ref, hv, j * tk, tk) for j in range(n_lat_keys // tk)]
    return chunks


def _split_refs(refs, n_lead, has_lat):
    lead = refs[:n_lead]
    rest = refs[n_lead:]
    q_ref, kc_ref, vc_ref = rest[:3]
    if has_lat:
        kl_ref, vl_ref, o_ref = rest[3:6]
    else:
        kl_ref = vl_ref = None
        o_ref = rest[3]
    return lead, q_ref, kc_ref, vc_ref, kl_ref, vl_ref, o_ref


def _lane_lo(rows):
    lane = lax.broadcasted_iota(jnp.int32, (rows, LANES), 1)
    return (lane % HALF) < (HALF // 2)


def _attn_diff_kernel(*refs, has_lat, n_heads, n_ctx_keys, n_lat_keys, tk, lam_init):
    (lam_ref, subln_ref), q_ref, kc_ref, vc_ref, kl_ref, vl_ref, o_ref = _split_refs(refs, 2, has_lat)
    tq = q_ref.shape[1]
    lo = _lane_lo(tq)
    lf = lam_ref[...]
    lam = (jnp.exp(jnp.sum(lf[0:1] * lf[1:2], keepdims=True))
           - jnp.exp(jnp.sum(lf[2:3] * lf[3:4], keepdims=True)) + lam_init)
    subln = subln_ref[...]

    def job(h):
        def make_qs():
            q = q_ref[h]
            zero = jnp.zeros_like(q)
            return jnp.concatenate([jnp.where(lo, q, zero), jnp.where(lo, zero, q)], axis=0)

        def done(acc, l):
            o = acc / l
            o = o[:tq] - lam * o[tq:]
            o_ref[h] = (_rms(o, subln) * (1.0 - lam_init)).astype(BF16)

        return make_qs, _kv_chunks(kc_ref, vc_ref, kl_ref, vl_ref, h, h, n_ctx_keys, n_lat_keys, tk), done

    def head_pair(i, carry):
        _flash_stream([job(2 * i), job(2 * i + 1)])
        return carry

    lax.fori_loop(0, n_heads // 2, head_pair, 0)


def _attn_gqa_kernel(*refs, has_lat, grp, n_ctx_keys, n_lat_keys, tk):
    _, q_ref, kc_ref, vc_ref, kl_ref, vl_ref, o_ref = _split_refs(refs, 0, has_lat)
    tq = q_ref.shape[1]
    rows = min(tq, GQA_JOB_ROWS)

    def job(r0):
        def done(acc, l):
            o = acc / l
            for j in range(grp):
                o_ref[j, r0:r0 + rows, :] = o[j * rows:(j + 1) * rows].astype(BF16)

        return ((lambda: jnp.concatenate([q_ref[j, r0:r0 + rows, :] for j in range(grp)], axis=0)),
                _kv_chunks(kc_ref, vc_ref, kl_ref, vl_ref, 0, 0, n_ctx_keys, n_lat_keys, tk), done)

    _flash_stream([job(r0) for r0 in range(0, tq, rows)])


def _attn_mla_kernel(*refs, has_lat, n_pairs, n_ctx_keys, n_lat_keys, tk):
    _, q_ref, kc_ref, vc_ref, kl_ref, vl_ref, o_ref = _split_refs(refs, 0, has_lat)
    tq = q_ref.shape[1]
    lane = lax.broadcasted_iota(jnp.int32, (tq, LANES), 1)

    def pair(p, carry):
        even_out = []

        def job(e):
            h = 2 * p + e

            def done(acc, l):
                if e == 0:
                    even_out.append(acc / l)
                else:
                    o_ref[p] = jnp.where(lane < HALF, even_out[0], acc / l).astype(BF16)

            return ((lambda: q_ref[h]),
                    _kv_chunks(kc_ref, vc_ref, kl_ref, vl_ref, h, p, n_ctx_keys, n_lat_keys, tk), done)

        _flash_stream([job(0), job(1)])
        return carry

    lax.fori_loop(0, n_pairs, pair, 0)


def _attn_swa_kernel(sink_ref, q_ref, kc_ref, vc_ref, kl_ref, vl_ref, o_ref, *, n_pairs, pairs_per_stage, sub_rows, seq):
    tq = sub_rows
    win = tq + 2 * WINDOW
    n_sub = q_ref.shape[1] // tq
    i = pl.program_id(2)
    lo = _lane_lo(tq)
    zero = jnp.zeros((tq, LANES), BF16)
    n_stage = n_pairs // pairs_per_stage
    heads = 2 * pairs_per_stage

    def values_and_ones(v):
        lane = lax.broadcasted_iota(jnp.int32, v.shape, 1)
        return jnp.where(lane < HALF, v, jnp.ones_like(v))

    k_ctx, v_ctx = kc_ref[0], values_and_ones(vc_ref[0])

    def band(sub):
        q0 = (i * n_sub + sub) * tq
        start = pl.multiple_of(jnp.clip(q0 - WINDOW, 0, seq - win), WINDOW)
        kpos = start + lax.broadcasted_iota(jnp.int32, (win, tq), 0)
        qpos = q0 + lax.broadcasted_iota(jnp.int32, (win, tq), 1)
        bias = jnp.where(jnp.abs(qpos - kpos) <= WINDOW, 0.0, NEG_INF)
        return (jnp.concatenate([bias] * heads, axis=1), kl_ref[0, pl.ds(start, win), :],
                values_and_ones(vl_ref[0, pl.ds(start, win), :]))

    bands = [band(sub) for sub in range(n_sub)]

    stages = [(sub, t) for sub in range(n_sub) for t in range(n_stage)]

    def scores(sub, t):
        bias_cols, k_win, _ = bands[sub]
        q_rows = []
        for j in range(t * pairs_per_stage, (t + 1) * pairs_per_stage):
            q = q_ref[j, sub * tq:(sub + 1) * tq, :]
            q_rows += [jnp.where(lo, q, zero), jnp.where(lo, zero, q)]
        qs = jnp.concatenate(q_rows, axis=0)
        return _qk(k_ctx, qs), _qk(k_win, qs) + bias_cols

    def softmax(sub, t, s_ctx, s_band):
        sink = jnp.concatenate([sink_ref[h:h + 1, :] for h in range(t * heads, (t + 1) * heads)
                                for _ in range(tq // LANES)], axis=1)
        m = jnp.maximum(jnp.maximum(jnp.max(s_ctx, axis=0, keepdims=True),
                                    jnp.max(s_band, axis=0, keepdims=True)), sink)
        return jnp.exp2(s_ctx - m).astype(BF16), jnp.exp2(s_band - m).astype(BF16), jnp.exp2(sink - m)

    def pv_t(v, p):
        return lax.dot_general(v, p, (((0,), (0,)), ((), ())), preferred_element_type=F32)

    def finish(sub, t, p_ctx, p_band, p_sink):
        acc = pv_t(v_ctx, p_ctx) + pv_t(bands[sub][2], p_band)
        o_t = acc[:HALF] / (acc[HALF:HALF + 1] + p_sink)
        for j in range(pairs_per_stage):
            pair_t = jnp.concatenate([o_t[:, 2 * j * tq:(2 * j + 1) * tq],
                                      o_t[:, (2 * j + 1) * tq:(2 * j + 2) * tq]], axis=0)
            o_ref[t * pairs_per_stage + j, sub * tq:(sub + 1) * tq, :] = pair_t.T.astype(BF16)

    s_vals, p_vals = {}, {}
    for n in range(len(stages) + 2):
        if n < len(stages):
            s_vals[n] = scores(*stages[n])
        if 0 <= n - 1 < len(stages):
            p_vals[n - 1] = softmax(*stages[n - 1], *s_vals.pop(n - 1))
        if 0 <= n - 2 < len(stages):
            finish(*stages[n - 2], *p_vals.pop(n - 2))


def _attn_call(geom, kern, lead, lead_specs, q, k, v, *, has_lat, tq, q_per, k_per, v_per, o_per, n_groups, name):
    b, s, c = geom.batch, geom.seq, geom.ctx
    ctx_blk0 = geom.n_lat_rows // c
    if has_lat:
        n_q = s // tq
        q_map = lambda bi, g, i: (g, bi * n_q + i, 0)
        o_rows = geom.n_lat_rows
    else:
        assert tq == c
        n_q = 1
        q_map = lambda bi, g, i: (g, ctx_blk0 + bi, 0)
        o_rows = b * c
    o_map = lambda bi, g, i: (g, bi * n_q + i, 0)
    ctx_map = lambda bi, g, i: (g, ctx_blk0 + bi, 0)
    lat_map = lambda bi, g, i: (g, bi, 0)
    in_specs = list(lead_specs) + [pl.BlockSpec((q_per, tq, LANES), q_map),
                                   pl.BlockSpec((k_per, c, LANES), ctx_map),
                                   pl.BlockSpec((v_per, c, LANES), ctx_map)]
    args = list(lead) + [q, k, v]
    if has_lat:
        in_specs += [pl.BlockSpec((k_per, s, LANES), lat_map), pl.BlockSpec((v_per, s, LANES), lat_map)]
        args += [k, v]
    return pl.pallas_call(
        kern,
        grid=(b, n_groups, n_q),
        in_specs=in_specs,
        out_specs=pl.BlockSpec((o_per, tq, LANES), o_map),
        out_shape=jax.ShapeDtypeStruct((o_per * n_groups, o_rows, LANES), BF16),
        compiler_params=_cparams(3),
        name=name,
    )(*args)


def _rope_tables(seq, rot_dim, aidx, active, scale):
    scale = scale * LOG2E
    pos = np.arange(seq)
    row = jnp.asarray(pos // GRID_W, F32)
    col = jnp.asarray(pos % GRID_W, F32)
    n_axis = rot_dim // 4
    freqs = ROPE_THETA ** (-jnp.arange(n_axis, dtype=F32) / n_axis)
    ang = jnp.concatenate([row[:, None] * freqs, col[:, None] * freqs], axis=-1)
    a = ang[:, np.asarray(aidx)]
    act = jnp.asarray(active)[None, :]
    sign = jnp.asarray(np.where(np.arange(LANES) < HALF, -1.0, 1.0), F32)[None, :]
    c = jnp.where(act, jnp.cos(a), 1.0)
    sn = jnp.where(act, jnp.sin(a) * sign, 0.0)
    c = jnp.concatenate([c, jnp.ones((ROW_TILE, LANES), F32)], axis=0)
    sn = jnp.concatenate([sn, jnp.zeros((ROW_TILE, LANES), F32)], axis=0)
    return (c * scale, sn * scale), (c, sn)


def _take_cols(w, idx):
    idx = np.asarray(idx)
    out = jnp.take(w, jnp.asarray(np.maximum(idx, 0)), axis=1)
    if (idx < 0).any():
        out = jnp.where(jnp.asarray(idx >= 0)[None, :], out, 0.0)
    return out


def _eo(d):
    return np.concatenate([np.arange(0, d, 2), np.arange(1, d, 2)])


def _pair_layout():
    ev, od = np.arange(0, 64, 2), np.arange(1, 64, 2)
    which = np.concatenate([np.zeros(32), np.ones(32), np.zeros(32), np.ones(32)]).astype(np.int64)
    dim = np.concatenate([ev, ev, od, od])
    return which, dim


def kernel(x, c, ctx, c_ctx, ada_w, ada_b, norm_g, ffn_w_in, ffn_w_out, da_w_in, da_lambda, da_subln, da_w_out,
           ga_w_in, ga_q_norm, ga_k_norm, ga_w_out, mla_w_in, mla_q_norm, mla_kv_norm, mla_w_uq, mla_w_ukv,
           mla_w_out, swa_w_in, swa_sink, swa_w_out):
    batch, seq, d = x.shape
    n_ctx = ctx.shape[1]
    depth = ada_w.shape[0]
    assert d == D_MODEL and seq % FFN_TILE == 0 and (batch * n_ctx) % FFN_TILE == 0
    assert n_ctx % V7X_MXU_WIDTH == 0 and batch < MOD_ROWS and depth == 4
    assert all(seq % t == 0 for t in (ATTN_KEY_CHUNK, DIFF_Q_TILE, MLA_Q_TILE, 2 * GQA_JOB_ROWS,
                                    SWA_SUBTILES * SWA_Q_TILE))
    geom = _Geom(batch, seq, n_ctx, ROW_TILE)
    fgeom = _Geom(batch, seq, n_ctx, FFN_TILE)

    cond = jnp.zeros((MOD_ROWS, d), F32).at[:batch].set(c).at[batch].set(c_ctx)
    mods = _ada_call(cond, ada_w, ada_b).reshape(depth, MOD_ROWS, 1, N_MOD * d)

    h = (x.reshape(batch * seq, d), ctx.reshape(batch * n_ctx, d))
    g8 = jnp.concatenate([norm_g, jnp.ones((depth, 2, d), F32)], axis=1)
    w_in_b = ffn_w_in.astype(BF16)
    w_out_b = ffn_w_out.astype(BF16)

    which, pdim = _pair_layout()
    tk = ATTN_KEY_CHUNK

    for i in range(depth):
        mod = mods[i]
        last = i == depth - 1
        h = _ffn_call(fgeom, h, mod, g8[i], w_in_b, w_out_b, i, 0, k=0, gi=0, name=f"ffn_a{i}")

        if i == 0:
            w = da_w_in[0]
            hq = DA_HEADS * 2 * DA_QK
            qcols = np.concatenate([hd * 128 + which * 64 + pdim for hd in range(DA_HEADS)])
            wcat = jnp.concatenate([_take_cols(w, qcols), _take_cols(w, hq + qcols), w[:, 2 * hq:]], axis=1)
            tq_tab, tk_tab = _rope_tables(seq, DA_QK, np.arange(LANES) % 32, np.ones(LANES, bool), DA_QK ** -0.5)
            q, k, v = _proj_call(geom, functools.partial(_proj0_kernel, nq=8, nk=8, nv=8), h, mod, g8[i],
                                 [wcat.astype(BF16)], [*tq_tab, *tk_tab], 8, 8, 8, name="proj_diff")
            lam_init = 0.8 - 0.6 * math.exp(-0.3 * i)
            lead = [da_lambda[0], da_subln[0].reshape(1, LANES)]
            lead_specs = [pl.BlockSpec((4, DA_QK), lambda bi, g, qi: (0, 0)),
                          pl.BlockSpec((1, LANES), lambda bi, g, qi: (0, 0))]
            hs = DIFF_HEADS_PER_STEP
            mk = lambda has_lat: functools.partial(_attn_diff_kernel, has_lat=has_lat, n_heads=hs, n_ctx_keys=n_ctx,
                                                   n_lat_keys=seq, tk=tk, lam_init=lam_init)
            o_lat = _attn_call(geom, mk(True), lead, lead_specs, q, k, v, has_lat=True, tq=DIFF_Q_TILE, q_per=hs, k_per=hs,
                               v_per=hs, o_per=hs, n_groups=DA_HEADS // hs, name="attn_diff")
            o_ctx = _attn_call(geom, mk(False), lead, lead_specs, q, k, v, has_lat=False, tq=n_ctx, q_per=hs,
                               k_per=hs, v_per=hs, o_per=hs, n_groups=DA_HEADS // hs, name="attn_diff_ctx")
            w_mo = da_w_out[0]
        elif i == 1:
            w = ga_w_in[0]
            eo = _eo(GA_DIM)
            qcols = np.concatenate([hd * GA_DIM + eo for hd in range(GA_HEADS)])
            kcols = GA_HEADS * GA_DIM + np.concatenate([hd * GA_DIM + eo for hd in range(GA_KV)])
            wcat = jnp.concatenate([_take_cols(w, qcols), _take_cols(w, kcols), w[:, (GA_HEADS + GA_KV) * GA_DIM:]],
                                   axis=1)
            tq_tab, tk_tab = _rope_tables(seq, GA_DIM, np.arange(LANES) % 64, np.ones(LANES, bool), GA_DIM ** -0.5)
            consts = [wcat.astype(BF16), ga_q_norm[0][eo].reshape(1, LANES), ga_k_norm[0][eo].reshape(1, LANES)]
            q, k, v = _proj_call(geom, functools.partial(_proj1_kernel, nq=8, nk=2, nv=2), h, mod, g8[i],
                                 consts, [*tq_tab, *tk_tab], 8, 2, 2, name="proj_gqa")
            grp = GA_HEADS // GA_KV
            mk = lambda has_lat: functools.partial(_attn_gqa_kernel, has_lat=has_lat, grp=grp, n_ctx_keys=n_ctx,
                                                   n_lat_keys=seq, tk=tk)
            o_lat = _attn_call(geom, mk(True), [], [], q, k, v, has_lat=True, tq=2 * GQA_JOB_ROWS, q_per=grp,
                               k_per=1, v_per=1, o_per=grp, n_groups=GA_KV, name="attn_gqa")
            o_ctx = _attn_call(geom, mk(False), [], [], q, k, v, has_lat=False, tq=n_ctx, q_per=grp, k_per=1,
                               v_per=1, o_per=grp, n_groups=GA_KV, name="attn_gqa_ctx")
            w_mo = ga_w_out[0]
        elif i == 2:
            lanes = np.arange(LANES)
            is_rope = ((lanes % HALF) >= 32) & ((lanes % HALF) < 48)
            is_nope = (lanes % HALF) < 32
            rope_dim = 2 * ((lanes % HALF) - 32) + (lanes >= HALF)
            nope_dim = (lanes % HALF) + 32 * (lanes >= HALF)
            w = mla_w_in[0]
            kr_cols = np.where(is_rope, MLA_Q_RANK + MLA_KV_RANK + rope_dim, -1)
            w1 = jnp.concatenate([w[:, :MLA_Q_RANK + MLA_KV_RANK], _take_cols(w, kr_cols)], axis=1)
            qd = MLA_NOPE + MLA_ROPE
            q_blk = np.where(is_nope, nope_dim, np.where(is_rope, MLA_NOPE + rope_dim, -1))
            q_cols = np.concatenate([np.where(q_blk >= 0, hd * qd + q_blk, -1) for hd in range(MLA_HEADS)])
            kvd = MLA_NOPE + MLA_V
            k_blk = np.where(is_nope, nope_dim, -1)
            k_cols = np.concatenate([np.where(k_blk >= 0, hd * kvd + k_blk, -1) for hd in range(MLA_HEADS)])
            v_cols = np.concatenate([hd * kvd + MLA_NOPE + np.arange(MLA_V) for hd in range(MLA_HEADS)])
            consts = [w1.astype(BF16), mla_q_norm[0].reshape(1, -1), mla_kv_norm[0].reshape(1, -1),
                      _take_cols(mla_w_uq[0], q_cols).astype(BF16), _take_cols(mla_w_ukv[0], k_cols).astype(BF16),
                      _take_cols(mla_w_ukv[0], v_cols).astype(BF16)]
            aidx = np.where(is_rope, (lanes % HALF) - 32, 0)
            tq_tab, tk_tab = _rope_tables(seq, MLA_ROPE, aidx, is_rope, (MLA_NOPE + MLA_ROPE) ** -0.5)
            q, k, v = _proj_call(geom, _proj2_kernel, h, mod, g8[i], consts, [*tq_tab, *tk_tab],
                                 MLA_HEADS, MLA_HEADS, MLA_HEADS // 2, name="proj_mla")
            n_pairs = MLA_PAIRS_PER_STEP
            mk = lambda has_lat: functools.partial(_attn_mla_kernel, has_lat=has_lat, n_pairs=n_pairs,
                                                   n_ctx_keys=n_ctx, n_lat_keys=seq, tk=tk)
            o_lat = _attn_call(geom, mk(True), [], [], q, k, v, has_lat=True, tq=MLA_Q_TILE, q_per=2 * n_pairs,
                               k_per=2 * n_pairs, v_per=n_pairs, o_per=n_pairs,
                               n_groups=MLA_HEADS // (2 * n_pairs), name="attn_mla")
            o_ctx = _attn_call(geom, mk(False), [], [], q, k, v, has_lat=False, tq=n_ctx, q_per=2 * n_pairs,
                               k_per=2 * n_pairs, v_per=n_pairs, o_per=n_pairs,
                               n_groups=MLA_HEADS // (2 * n_pairs), name="attn_mla_ctx")
            w_mo = mla_w_out[0]
        else:
            w = swa_w_in[0]
            hq = SWA_HEADS * SWA_DIM
            qcols = np.concatenate([(2 * p + which) * SWA_DIM + pdim for p in range(SWA_HEADS // 2)])
            kcols = np.concatenate([hq + g * SWA_DIM + pdim for g in range(SWA_KV)])
            vcols = np.concatenate([hq + SWA_KV * SWA_DIM + g * SWA_DIM + (np.arange(LANES) % SWA_DIM)
                                    for g in range(SWA_KV)])
            wcat = jnp.concatenate([_take_cols(w, qcols), _take_cols(w, kcols), _take_cols(w, vcols)], axis=1)
            tq_tab, tk_tab = _rope_tables(seq, SWA_DIM, np.arange(LANES) % 32, np.ones(LANES, bool), SWA_DIM ** -0.5)
            q, k, v = _proj_call(geom, functools.partial(_proj0_kernel, nq=8, nk=2, nv=2), h, mod, g8[i],
                                 [wcat.astype(BF16)], [*tq_tab, *tk_tab], 8, 2, 2, name="proj_swa")
            hpg = SWA_HEADS // SWA_KV
            sink = jnp.broadcast_to((swa_sink[0].astype(F32) * LOG2E).reshape(SWA_KV, hpg, 1), (SWA_KV, hpg, LANES))
            tq = SWA_SUBTILES * SWA_Q_TILE
            kern = functools.partial(_attn_swa_kernel, n_pairs=hpg // 2, pairs_per_stage=SWA_PAIRS_PER_STAGE,
                                     sub_rows=SWA_Q_TILE, seq=seq)
            o_lat = _attn_call(geom, kern, [sink], [pl.BlockSpec((None, hpg, LANES), lambda bi, g, qi: (g, 0, 0))],
                               q, k, v, has_lat=True, tq=tq, q_per=hpg // 2, k_per=1, v_per=1, o_per=hpg // 2,
                               n_groups=SWA_KV, name="attn_swa")
            o_ctx = None
            w_mo = swa_w_out[0]

        h = _ffn_call(fgeom, h, mod, g8[i], w_in_b, w_out_b, i, 1, k=6, gi=4,
                      pre=(o_lat, o_ctx, w_mo.astype(BF16)), n_tiles=fgeom.n_lat if last else None,
                      name=f"ffn_b{i}")
    return h.reshape(batch, seq, d)
```

```python
import functools
import math

import numpy as np
import jax
import jax.numpy as jnp
from jax import lax
from jax.experimental import pallas as pl
from jax.experimental.pallas import tpu as pltpu

F32 = jnp.float32
BF16 = jnp.bfloat16

D_MODEL = 1024
D_FF = 2816
N_MOD = 9
NORM_EPS = 1e-6
GRID_W = 64
ROPE_THETA = 10000.0
NEG_INF = -1e30
WINDOW = 128
LOG2E = math.log2(math.e)

LANES = 128
HALF = LANES // 2
V7X_MXU_WIDTH = 256
V7X_VMEM_BYTES = 64 * 1024 * 1024
ROW_TILE = 1024
FFN_HALF = 512
FFN_TILE = 2 * FFN_HALF
FF_CHUNK = V7X_MXU_WIDTH
ADA_COL_TILE = 1024
MOD_ROWS = 16
ATTN_KEY_CHUNK = 512
GQA_JOB_ROWS = 256
DIFF_Q_TILE, DIFF_HEADS_PER_STEP = 512, 4
MLA_Q_TILE, MLA_PAIRS_PER_STEP = 1024, 4
SWA_Q_TILE, SWA_PAIRS_PER_STAGE, SWA_SUBTILES = 256, 2, 4
VMEM_LIMIT = V7X_VMEM_BYTES * 13 // 16
FFN_VMEM_LIMIT = V7X_VMEM_BYTES * 29 // 32

DA_HEADS, DA_QK = 8, 64
GA_HEADS, GA_KV, GA_DIM = 8, 2, 128
MLA_HEADS, MLA_Q_RANK, MLA_KV_RANK, MLA_NOPE, MLA_ROPE, MLA_V = 16, 256, 128, 64, 32, 64
SWA_HEADS, SWA_KV, SWA_DIM = 16, 2, 64


def _cparams(n_axes, vmem_limit=VMEM_LIMIT, allow_input_fusion=None):
    return pltpu.CompilerParams(dimension_semantics=("arbitrary",) * n_axes,
                                vmem_limit_bytes=vmem_limit, allow_input_fusion=allow_input_fusion)


def _rms(x, g):
    return x * lax.rsqrt(jnp.mean(x * x, axis=-1, keepdims=True) + NORM_EPS) * g


def _mod(mod_ref, k):
    return mod_ref[:, k * D_MODEL:(k + 1) * D_MODEL]


def _modnorm(x, g, mod_ref, k):
    return _rms(x, g) * (1.0 + _mod(mod_ref, k + 1)) + _mod(mod_ref, k)


def _rope(x, c, s):
    return x * c + pltpu.roll(x, HALF, 1) * s


def _rope_blocks(xs, c, s):
    rolled = [pltpu.roll(x, HALF, 1) for x in xs]
    return [x * c + r * s for x, r in zip(xs, rolled)]


def _lane_blocks(x, first, n):
    return [x[:, (first + i) * LANES:(first + i + 1) * LANES] for i in range(n)]


def _ada_kernel(c_ref, w_ref, b_ref, o_ref):
    c = c_ref[...]
    sc = c / (1.0 + jnp.exp(-c))
    o_ref[...] = jnp.dot(sc, w_ref[...], precision=lax.Precision.HIGHEST,
                         preferred_element_type=F32) + b_ref[...]


def _ada_call(cond, ada_w, ada_b):
    depth, d, n = ada_w.shape
    cw = ADA_COL_TILE
    return pl.pallas_call(
        _ada_kernel,
        grid=(depth, n // cw),
        in_specs=[pl.BlockSpec((MOD_ROWS, d), lambda l, j: (0, 0)),
                  pl.BlockSpec((None, d, cw), lambda l, j: (l, 0, j)),
                  pl.BlockSpec((None, 1, cw), lambda l, j: (l, 0, j))],
        out_specs=pl.BlockSpec((None, MOD_ROWS, cw), lambda l, j: (l, 0, j)),
        out_shape=jax.ShapeDtypeStruct((depth, MOD_ROWS, n), F32),
        compiler_params=_cparams(2),
        name="adaln",
    )(cond, ada_w, ada_b.reshape(depth, 1, n))


class _Geom:
    def __init__(self, batch, seq, ctx, tile):
        self.batch, self.seq, self.ctx, self.tile = batch, seq, ctx, tile
        self.n_lat_rows = batch * seq
        self.n_rows = batch * (seq + ctx)
        self.n_lat = self.n_lat_rows // tile
        self.n_tiles = self.n_rows // tile
        self.tiles_per_batch = seq // tile

    def mod_row(self, t):
        return jnp.where(t < self.n_lat, t // self.tiles_per_batch, self.batch)

    def rope_blk(self, t):
        return jnp.where(t < self.n_lat, t % self.tiles_per_batch, self.tiles_per_batch)


def _const_spec(shape):
    nd = len(shape)
    return pl.BlockSpec(shape, lambda *_: (0,) * nd, pipeline_mode=pl.Buffered(1))


def _ffn_kernel(*refs, k, gi, n_lat, split_in, has_pre, has_ctx):
    it = iter(refs)
    x_ref = next(it)
    xc_ref = next(it) if split_in else None
    mod_ref, g_ref, win_ref, wout_ref = (next(it) for _ in range(4))
    if has_pre:
        ol_ref = next(it)
        oc_ref = next(it) if has_ctx else None
        wmo_ref = next(it)
    out_ref = next(it)
    is_lat = pl.program_id(0) < n_lat

    def prologue(r0):
        rows = slice(r0, r0 + FFN_HALF)
        x = x_ref[rows, :]
        if split_in:
            x = jnp.where(is_lat, x, xc_ref[rows, :])
        if has_pre:
            nblk = ol_ref.shape[0]
            o = jnp.concatenate([ol_ref[i, rows, :] for i in range(nblk)], axis=1)
            if has_ctx:
                o = jnp.where(is_lat, o, jnp.concatenate([oc_ref[i, rows, :] for i in range(nblk)], axis=1))
            y = jnp.dot(o, wmo_ref[...], preferred_element_type=F32)
            x = x + _mod(mod_ref, 5) * _rms(y, g_ref[3:4, :])
        return x, _modnorm(x, g_ref[gi:gi + 1, :], mod_ref, k).astype(BF16)

    def gate_up(u, c):
        lo = c * FF_CHUNK
        return (jnp.dot(u, win_ref[:, lo:lo + FF_CHUNK], preferred_element_type=F32),
                jnp.dot(u, win_ref[:, D_FF + lo:D_FF + lo + FF_CHUNK], preferred_element_type=F32))

    n_chunks = D_FF // FF_CHUNK
    steps = [(blk, c) for blk in range(FFN_TILE // FFN_HALF) for c in range(n_chunks)]
    xs, us = {}, {}
    xs[0], us[0] = prologue(0)
    nxt = gate_up(us[0], 0)
    xs[1], us[1] = prologue(FFN_HALF)
    acc = None
    for idx, (blk, c) in enumerate(steps):
        a, b = nxt
        if idx + 1 < len(steps):
            nxt = gate_up(us[steps[idx + 1][0]], steps[idx + 1][1])
        act = (a / (1.0 + jnp.exp(-a)) * b).astype(BF16)
        part = jnp.dot(act, wout_ref[c * FF_CHUNK:(c + 1) * FF_CHUNK, :], preferred_element_type=F32)
        acc = part if c == 0 else acc + part
        if c == n_chunks - 1:
            r0 = blk * FFN_HALF
            out_ref[r0:r0 + FFN_HALF, :] = (xs.pop(blk)
                                            + 0.5 * _mod(mod_ref, k + 2) * _rms(acc, g_ref[gi + 1:gi + 2, :]))


def _ffn_call(geom, h, mod, g8, w_in, w_out, layer, half, *, k, gi, pre=None, n_tiles=None, name):
    n_tiles = geom.n_tiles if n_tiles is None else n_tiles
    d = D_MODEL
    w_spec = lambda w: pl.BlockSpec((None, None) + w.shape[2:], lambda t: (layer, half, 0, 0),
                                    pipeline_mode=pl.Buffered(1))
    split_in = isinstance(h, tuple)
    if split_in:
        in_specs = [pl.BlockSpec((FFN_TILE, d), lambda t: (jnp.minimum(t, geom.n_lat - 1), 0)),
                    pl.BlockSpec((FFN_TILE, d), lambda t: (jnp.maximum(t - geom.n_lat, 0), 0))]
        args = list(h)
    else:
        in_specs = [pl.BlockSpec((FFN_TILE, d), lambda t: (t, 0))]
        args = [h]
    in_specs += [pl.BlockSpec((None, 1, N_MOD * d), lambda t: (geom.mod_row(t), 0, 0)),
                 _const_spec(g8.shape), w_spec(w_in), w_spec(w_out)]
    args += [mod, g8, w_in, w_out]
    has_pre = pre is not None
    has_ctx = False
    if has_pre:
        o_lat, o_ctx, w_mo = pre
        nblk = o_lat.shape[0]
        in_specs.append(pl.BlockSpec((nblk, FFN_TILE, LANES),
                                     lambda t: (0, jnp.minimum(t, geom.n_lat - 1), 0)))
        args.append(o_lat)
        has_ctx = o_ctx is not None
        if has_ctx:
            in_specs.append(pl.BlockSpec((nblk, FFN_TILE, LANES),
                                         lambda t: (0, jnp.maximum(t - geom.n_lat, 0), 0)))
            args.append(o_ctx)
        in_specs.append(_const_spec(w_mo.shape))
        args.append(w_mo)
    kern = functools.partial(_ffn_kernel, k=k, gi=gi, n_lat=geom.n_lat, split_in=split_in, has_pre=has_pre,
                             has_ctx=has_ctx)
    return pl.pallas_call(
        kern,
        grid=(n_tiles,),
        in_specs=in_specs,
        out_specs=pl.BlockSpec((FFN_TILE, d), lambda t: (t, 0)),
        out_shape=jax.ShapeDtypeStruct((n_tiles * FFN_TILE, d), F32),
        compiler_params=_cparams(1, FFN_VMEM_LIMIT),
        name=name,
    )(*args)


def _two_halves(*stages):
    halves = [slice(r, r + 256) for r in range(0, ROW_TILE, 256)]
    states = [stages[0](rows) for rows in halves]
    for stage in stages[1:]:
        states = [stage(rows, st) for rows, st in zip(halves, states)]


def _proj0_kernel(x_ref, mod_ref, g_ref, w_ref, cq_ref, sq_ref, ck_ref, sk_ref, q_ref, k_ref, v_ref, *, nq, nk, nv):
    def front(rows):
        u = _modnorm(x_ref[rows, :], g_ref[2:3, :], mod_ref, 3).astype(BF16)
        return jnp.dot(u, w_ref[...], preferred_element_type=F32)

    def back(rows, t):
        for i, blk in enumerate(_rope_blocks(_lane_blocks(t, 0, nq), cq_ref[rows, :], sq_ref[rows, :])):
            q_ref[i, rows, :] = blk.astype(BF16)
        for i, blk in enumerate(_rope_blocks(_lane_blocks(t, nq, nk), ck_ref[rows, :], sk_ref[rows, :])):
            k_ref[i, rows, :] = blk.astype(BF16)
        for i, blk in enumerate(_lane_blocks(t, nq + nk, nv)):
            v_ref[i, rows, :] = blk.astype(BF16)

    _two_halves(front, back)


def _proj1_kernel(x_ref, mod_ref, g_ref, w_ref, qn_ref, kn_ref, cq_ref, sq_ref, ck_ref, sk_ref,
                  q_ref, k_ref, v_ref, *, nq, nk, nv):
    qn, kn = qn_ref[...], kn_ref[...]

    def front(rows):
        u = _modnorm(x_ref[rows, :], g_ref[2:3, :], mod_ref, 3).astype(BF16)
        return jnp.dot(u, w_ref[...], preferred_element_type=F32)

    def back(rows, t):
        cq, sq, ck, sk = cq_ref[rows, :], sq_ref[rows, :], ck_ref[rows, :], sk_ref[rows, :]
        blocks = _lane_blocks(t, 0, nq + nk)
        inv = [lax.rsqrt(jnp.mean(b * b, axis=-1, keepdims=True) + NORM_EPS) for b in blocks]
        normed = [b * r * (qn if i < nq else kn) for i, (b, r) in enumerate(zip(blocks, inv))]
        for i, blk in enumerate(_rope_blocks(normed[:nq], cq, sq)):
            q_ref[i, rows, :] = blk.astype(BF16)
        for i, blk in enumerate(_rope_blocks(normed[nq:], ck, sk)):
            k_ref[i, rows, :] = blk.astype(BF16)
        for i, blk in enumerate(_lane_blocks(t, nq + nk, nv)):
            v_ref[i, rows, :] = blk.astype(BF16)

    _two_halves(front, back)


def _proj2_kernel(x_ref, mod_ref, g_ref, w1_ref, qn_ref, kvn_ref, wuq_ref, wuk_ref, wuv_ref,
                  cq_ref, sq_ref, ck_ref, sk_ref, q_ref, k_ref, v_ref):
    u = _modnorm(x_ref[...], g_ref[2:3, :], mod_ref, 3).astype(BF16)
    t = jnp.dot(u, w1_ref[...], preferred_element_type=F32)
    cqv = _rms(t[:, :MLA_Q_RANK], qn_ref[...]).astype(BF16)
    ckv = _rms(t[:, MLA_Q_RANK:MLA_Q_RANK + MLA_KV_RANK], kvn_ref[...]).astype(BF16)
    kr = _rope(t[:, MLA_Q_RANK + MLA_KV_RANK:], ck_ref[...], sk_ref[...])
    qa = jnp.dot(cqv, wuq_ref[...], preferred_element_type=F32)
    for i, blk in enumerate(_rope_blocks(_lane_blocks(qa, 0, MLA_HEADS), cq_ref[...], sq_ref[...])):
        q_ref[i] = blk.astype(BF16)
    ka = jnp.dot(ckv, wuk_ref[...], preferred_element_type=F32)
    for i, blk in enumerate(_lane_blocks(ka, 0, MLA_HEADS)):
        k_ref[i] = (blk + kr).astype(BF16)
    va = jnp.dot(ckv, wuv_ref[...], preferred_element_type=F32)
    for i, blk in enumerate(_lane_blocks(va, 0, MLA_HEADS // 2)):
        v_ref[i] = blk.astype(BF16)


def _proj_call(geom, kern, h, mod, g8, consts, tables, nq, nk, nv, name):
    d = D_MODEL
    tile = lambda t: (t, 0)
    rope_spec = pl.BlockSpec((ROW_TILE, LANES), lambda t: (geom.rope_blk(t), 0))
    in_specs = ([pl.BlockSpec((ROW_TILE, d), tile),
                 pl.BlockSpec((None, 1, N_MOD * d), lambda t: (geom.mod_row(t), 0, 0)),
                 _const_spec(g8.shape)]
                + [_const_spec(c.shape) for c in consts]
                + [rope_spec] * 4)
    blk = lambda n: pl.BlockSpec((n, ROW_TILE, LANES), lambda t: (0, t, 0))
    shp = lambda n: jax.ShapeDtypeStruct((n, geom.n_rows, LANES), BF16)
    return pl.pallas_call(
        kern,
        grid=(geom.n_tiles,),
        in_specs=in_specs,
        out_specs=[blk(nq), blk(nk), blk(nv)],
        out_shape=[shp(nq), shp(nk), shp(nv)],
        compiler_params=_cparams(1, allow_input_fusion=[False] * 3 + [c.dtype == BF16 for c in consts] + [True] * 4),
        name=name,
    )(h, mod, g8, *consts, *tables)


def _qk(qs, k):
    return lax.dot_general(qs, k, (((1,), (1,)), ((), ())), preferred_element_type=F32)


def _pv_with_sums(p, v):
    v_ext = jnp.concatenate([v, jnp.ones_like(v)], axis=1)
    return jnp.dot(p, v_ext, preferred_element_type=F32)


def _flash_stream(jobs):
    steps = [(j, c) for j, (_, chunks, _) in enumerate(jobs) for c in range(len(chunks))]
    qs_of = {}

    def scores(j, c):
        if j not in qs_of:
            qs_of[j] = jobs[j][0]()
        k_ref, hk, _, _, start, size = jobs[j][1][c]
        return _qk(qs_of[j], k_ref[hk, pl.ds(start, size), :])

    m = acc = None
    s_next = scores(*steps[0])
    for idx, (j, c) in enumerate(steps):
        s = s_next
        if idx + 1 < len(steps):
            s_next = scores(*steps[idx + 1])
        _, _, v_ref, hv, start, size = jobs[j][1][c]
        mc = jnp.max(s, axis=1, keepdims=True)
        m_new = mc if c == 0 else jnp.maximum(m, mc)
        p = jnp.exp2(s - m_new).astype(BF16)
        pv = _pv_with_sums(p, v_ref[hv, pl.ds(start, size), :])
        acc = pv if c == 0 else jnp.exp2(m - m_new) * acc + pv
        m = m_new
        if c == len(jobs[j][1]) - 1:
            qs_of.pop(j)
            jobs[j][2](acc[:, :LANES], acc[:, LANES:])


def _kv_chunks(kc_ref, vc_ref, kl_ref, vl_ref, hk, hv, n_ctx_keys, n_lat_keys, tk):
    chunks = [(kc_ref, hk, vc_ref, hv, 0, n_ctx_keys)]
    if kl_ref is not None:
        chunks += [(kl_ref, hk, vl_ref, hv, j * tk, tk) for j in range(n_lat_keys // tk)]
    return chunks


def _split_refs(refs, n_lead, has_lat):
    lead = refs[:n_lead]
    rest = refs[n_lead:]
    q_ref, kc_ref, vc_ref = rest[:3]
    if has_lat:
        kl_ref, vl_ref, o_ref = rest[3:6]
    else:
        kl_ref = vl_ref = None
        o_ref = rest[3]
    return lead, q_ref, kc_ref, vc_ref, kl_ref, vl_ref, o_ref


def _lane_lo(rows):
    lane = lax.broadcasted_iota(jnp.int32, (rows, LANES), 1)
    return (lane % HALF) < (HALF // 2)


def _attn_diff_kernel(*refs, has_lat, n_heads, n_ctx_keys, n_lat_keys, tk, lam_init):
    (lam_ref, subln_ref), q_ref, kc_ref, vc_ref, kl_ref, vl_ref, o_ref = _split_refs(refs, 2, has_lat)
    tq = q_ref.shape[1]
    lo = _lane_lo(tq)
    lf = lam_ref[...]
    lam = (jnp.exp(jnp.sum(lf[0:1] * lf[1:2], keepdims=True))
           - jnp.exp(jnp.sum(lf[2:3] * lf[3:4], keepdims=True)) + lam_init)
    subln = subln_ref[...]

    def job(h):
        def make_qs():
            q = q_ref[h]
            zero = jnp.zeros_like(q)
            return jnp.concatenate([jnp.where(lo, q, zero), jnp.where(lo, zero, q)], axis=0)

        def done(acc, l):
            o = acc / l
            o = o[:tq] - lam * o[tq:]
            o_ref[h] = (_rms(o, subln) * (1.0 - lam_init)).astype(BF16)

        return make_qs, _kv_chunks(kc_ref, vc_ref, kl_ref, vl_ref, h, h, n_ctx_keys, n_lat_keys, tk), done

    def head_pair(i, carry):
        _flash_stream([job(2 * i), job(2 * i + 1)])
        return carry

    lax.fori_loop(0, n_heads // 2, head_pair, 0)


def _attn_gqa_kernel(*refs, has_lat, grp, n_ctx_keys, n_lat_keys, tk):
    _, q_ref, kc_ref, vc_ref, kl_ref, vl_ref, o_ref = _split_refs(refs, 0, has_lat)
    tq = q_ref.shape[1]
    rows = min(tq, GQA_JOB_ROWS)

    def job(r0):
        def done(acc, l):
            o = acc / l
            for j in range(grp):
                o_ref[j, r0:r0 + rows, :] = o[j * rows:(j + 1) * rows].astype(BF16)

        return ((lambda: jnp.concatenate([q_ref[j, r0:r0 + rows, :] for j in range(grp)], axis=0)),
                _kv_chunks(kc_ref, vc_ref, kl_ref, vl_ref, 0, 0, n_ctx_keys, n_lat_keys, tk), done)

    _flash_stream([job(r0) for r0 in range(0, tq, rows)])


def _attn_mla_kernel(*refs, has_lat, n_pairs, n_ctx_keys, n_lat_keys, tk):
    _, q_ref, kc_ref, vc_ref, kl_ref, vl_ref, o_ref = _split_refs(refs, 0, has_lat)
    tq = q_ref.shape[1]
    lane = lax.broadcasted_iota(jnp.int32, (tq, LANES), 1)

    def pair(p, carry):
        even_out = []

        def job(e):
            h = 2 * p + e

            def done(acc, l):
                if e == 0:
                    even_out.append(acc / l)
                else:
                    o_ref[p] = jnp.where(lane < HALF, even_out[0], acc / l).astype(BF16)

            return ((lambda: q_ref[h]),
                    _kv_chunks(kc_ref, vc_ref, kl_ref, vl_ref, h, p, n_ctx_keys, n_lat_keys, tk), done)

        _flash_stream([job(0), job(1)])
        return carry

    lax.fori_loop(0, n_pairs, pair, 0)


def _attn_swa_kernel(sink_ref, q_ref, kc_ref, vc_ref, kl_ref, vl_ref, o_ref, *, n_pairs, pairs_per_stage, sub_rows, seq):
    tq = sub_rows
    win = tq + 2 * WINDOW
    n_sub = q_ref.shape[1] // tq
    i = pl.program_id(2)
    lo = _lane_lo(tq)
    zero = jnp.zeros((tq, LANES), BF16)
    n_stage = n_pairs // pairs_per_stage
    heads = 2 * pairs_per_stage

    def values_and_ones(v):
        lane = lax.broadcasted_iota(jnp.int32, v.shape, 1)
        return jnp.where(lane < HALF, v, jnp.ones_like(v))

    k_ctx, v_ctx = kc_ref[0], values_and_ones(vc_ref[0])

    def band(sub):
        q0 = (i * n_sub + sub) * tq
        start = pl.multiple_of(jnp.clip(q0 - WINDOW, 0, seq - win), WINDOW)
        kpos = start + lax.broadcasted_iota(jnp.int32, (win, tq), 0)
        qpos = q0 + lax.broadcasted_iota(jnp.int32, (win, tq), 1)
        bias = jnp.where(jnp.abs(qpos - kpos) <= WINDOW, 0.0, NEG_INF)
        return (jnp.concatenate([bias] * heads, axis=1), kl_ref[0, pl.ds(start, win), :],
                values_and_ones(vl_ref[0, pl.ds(start, win), :]))

    bands = [band(sub) for sub in range(n_sub)]

    stages = [(sub, t) for sub in range(n_sub) for t in range(n_stage)]

    def scores(sub, t):
        bias_cols, k_win, _ = bands[sub]
        q_rows = []
        for j in range(t * pairs_per_stage, (t + 1) * pairs_per_stage):
            q = q_ref[j, sub * tq:(sub + 1) * tq, :]
            q_rows += [jnp.where(lo, q, zero), jnp.where(lo, zero, q)]
        qs = jnp.concatenate(q_rows, axis=0)
        return _qk(k_ctx, qs), _qk(k_win, qs) + bias_cols

    def softmax(sub, t, s_ctx, s_band):
        sink = jnp.concatenate([sink_ref[h:h + 1, :] for h in range(t * heads, (t + 1) * heads)
                                for _ in range(tq // LANES)], axis=1)
        m = jnp.maximum(jnp.maximum(jnp.max(s_ctx, axis=0, keepdims=True),
                                    jnp.max(s_band, axis=0, keepdims=True)), sink)
        return jnp.exp2(s_ctx - m).astype(BF16), jnp.exp2(s_band - m).astype(BF16), jnp.exp2(sink - m)

    def pv_t(v, p):
        return lax.dot_general(v, p, (((0,), (0,)), ((), ())), preferred_element_type=F32)

    def finish(sub, t, p_ctx, p_band, p_sink):
        acc = pv_t(v_ctx, p_ctx) + pv_t(bands[sub][2], p_band)
        o_t = acc[:HALF] / (acc[HALF:HALF + 1] + p_sink)
        for j in range(pairs_per_stage):
            pair_t = jnp.concatenate([o_t[:, 2 * j * tq:(2 * j + 1) * tq],
                                      o_t[:, (2 * j + 1) * tq:(2 * j + 2) * tq]], axis=0)
            o_ref[t * pairs_per_stage + j, sub * tq:(sub + 1) * tq, :] = pair_t.T.astype(BF16)

    s_vals, p_vals = {}, {}
    for n in range(len(stages) + 2):
        if n < len(stages):
            s_vals[n] = scores(*stages[n])
        if 0 <= n - 1 < len(stages):
            p_vals[n - 1] = softmax(*stages[n - 1], *s_vals.pop(n - 1))
        if 0 <= n - 2 < len(stages):
            finish(*stages[n - 2], *p_vals.pop(n - 2))


def _attn_call(geom, kern, lead, lead_specs, q, k, v, *, has_lat, tq, q_per, k_per, v_per, o_per, n_groups, name):
    b, s, c = geom.batch, geom.seq, geom.ctx
    ctx_blk0 = geom.n_lat_rows // c
    if has_lat:
        n_q = s // tq
        q_map = lambda bi, g, i: (g, bi * n_q + i, 0)
        o_rows = geom.n_lat_rows
    else:
        assert tq == c
        n_q = 1
        q_map = lambda bi, g, i: (g, ctx_blk0 + bi, 0)
        o_rows = b * c
    o_map = lambda bi, g, i: (g, bi * n_q + i, 0)
    ctx_map = lambda bi, g, i: (g, ctx_blk0 + bi, 0)
    lat_map = lambda bi, g, i: (g, bi, 0)
    in_specs = list(lead_specs) + [pl.BlockSpec((q_per, tq, LANES), q_map),
                                   pl.BlockSpec((k_per, c, LANES), ctx_map),
                                   pl.BlockSpec((v_per, c, LANES), ctx_map)]
    args = list(lead) + [q, k, v]
    if has_lat:
        in_specs += [pl.BlockSpec((k_per, s, LANES), lat_map), pl.BlockSpec((v_per, s, LANES), lat_map)]
        args += [k, v]
    return pl.pallas_call(
        kern,
        grid=(b, n_groups, n_q),
        in_specs=in_specs,
        out_specs=pl.BlockSpec((o_per, tq, LANES), o_map),
        out_shape=jax.ShapeDtypeStruct((o_per * n_groups, o_rows, LANES), BF16),
        compiler_params=_cparams(3),
        name=name,
    )(*args)


def _rope_tables(seq, rot_dim, aidx, active, scale):
    scale = scale * LOG2E
    pos = np.arange(seq)
    row = jnp.asarray(pos // GRID_W, F32)
    col = jnp.asarray(pos % GRID_W, F32)
    n_axis = rot_dim // 4
    freqs = ROPE_THETA ** (-jnp.arange(n_axis, dtype=F32) / n_axis)
    ang = jnp.concatenate([row[:, None] * freqs, col[:, None] * freqs], axis=-1)
    a = ang[:, np.asarray(aidx)]
    act = jnp.asarray(active)[None, :]
    sign = jnp.asarray(np.where(np.arange(LANES) < HALF, -1.0, 1.0), F32)[None, :]
    c = jnp.where(act, jnp.cos(a), 1.0)
    sn = jnp.where(act, jnp.sin(a) * sign, 0.0)
    c = jnp.concatenate([c, jnp.ones((ROW_TILE, LANES), F32)], axis=0)
    sn = jnp.concatenate([sn, jnp.zeros((ROW_TILE, LANES), F32)], axis=0)
    return (c * scale, sn * scale), (c, sn)


def _take_cols(w, idx):
    idx = np.asarray(idx)
    out = jnp.take(w, jnp.asarray(np.maximum(idx, 0)), axis=1)
    if (idx < 0).any():
        out = jnp.where(jnp.asarray(idx >= 0)[None, :], out, 0.0)
    return out


def _eo(d):
    return np.concatenate([np.arange(0, d, 2), np.arange(1, d, 2)])


def _pair_layout():
    ev, od = np.arange(0, 64, 2), np.arange(1, 64, 2)
    which = np.concatenate([np.zeros(32), np.ones(32), np.zeros(32), np.ones(32)]).astype(np.int64)
    dim = np.concatenate([ev, ev, od, od])
    return which, dim


def kernel(x, c, ctx, c_ctx, ada_w, ada_b, norm_g, ffn_w_in, ffn_w_out, da_w_in, da_lambda, da_subln, da_w_out,
           ga_w_in, ga_q_norm, ga_k_norm, ga_w_out, mla_w_in, mla_q_norm, mla_kv_norm, mla_w_uq, mla_w_ukv,
           mla_w_out, swa_w_in, swa_sink, swa_w_out):
    batch, seq, d = x.shape
    n_ctx = ctx.shape[1]
    depth = ada_w.shape[0]
    assert d == D_MODEL and seq % FFN_TILE == 0 and (batch * n_ctx) % FFN_TILE == 0
    assert n_ctx % V7X_MXU_WIDTH == 0 and batch < MOD_ROWS and depth == 4
    assert all(seq % t == 0 for t in (ATTN_KEY_CHUNK, DIFF_Q_TILE, MLA_Q_TILE, 2 * GQA_JOB_ROWS,
                                    SWA_SUBTILES * SWA_Q_TILE))
    geom = _Geom(batch, seq, n_ctx, ROW_TILE)
    fgeom = _Geom(batch, seq, n_ctx, FFN_TILE)

    cond = jnp.zeros((MOD_ROWS, d), F32).at[:batch].set(c).at[batch].set(c_ctx)
    mods = _ada_call(cond, ada_w, ada_b).reshape(depth, MOD_ROWS, 1, N_MOD * d)

    h = (x.reshape(batch * seq, d), ctx.reshape(batch * n_ctx, d))
    g8 = jnp.concatenate([norm_g, jnp.ones((depth, 2, d), F32)], axis=1)
    w_in_b = ffn_w_in.astype(BF16)
    w_out_b = ffn_w_out.astype(BF16)

    which, pdim = _pair_layout()
    tk = ATTN_KEY_CHUNK

    for i in range(depth):
        mod = mods[i]
        last = i == depth - 1
        h = _ffn_call(fgeom, h, mod, g8[i], w_in_b, w_out_b, i, 0, k=0, gi=0, name=f"ffn_a{i}")

        if i == 0:
            w = da_w_in[0]
            hq = DA_HEADS * 2 * DA_QK
            qcols = np.concatenate([hd * 128 + which * 64 + pdim for hd in range(DA_HEADS)])
            wcat = jnp.concatenate([_take_cols(w, qcols), _take_cols(w, hq + qcols), w[:, 2 * hq:]], axis=1)
            tq_tab, tk_tab = _rope_tables(seq, DA_QK, np.arange(LANES) % 32, np.ones(LANES, bool), DA_QK ** -0.5)
            q, k, v = _proj_call(geom, functools.partial(_proj0_kernel, nq=8, nk=8, nv=8), h, mod, g8[i],
                                 [wcat.astype(BF16)], [*tq_tab, *tk_tab], 8, 8, 8, name="proj_diff")
            lam_init = 0.8 - 0.6 * math.exp(-0.3 * i)
            lead = [da_lambda[0], da_subln[0].reshape(1, LANES)]
            lead_specs = [pl.BlockSpec((4, DA_QK), lambda bi, g, qi: (0, 0)),
                          pl.BlockSpec((1, LANES), lambda bi, g, qi: (0, 0))]
            hs = DIFF_HEADS_PER_STEP
            mk = lambda has_lat: functools.partial(_attn_diff_kernel, has_lat=has_lat, n_heads=hs, n_ctx_keys=n_ctx,
                                                   n_lat_keys=seq, tk=tk, lam_init=lam_init)
            o_lat = _attn_call(geom, mk(True), lead, lead_specs, q, k, v, has_lat=True, tq=DIFF_Q_TILE, q_per=hs, k_per=hs,
                               v_per=hs, o_per=hs, n_groups=DA_HEADS // hs, name="attn_diff")
            o_ctx = _attn_call(geom, mk(False), lead, lead_specs, q, k, v, has_lat=False, tq=n_ctx, q_per=hs,
                               k_per=hs, v_per=hs, o_per=hs, n_groups=DA_HEADS // hs, name="attn_diff_ctx")
            w_mo = da_w_out[0]
        elif i == 1:
            w = ga_w_in[0]
            eo = _eo(GA_DIM)
            qcols = np.concatenate([hd * GA_DIM + eo for hd in range(GA_HEADS)])
            kcols = GA_HEADS * GA_DIM + np.concatenate([hd * GA_DIM + eo for hd in range(GA_KV)])
            wcat = jnp.concatenate([_take_cols(w, qcols), _take_cols(w, kcols), w[:, (GA_HEADS + GA_KV) * GA_DIM:]],
                                   axis=1)
            tq_tab, tk_tab = _rope_tables(seq, GA_DIM, np.arange(LANES) % 64, np.ones(LANES, bool), GA_DIM ** -0.5)
            consts = [wcat.astype(BF16), ga_q_norm[0][eo].reshape(1, LANES), ga_k_norm[0][eo].reshape(1, LANES)]
            q, k, v = _proj_call(geom, functools.partial(_proj1_kernel, nq=8, nk=2, nv=2), h, mod, g8[i],
                                 consts, [*tq_tab, *tk_tab], 8, 2, 2, name="proj_gqa")
            grp = GA_HEADS // GA_KV
            mk = lambda has_lat: functools.partial(_attn_gqa_kernel, has_lat=has_lat, grp=grp, n_ctx_keys=n_ctx,
                                                   n_lat_keys=seq, tk=tk)
            o_lat = _attn_call(geom, mk(True), [], [], q, k, v, has_lat=True, tq=2 * GQA_JOB_ROWS, q_per=grp,
                               k_per=1, v_per=1, o_per=grp, n_groups=GA_KV, name="attn_gqa")
            o_ctx = _attn_call(geom, mk(False), [], [], q, k, v, has_lat=False, tq=n_ctx, q_per=grp, k_per=1,
                               v_per=1, o_per=grp, n_groups=GA_KV, name="attn_gqa_ctx")
            w_mo = ga_w_out[0]
        elif i == 2:
            lanes = np.arange(LANES)
            is_rope = ((lanes % HALF) >= 32) & ((lanes % HALF) < 48)
            is_nope = (lanes % HALF) < 32
            rope_dim = 2 * ((lanes % HALF) - 32) + (lanes >= HALF)
            nope_dim = (lanes % HALF) + 32 * (lanes >= HALF)
            w = mla_w_in[0]
            kr_cols = np.where(is_rope, MLA_Q_RANK + MLA_KV_RANK + rope_dim, -1)
            w1 = jnp.concatenate([w[:, :MLA_Q_RANK + MLA_KV_RANK], _take_cols(w, kr_cols)], axis=1)
            qd = MLA_NOPE + MLA_ROPE
            q_blk = np.where(is_nope, nope_dim, np.where(is_rope, MLA_NOPE + rope_dim, -1))
            q_cols = np.concatenate([np.where(q_blk >= 0, hd * qd + q_blk, -1) for hd in range(MLA_HEADS)])
            kvd = MLA_NOPE + MLA_V
            k_blk = np.where(is_nope, nope_dim, -1)
            k_cols = np.concatenate([np.where(k_blk >= 0, hd * kvd + k_blk, -1) for hd in range(MLA_HEADS)])
            v_cols = np.concatenate([hd * kvd + MLA_NOPE + np.arange(MLA_V) for hd in range(MLA_HEADS)])
            consts = [w1.astype(BF16), mla_q_norm[0].reshape(1, -1), mla_kv_norm[0].reshape(1, -1),
                      _take_cols(mla_w_uq[0], q_cols).astype(BF16), _take_cols(mla_w_ukv[0], k_cols).astype(BF16),
                      _take_cols(mla_w_ukv[0], v_cols).astype(BF16)]
            aidx = np.where(is_rope, (lanes % HALF) - 32, 0)
            tq_tab, tk_tab = _rope_tables(seq, MLA_ROPE, aidx, is_rope, (MLA_NOPE + MLA_ROPE) ** -0.5)
            q, k, v = _proj_call(geom, _proj2_kernel, h, mod, g8[i], consts, [*tq_tab, *tk_tab],
                                 MLA_HEADS, MLA_HEADS, MLA_HEADS // 2, name="proj_mla")
            n_pairs = MLA_PAIRS_PER_STEP
            mk = lambda has_lat: functools.partial(_attn_mla_kernel, has_lat=has_lat, n_pairs=n_pairs,
                                                   n_ctx_keys=n_ctx, n_lat_keys=seq, tk=tk)
            o_lat = _attn_call(geom, mk(True), [], [], q, k, v, has_lat=True, tq=MLA_Q_TILE, q_per=2 * n_pairs,
                               k_per=2 * n_pairs, v_per=n_pairs, o_per=n_pairs,
                               n_groups=MLA_HEADS // (2 * n_pairs), name="attn_mla")
            o_ctx = _attn_call(geom, mk(False), [], [], q, k, v, has_lat=False, tq=n_ctx, q_per=2 * n_pairs,
                               k_per=2 * n_pairs, v_per=n_pairs, o_per=n_pairs,
                               n_groups=MLA_HEADS // (2 * n_pairs), name="attn_mla_ctx")
            w_mo = mla_w_out[0]
        else:
            w = swa_w_in[0]
            hq = SWA_HEADS * SWA_DIM
            qcols = np.concatenate([(2 * p + which) * SWA_DIM + pdim for p in range(SWA_HEADS // 2)])
            kcols = np.concatenate([hq + g * SWA_DIM + pdim for g in range(SWA_KV)])
            vcols = np.concatenate([hq + SWA_KV * SWA_DIM + g * SWA_DIM + (np.arange(LANES) % SWA_DIM)
                                    for g in range(SWA_KV)])
            wcat = jnp.concatenate([_take_cols(w, qcols), _take_cols(w, kcols), _take_cols(w, vcols)], axis=1)
            tq_tab, tk_tab = _rope_tables(seq, SWA_DIM, np.arange(LANES) % 32, np.ones(LANES, bool), SWA_DIM ** -0.5)
            q, k, v = _proj_call(geom, functools.partial(_proj0_kernel, nq=8, nk=2, nv=2), h, mod, g8[i],
                                 [wcat.astype(BF16)], [*tq_tab, *tk_tab], 8, 2, 2, name="proj_swa")
            hpg = SWA_HEADS // SWA_KV
            sink = jnp.broadcast_to((swa_sink[0].astype(F32) * LOG2E).reshape(SWA_KV, hpg, 1), (SWA_KV, hpg, LANES))
            tq = SWA_SUBTILES * SWA_Q_TILE
            kern = functools.partial(_attn_swa_kernel, n_pairs=hpg // 2, pairs_per_stage=SWA_PAIRS_PER_STAGE,
                                     sub_rows=SWA_Q_TILE, seq=seq)
            o_lat = _attn_call(geom, kern, [sink], [pl.BlockSpec((None, hpg, LANES), lambda bi, g, qi: (g, 0, 0))],
                               q, k, v, has_lat=True, tq=tq, q_per=hpg // 2, k_per=1, v_per=1, o_per=hpg // 2,
                               n_groups=SWA_KV, name="attn_swa")
            o_ctx = None
            w_mo = swa_w_out[0]

        h = _ffn_call(fgeom, h, mod, g8[i], w_in_b, w_out_b, i, 1, k=6, gi=4,
                      pre=(o_lat, o_ctx, w_mo.astype(BF16)), n_tiles=fgeom.n_lat if last else None,
                      name=f"ffn_b{i}")
    return h.reshape(batch, seq, d)
```
